```python
import jax, jax.numpy as jnp
from jax import lax
import numpy as np

D_MODEL = 1024
BATCH = 8
SEQ = 16384
DEPTH = 2

A_WIDTH = 512
A_GROUPS = 8
A_CHUNK = 128
B_HEADS = 8
B_NOPE = 64
B_ROPE = 32
B_VDIM = 64
B_QK = B_NOPE + B_ROPE
B_Q_LORA = 384
B_KV_LORA = 256
Q_BLOCK = 128
ROPE_BASE = 10000.0
C_HEADS = 8
C_DK = 64
C_DV = 64
C_CONV = 4
C_CHUNK = 64
C_QKV = C_HEADS * (2 * C_DK + C_DV)
C_Z = C_HEADS * C_DV
N_BRANCH = 3
BRANCH_WIDTH = 512
D_FF = 4 * D_MODEL
EPS = 1e-6
SPLIT_SIZES = (A_WIDTH, A_WIDTH, B_Q_LORA, B_KV_LORA, B_ROPE, C_QKV, C_Z, C_HEADS, C_HEADS, N_BRANCH * D_MODEL)
D_IN = 2 * A_WIDTH + B_Q_LORA + B_KV_LORA + B_ROPE + C_QKV + C_Z + 2 * C_HEADS + N_BRANCH * D_MODEL

kernel_name = "hybrid_gated_gmlp_mla_deltanet"


def rms_norm(x, g):
    xf = x.astype(jnp.float32)
    y = xf * lax.rsqrt(jnp.mean(xf * xf, axis=-1, keepdims=True) + EPS)
    return (y * g.astype(jnp.float32)).astype(x.dtype)


def l2_normalize(x):
    xf = x.astype(jnp.float32)
    return xf * lax.rsqrt(jnp.sum(xf * xf, axis=-1, keepdims=True) + EPS)


def rope_tables(positions):
    half = B_ROPE // 2
    inv_freq = 1.0 / (ROPE_BASE ** (jnp.arange(half, dtype=jnp.float32) / half))
    ang = positions.astype(jnp.float32)[..., None] * inv_freq
    return jnp.cos(ang)[:, :, None, :], jnp.sin(ang)[:, :, None, :]


def rope_tail(x, cos, sin):
    half = B_ROPE // 2
    x_pass = x[..., :B_NOPE]
    x1 = x[..., B_NOPE:B_NOPE + half].astype(jnp.float32)
    x2 = x[..., B_NOPE + half:].astype(jnp.float32)
    rot = jnp.concatenate([x1 * cos - x2 * sin, x1 * sin + x2 * cos], axis=-1)
    return jnp.concatenate([x_pass, rot.astype(x.dtype)], axis=-1)


def spatial_gating_branch(a_u, a_v, sgu_norm_g, w_s, b_s):
    b, s, _ = a_u.shape
    n = s // A_CHUNK
    u = jax.nn.gelu(a_u, approximate=False)
    v = rms_norm(jax.nn.gelu(a_v, approximate=False), sgu_norm_g)
    vg = v.reshape(b, n, A_CHUNK, A_GROUPS, A_WIDTH // A_GROUPS)
    tril = jnp.tril(jnp.ones((A_CHUNK, A_CHUNK), dtype=bool))
    w = jnp.where(tril[None], w_s, jnp.zeros_like(w_s))
    sv = jnp.einsum('gts,bnsgc->bntgc', w, vg) + b_s.T[:, :, None]
    return u * sv.reshape(b, s, A_WIDTH)


def causal_block_attention(q, k, v):
    b, s, h, dq = q.shape
    nb = s // Q_BLOCK
    qb = q.reshape(b, nb, Q_BLOCK, h, dq).transpose(1, 0, 2, 3, 4)
    key_pos = jnp.arange(s)
    scale = dq ** -0.5

    def one_block(args):
        q_blk, i = args
        q_pos = i * Q_BLOCK + jnp.arange(Q_BLOCK)
        sc = jnp.einsum('bqhd,bkhd->bhqk', q_blk, k, preferred_element_type=jnp.float32) * scale
        sc = jnp.where(key_pos[None, :] <= q_pos[:, None], sc, -jnp.inf)
        p = jax.nn.softmax(sc, axis=-1)
        return jnp.einsum('bhqk,bkhv->bqhv', p.astype(v.dtype), v)

    out = lax.map(one_block, (qb, jnp.arange(nb)))
    return out.transpose(1, 0, 2, 3, 4).reshape(b, s, h, v.shape[-1])


def mla_branch(q_lat, kv_lat, k_rope, cos, sin, q_lat_norm_g, w_q_up, kv_lat_norm_g, w_kv_up, q_norm_g, k_norm_g):
    b, s, _ = q_lat.shape
    q = (rms_norm(q_lat, q_lat_norm_g) @ w_q_up).reshape(b, s, B_HEADS, B_QK)
    kv = (rms_norm(kv_lat, kv_lat_norm_g) @ w_kv_up).reshape(b, s, B_HEADS, B_NOPE + B_VDIM)
    k_nope, v = kv[..., :B_NOPE], kv[..., B_NOPE:]
    k = jnp.concatenate([k_nope, jnp.broadcast_to(k_rope[:, :, None, :], (b, s, B_HEADS, B_ROPE))], axis=-1)
    q = rope_tail(rms_norm(q, q_norm_g), cos, sin)
    k = rope_tail(rms_norm(k, k_norm_g), cos, sin)
    o = causal_block_attention(q, k, v)
    return o.reshape(b, s, B_HEADS * B_VDIM)


def causal_depthwise_conv(x, w):
    c = x.shape[-1]
    return lax.conv_general_dilated(x, w[:, None, :].astype(x.dtype), window_strides=(1,),
                                    padding=[(C_CONV - 1, 0)], dimension_numbers=('NWC', 'WIO', 'NWC'),
                                    feature_group_count=c)


def chunked_gated_delta_rule(q, k, v, g, beta):
    b, s, h, dk = q.shape
    dv = v.shape[-1]
    n = s // C_CHUNK
    q = q * (dk ** -0.5)

    def chunk(x):
        return x.reshape(b, n, C_CHUNK, h, -1).transpose(0, 3, 1, 2, 4)

    qc, kc, vc = chunk(q), chunk(k), chunk(v)
    gc = jnp.cumsum(g.reshape(b, n, C_CHUNK, h).transpose(0, 3, 1, 2), axis=-1)
    bc = beta.reshape(b, n, C_CHUNK, h).transpose(0, 3, 1, 2)
    tril = jnp.tril(jnp.ones((C_CHUNK, C_CHUNK), dtype=bool))
    strict = jnp.tril(jnp.ones((C_CHUNK, C_CHUNK), dtype=bool), -1)
    diff = gc[..., :, None] - gc[..., None, :]
    decay = jnp.exp(jnp.where(tril, diff, -jnp.inf))
    k_beta = kc * bc[..., None]
    v_beta = vc * bc[..., None]
    lower = jnp.where(strict, jnp.einsum('bhncd,bhnsd->bhncs', k_beta, kc) * decay, 0.0)
    a_mat = lower + jnp.eye(C_CHUNK, dtype=lower.dtype)
    rhs = jnp.concatenate([v_beta, k_beta * jnp.exp(gc)[..., None]], axis=-1)
    sol = lax.linalg.triangular_solve(a_mat, rhs, left_side=True, lower=True, unit_diagonal=True)
    u, w = sol[..., :dv], sol[..., dv:]
    intra = jnp.where(tril, jnp.einsum('bhncd,bhnsd->bhncs', qc, kc) * decay, 0.0)

    xs = tuple(jnp.moveaxis(t, 2, 0) for t in (qc, kc, u, w, intra, gc))

    def step(state, inp):
        q_i, k_i, u_i, w_i, a_i, g_i = inp
        v_new = u_i - jnp.einsum('bhck,bhkv->bhcv', w_i, state)
        o = (jnp.einsum('bhck,bhkv->bhcv', q_i * jnp.exp(g_i)[..., None], state)
             + jnp.einsum('bhcs,bhsv->bhcv', a_i, v_new))
        g_last = g_i[..., -1]
        state = (state * jnp.exp(g_last)[..., None, None]
                 + jnp.einsum('bhck,bhcv->bhkv', k_i * jnp.exp(g_last[..., None] - g_i)[..., None], v_new))
        return state, o

    state0 = jnp.zeros((b, h, dk, dv), dtype=jnp.float32)
    _, o = lax.scan(step, state0, xs)
    return o.transpose(1, 0, 3, 2, 4).reshape(b, s, h, dv)


def gated_deltanet_branch(c_qkv, c_z, c_b, c_a, conv_w, a_log, dt_bias, o_norm_g):
    b, s, _ = c_qkv.shape
    qkv = jax.nn.silu(causal_depthwise_conv(c_qkv, conv_w))
    q, k, v = jnp.split(qkv, [C_HEADS * C_DK, 2 * C_HEADS * C_DK], axis=-1)
    q = l2_normalize(q.reshape(b, s, C_HEADS, C_DK))
    k = l2_normalize(k.reshape(b, s, C_HEADS, C_DK))
    v = v.reshape(b, s, C_HEADS, C_DV).astype(jnp.float32)
    beta = jax.nn.sigmoid(c_b.astype(jnp.float32))
    g = -jnp.exp(a_log.astype(jnp.float32)) * jax.nn.softplus(c_a.astype(jnp.float32) + dt_bias.astype(jnp.float32))
    o = chunked_gated_delta_rule(q, k, v, g, beta)
    o = rms_norm(o, o_norm_g) * jax.nn.silu(c_z.reshape(b, s, C_HEADS, C_DV).astype(jnp.float32))
    return o.reshape(b, s, C_HEADS * C_DV).astype(c_qkv.dtype)


def _fwd_setup_inputs(seed: int = 0) -> dict:
    key = jax.random.key(seed)
    ks = jax.random.split(key, 24)
    f32 = jnp.float32

    def nrm(k, shape, scale):
        return jax.random.normal(k, shape, f32) * scale

    def gain(k, shape):
        return 1.0 + 0.02 * jax.random.normal(k, shape, f32)

    x = jax.random.normal(ks[0], (BATCH, SEQ, D_MODEL), f32)
    offset = jax.random.randint(ks[1], (BATCH, 1), 0, 4096, dtype=jnp.int32)
    positions = offset + jnp.arange(SEQ, dtype=jnp.int32)[None, :]
    dt = jnp.exp(jax.random.uniform(ks[15], (DEPTH, C_HEADS), f32) * (jnp.log(0.1) - jnp.log(0.001)) + jnp.log(0.001))
    return {
        "x": x,
        "positions": positions,
        "norm1_g": gain(ks[2], (DEPTH, D_MODEL)),
        "w_in": nrm(ks[3], (DEPTH, D_MODEL, D_IN), D_MODEL ** -0.5),
        "sgu_norm_g": gain(ks[4], (DEPTH, A_WIDTH)),
        "w_spatial": nrm(ks[5], (DEPTH, A_GROUPS, A_CHUNK, A_CHUNK), A_CHUNK ** -0.5),
        "b_spatial": gain(ks[6], (DEPTH, A_GROUPS, A_CHUNK)),
        "q_lat_norm_g": gain(ks[7], (DEPTH, B_Q_LORA)),
        "w_q_up": nrm(ks[8], (DEPTH, B_Q_LORA, B_HEADS * B_QK), B_Q_LORA ** -0.5),
        "kv_lat_norm_g": gain(ks[9], (DEPTH, B_KV_LORA)),
        "w_kv_up": nrm(ks[10], (DEPTH, B_KV_LORA, B_HEADS * (B_NOPE + B_VDIM)), B_KV_LORA ** -0.5),
        "q_norm_g": gain(ks[11], (DEPTH, B_QK)),
        "k_norm_g": gain(ks[12], (DEPTH, B_QK)),
        "conv_w": nrm(ks[13], (DEPTH, C_CONV, C_QKV), C_CONV ** -0.5),
        "a_log": jnp.log(jax.random.uniform(ks[14], (DEPTH, C_HEADS), f32, 1.0, 16.0)),
        "dt_bias": dt + jnp.log(-jnp.expm1(-dt)),
        "o_norm_g": gain(ks[16], (DEPTH, C_DV)),
        "w_branch": nrm(ks[17], (DEPTH, N_BRANCH, BRANCH_WIDTH, D_MODEL), BRANCH_WIDTH ** -0.5),
        "w_out": nrm(ks[18], (DEPTH, D_MODEL, D_MODEL), D_MODEL ** -0.5),
        "norm2_g": gain(ks[19], (DEPTH, D_MODEL)),
        "w_ff1": nrm(ks[20], (DEPTH, D_MODEL, D_FF), D_MODEL ** -0.5),
        "w_ff2": nrm(ks[21], (DEPTH, D_FF, D_MODEL), D_FF ** -0.5),
    }


def _fwd_reference(x, positions, norm1_g, w_in, sgu_norm_g, w_spatial, b_spatial, q_lat_norm_g, w_q_up,
              kv_lat_norm_g, w_kv_up, q_norm_g, k_norm_g, conv_w, a_log, dt_bias, o_norm_g,
              w_branch, w_out, norm2_g, w_ff1, w_ff2):
    b, s, d = x.shape
    cos, sin = rope_tables(positions)
    split_points = [int(c) for c in np.cumsum(SPLIT_SIZES)[:-1]]
    for l in range(DEPTH):
        h = rms_norm(x, norm1_g[l])
        proj = h @ w_in[l]
        a_u, a_v, q_lat, kv_lat, k_rope, c_qkv, c_z, c_b, c_a, gates = jnp.split(proj, split_points, axis=-1)
        y_a = spatial_gating_branch(a_u, a_v, sgu_norm_g[l], w_spatial[l], b_spatial[l])
        y_b = mla_branch(q_lat, kv_lat, k_rope, cos, sin, q_lat_norm_g[l], w_q_up[l],
                         kv_lat_norm_g[l], w_kv_up[l], q_norm_g[l], k_norm_g[l])
        y_c = gated_deltanet_branch(c_qkv, c_z, c_b, c_a, conv_w[l], a_log[l], dt_bias[l], o_norm_g[l])
        branches = jnp.stack([y_a, y_b, y_c], axis=2)
        y = jnp.einsum('bsnc,ncd->bsnd', branches, w_branch[l])
        gate = jax.nn.sigmoid(gates.reshape(b, s, N_BRANCH, d))
        merged = jnp.einsum('bsnd,bsnd->bsd', gate, y)
        x = x + merged @ w_out[l]
        h2 = rms_norm(x, norm2_g[l])
        x = x + jnp.square(jax.nn.relu(h2 @ w_ff1[l])) @ w_ff2[l]
    return x


import jax as _jax
import jax.numpy as _jnp

TWIN_FORMAT = 'train_step'
FWD_PARAMS = ['x', 'positions', 'norm1_g', 'w_in', 'sgu_norm_g', 'w_spatial', 'b_spatial', 'q_lat_norm_g', 'w_q_up', 'kv_lat_norm_g', 'w_kv_up', 'q_norm_g', 'k_norm_g', 'conv_w', 'a_log', 'dt_bias', 'o_norm_g', 'w_branch', 'w_out', 'norm2_g', 'w_ff1', 'w_ff2']
TWIN_WEIGHTS = ['norm1_g', 'w_in', 'sgu_norm_g', 'w_spatial', 'b_spatial', 'q_lat_norm_g', 'w_q_up', 'kv_lat_norm_g', 'w_kv_up', 'q_norm_g', 'k_norm_g', 'conv_w', 'a_log', 'dt_bias', 'o_norm_g', 'w_branch', 'w_out', 'norm2_g', 'w_ff1', 'w_ff2']
TWIN_DIFF_INPUT = 'x'
TWIN_INPUTS = ['x', 'positions', 'norm1_g', 'w_in', 'sgu_norm_g', 'w_spatial', 'b_spatial', 'q_lat_norm_g', 'w_q_up', 'kv_lat_norm_g', 'w_kv_up', 'q_norm_g', 'k_norm_g', 'conv_w', 'a_log', 'dt_bias', 'o_norm_g', 'w_branch', 'w_out', 'norm2_g', 'w_ff1', 'w_ff2', 'loss_target', 'm_norm1_g', 'm_w_in', 'm_sgu_norm_g', 'm_w_spatial', 'm_b_spatial', 'm_q_lat_norm_g', 'm_w_q_up', 'm_kv_lat_norm_g', 'm_w_kv_up', 'm_q_norm_g', 'm_k_norm_g', 'm_conv_w', 'm_a_log', 'm_dt_bias', 'm_o_norm_g', 'm_w_branch', 'm_w_out', 'm_norm2_g', 'm_w_ff1', 'm_w_ff2', 'v_norm1_g', 'v_w_in', 'v_sgu_norm_g', 'v_w_spatial', 'v_b_spatial', 'v_q_lat_norm_g', 'v_w_q_up', 'v_kv_lat_norm_g', 'v_w_kv_up', 'v_q_norm_g', 'v_k_norm_g', 'v_conv_w', 'v_a_log', 'v_dt_bias', 'v_o_norm_g', 'v_w_branch', 'v_w_out', 'v_norm2_g', 'v_w_ff1', 'v_w_ff2']
TWIN_OUTPUTS = ['loss', 'grad_x', 'grad_norm1_g', 'grad_w_in', 'grad_sgu_norm_g', 'grad_w_spatial', 'grad_b_spatial', 'grad_q_lat_norm_g', 'grad_w_q_up', 'grad_kv_lat_norm_g', 'grad_w_kv_up', 'grad_q_norm_g', 'grad_k_norm_g', 'grad_conv_w', 'grad_a_log', 'grad_dt_bias', 'grad_o_norm_g', 'grad_w_branch', 'grad_w_out', 'grad_norm2_g', 'grad_w_ff1', 'grad_w_ff2', 'delta_norm1_g', 'delta_w_in', 'delta_sgu_norm_g', 'delta_w_spatial', 'delta_b_spatial', 'delta_q_lat_norm_g', 'delta_w_q_up', 'delta_kv_lat_norm_g', 'delta_w_kv_up', 'delta_q_norm_g', 'delta_k_norm_g', 'delta_conv_w', 'delta_a_log', 'delta_dt_bias', 'delta_o_norm_g', 'delta_w_branch', 'delta_w_out', 'delta_norm2_g', 'delta_w_ff1', 'delta_w_ff2', 'new_m_norm1_g', 'new_m_w_in', 'new_m_sgu_norm_g', 'new_m_w_spatial', 'new_m_b_spatial', 'new_m_q_lat_norm_g', 'new_m_w_q_up', 'new_m_kv_lat_norm_g', 'new_m_w_kv_up', 'new_m_q_norm_g', 'new_m_k_norm_g', 'new_m_conv_w', 'new_m_a_log', 'new_m_dt_bias', 'new_m_o_norm_g', 'new_m_w_branch', 'new_m_w_out', 'new_m_norm2_g', 'new_m_w_ff1', 'new_m_w_ff2', 'new_v_norm1_g', 'new_v_w_in', 'new_v_sgu_norm_g', 'new_v_w_spatial', 'new_v_b_spatial', 'new_v_q_lat_norm_g', 'new_v_w_q_up', 'new_v_kv_lat_norm_g', 'new_v_w_kv_up', 'new_v_q_norm_g', 'new_v_k_norm_g', 'new_v_conv_w', 'new_v_a_log', 'new_v_dt_bias', 'new_v_o_norm_g', 'new_v_w_branch', 'new_v_w_out', 'new_v_norm2_g', 'new_v_w_ff1', 'new_v_w_ff2']
TWIN_LEAF_KINDS = {'loss': 'loss', 'grad_x': 'grad_x', 'grad_norm1_g': 'grad_w', 'grad_w_in': 'grad_w', 'grad_sgu_norm_g': 'grad_w', 'grad_w_spatial': 'grad_w', 'grad_b_spatial': 'grad_w', 'grad_q_lat_norm_g': 'grad_w', 'grad_w_q_up': 'grad_w', 'grad_kv_lat_norm_g': 'grad_w', 'grad_w_kv_up': 'grad_w', 'grad_q_norm_g': 'grad_w', 'grad_k_norm_g': 'grad_w', 'grad_conv_w': 'grad_w', 'grad_a_log': 'grad_w', 'grad_dt_bias': 'grad_w', 'grad_o_norm_g': 'grad_w', 'grad_w_branch': 'grad_w', 'grad_w_out': 'grad_w', 'grad_norm2_g': 'grad_w', 'grad_w_ff1': 'grad_w', 'grad_w_ff2': 'grad_w', 'delta_norm1_g': 'delta_w', 'delta_w_in': 'delta_w', 'delta_sgu_norm_g': 'delta_w', 'delta_w_spatial': 'delta_w', 'delta_b_spatial': 'delta_w', 'delta_q_lat_norm_g': 'delta_w', 'delta_w_q_up': 'delta_w', 'delta_kv_lat_norm_g': 'delta_w', 'delta_w_kv_up': 'delta_w', 'delta_q_norm_g': 'delta_w', 'delta_k_norm_g': 'delta_w', 'delta_conv_w': 'delta_w', 'delta_a_log': 'delta_w', 'delta_dt_bias': 'delta_w', 'delta_o_norm_g': 'delta_w', 'delta_w_branch': 'delta_w', 'delta_w_out': 'delta_w', 'delta_norm2_g': 'delta_w', 'delta_w_ff1': 'delta_w', 'delta_w_ff2': 'delta_w', 'new_m_norm1_g': 'new_m', 'new_m_w_in': 'new_m', 'new_m_sgu_norm_g': 'new_m', 'new_m_w_spatial': 'new_m', 'new_m_b_spatial': 'new_m', 'new_m_q_lat_norm_g': 'new_m', 'new_m_w_q_up': 'new_m', 'new_m_kv_lat_norm_g': 'new_m', 'new_m_w_kv_up': 'new_m', 'new_m_q_norm_g': 'new_m', 'new_m_k_norm_g': 'new_m', 'new_m_conv_w': 'new_m', 'new_m_a_log': 'new_m', 'new_m_dt_bias': 'new_m', 'new_m_o_norm_g': 'new_m', 'new_m_w_branch': 'new_m', 'new_m_w_out': 'new_m', 'new_m_norm2_g': 'new_m', 'new_m_w_ff1': 'new_m', 'new_m_w_ff2': 'new_m', 'new_v_norm1_g': 'new_v', 'new_v_w_in': 'new_v', 'new_v_sgu_norm_g': 'new_v', 'new_v_w_spatial': 'new_v', 'new_v_b_spatial': 'new_v', 'new_v_q_lat_norm_g': 'new_v', 'new_v_w_q_up': 'new_v', 'new_v_kv_lat_norm_g': 'new_v', 'new_v_w_kv_up': 'new_v', 'new_v_q_norm_g': 'new_v', 'new_v_k_norm_g': 'new_v', 'new_v_conv_w': 'new_v', 'new_v_a_log': 'new_v', 'new_v_dt_bias': 'new_v', 'new_v_o_norm_g': 'new_v', 'new_v_w_branch': 'new_v', 'new_v_w_out': 'new_v', 'new_v_norm2_g': 'new_v', 'new_v_w_ff1': 'new_v', 'new_v_w_ff2': 'new_v'}


def _forward(args):
    return _fwd_reference(*[args[k] for k in FWD_PARAMS])


def _output_shape():
    def fwd():
        inp = _fwd_setup_inputs(0)
        return _fwd_reference(*[inp[k] for k in FWD_PARAMS])
    out = _jax.eval_shape(fwd)
    return out.shape, out.dtype

N_MICROBATCH = 1
ADAM_LR = 0.001
ADAM_B1 = 0.9
ADAM_B2 = 0.999
ADAM_EPS = 1e-08
ADAM_WD = 0.01
ADAM_STEP = 10
PER_EXAMPLE_BATCH_AXIS = {'x': 0, 'positions': 0, 'loss_target': 0}
SHARED_INPUTS = []
_WEIGHT_DTYPES = {'norm1_g': _jnp.float32, 'w_in': _jnp.float32, 'sgu_norm_g': _jnp.float32, 'w_spatial': _jnp.float32, 'b_spatial': _jnp.float32, 'q_lat_norm_g': _jnp.float32, 'w_q_up': _jnp.float32, 'kv_lat_norm_g': _jnp.float32, 'w_kv_up': _jnp.float32, 'q_norm_g': _jnp.float32, 'k_norm_g': _jnp.float32, 'conv_w': _jnp.float32, 'a_log': _jnp.float32, 'dt_bias': _jnp.float32, 'o_norm_g': _jnp.float32, 'w_branch': _jnp.float32, 'w_out': _jnp.float32, 'norm2_g': _jnp.float32, 'w_ff1': _jnp.float32, 'w_ff2': _jnp.float32}
MOMENT_SCALE = {'norm1_g': 5.061693e+01, 'w_in': 7.840857e+00, 'sgu_norm_g': 1.666771e+01, 'w_spatial': 4.193856e+00, 'b_spatial': 1.612067e+01, 'q_lat_norm_g': 7.283522e-01, 'w_q_up': 5.797512e-01, 'kv_lat_norm_g': 3.680091e+01, 'w_kv_up': 1.778602e+01, 'q_norm_g': 1.433770e+00, 'k_norm_g': 1.432082e+00, 'conv_w': 5.836016e+00, 'a_log': 2.574692e+01, 'dt_bias': 2.502562e+01, 'o_norm_g': 2.277812e+02, 'w_branch': 1.884067e+01, 'w_out': 3.163839e+01, 'norm2_g': 3.870241e+02, 'w_ff1': 1.927635e+01, 'w_ff2': 6.953840e+01}


def _to_microbatches(a, axis):
    t = _jnp.moveaxis(a, axis, 0)
    t = t.reshape((N_MICROBATCH, t.shape[0] // N_MICROBATCH) + t.shape[1:])
    return _jnp.moveaxis(t, 1, axis + 1)


def setup_inputs(seed: int = 0) -> dict:
    inp = _fwd_setup_inputs(seed)
    key = _jax.random.fold_in(_jax.random.key(seed), 7919)
    shape, _ = _output_shape()
    out = dict(inp)
    out["loss_target"] = _jax.random.normal(_jax.random.fold_in(key, 0), shape, _jnp.float32)
    for i, name in enumerate(TWIN_WEIGHTS):
        w = inp[name].astype(_jnp.float32)
        if MOMENT_SCALE is None:
            s = _jnp.sqrt(_jnp.mean(_jnp.square(w)) + 1e-30)
        else:
            s = MOMENT_SCALE[name]
        km, kv = _jax.random.split(_jax.random.fold_in(key, i + 1))
        out[name] = w
        out["m_" + name] = s * _jax.random.normal(km, w.shape, _jnp.float32)
        out["v_" + name] = (s * s) * _jax.random.uniform(kv, w.shape, _jnp.float32, 0.5, 1.5)
    if N_MICROBATCH > 1:
        for name, axis in PER_EXAMPLE_BATCH_AXIS.items():
            out[name] = _to_microbatches(out[name], axis)
    return {'x': out['x'], 'positions': out['positions'], 'norm1_g': out['norm1_g'], 'w_in': out['w_in'], 'sgu_norm_g': out['sgu_norm_g'], 'w_spatial': out['w_spatial'], 'b_spatial': out['b_spatial'], 'q_lat_norm_g': out['q_lat_norm_g'], 'w_q_up': out['w_q_up'], 'kv_lat_norm_g': out['kv_lat_norm_g'], 'w_kv_up': out['w_kv_up'], 'q_norm_g': out['q_norm_g'], 'k_norm_g': out['k_norm_g'], 'conv_w': out['conv_w'], 'a_log': out['a_log'], 'dt_bias': out['dt_bias'], 'o_norm_g': out['o_norm_g'], 'w_branch': out['w_branch'], 'w_out': out['w_out'], 'norm2_g': out['norm2_g'], 'w_ff1': out['w_ff1'], 'w_ff2': out['w_ff2'], 'loss_target': out['loss_target'], 'm_norm1_g': out['m_norm1_g'], 'm_w_in': out['m_w_in'], 'm_sgu_norm_g': out['m_sgu_norm_g'], 'm_w_spatial': out['m_w_spatial'], 'm_b_spatial': out['m_b_spatial'], 'm_q_lat_norm_g': out['m_q_lat_norm_g'], 'm_w_q_up': out['m_w_q_up'], 'm_kv_lat_norm_g': out['m_kv_lat_norm_g'], 'm_w_kv_up': out['m_w_kv_up'], 'm_q_norm_g': out['m_q_norm_g'], 'm_k_norm_g': out['m_k_norm_g'], 'm_conv_w': out['m_conv_w'], 'm_a_log': out['m_a_log'], 'm_dt_bias': out['m_dt_bias'], 'm_o_norm_g': out['m_o_norm_g'], 'm_w_branch': out['m_w_branch'], 'm_w_out': out['m_w_out'], 'm_norm2_g': out['m_norm2_g'], 'm_w_ff1': out['m_w_ff1'], 'm_w_ff2': out['m_w_ff2'], 'v_norm1_g': out['v_norm1_g'], 'v_w_in': out['v_w_in'], 'v_sgu_norm_g': out['v_sgu_norm_g'], 'v_w_spatial': out['v_w_spatial'], 'v_b_spatial': out['v_b_spatial'], 'v_q_lat_norm_g': out['v_q_lat_norm_g'], 'v_w_q_up': out['v_w_q_up'], 'v_kv_lat_norm_g': out['v_kv_lat_norm_g'], 'v_w_kv_up': out['v_w_kv_up'], 'v_q_norm_g': out['v_q_norm_g'], 'v_k_norm_g': out['v_k_norm_g'], 'v_conv_w': out['v_conv_w'], 'v_a_log': out['v_a_log'], 'v_dt_bias': out['v_dt_bias'], 'v_o_norm_g': out['v_o_norm_g'], 'v_w_branch': out['v_w_branch'], 'v_w_out': out['v_w_out'], 'v_norm2_g': out['v_norm2_g'], 'v_w_ff1': out['v_w_ff1'], 'v_w_ff2': out['v_w_ff2']}


def _loss(weights, diff, rest, loss_target):
    with _jax.named_scope("forward"):
        args = {**rest, TWIN_DIFF_INPUT: diff, **{k: w.astype(_WEIGHT_DTYPES[k]) for k, w in weights.items()}}
        y = _forward(args)
    with _jax.named_scope("loss_head"):
        err = _jnp.square(y.astype(_jnp.float32) - loss_target)
        return 0.5 * _jnp.sum(_jnp.mean(err, axis=-1)) if err.ndim else 0.5 * err


def _adamw(w, g, m, v):
    m = ADAM_B1 * m + (1.0 - ADAM_B1) * g
    v = ADAM_B2 * v + (1.0 - ADAM_B2) * _jnp.square(g)
    m_hat = m / (1.0 - ADAM_B1 ** ADAM_STEP)
    v_hat = v / (1.0 - ADAM_B2 ** ADAM_STEP)
    delta = -ADAM_LR * (m_hat / (_jnp.sqrt(v_hat) + ADAM_EPS) + ADAM_WD * w)
    return delta, m, v


def reference(x, positions, norm1_g, w_in, sgu_norm_g, w_spatial, b_spatial, q_lat_norm_g, w_q_up, kv_lat_norm_g, w_kv_up, q_norm_g, k_norm_g, conv_w, a_log, dt_bias, o_norm_g, w_branch, w_out, norm2_g, w_ff1, w_ff2, loss_target, m_norm1_g, m_w_in, m_sgu_norm_g, m_w_spatial, m_b_spatial, m_q_lat_norm_g, m_w_q_up, m_kv_lat_norm_g, m_w_kv_up, m_q_norm_g, m_k_norm_g, m_conv_w, m_a_log, m_dt_bias, m_o_norm_g, m_w_branch, m_w_out, m_norm2_g, m_w_ff1, m_w_ff2, v_norm1_g, v_w_in, v_sgu_norm_g, v_w_spatial, v_b_spatial, v_q_lat_norm_g, v_w_q_up, v_kv_lat_norm_g, v_w_kv_up, v_q_norm_g, v_k_norm_g, v_conv_w, v_a_log, v_dt_bias, v_o_norm_g, v_w_branch, v_w_out, v_norm2_g, v_w_ff1, v_w_ff2):
    given = dict(x=x, positions=positions, norm1_g=norm1_g, w_in=w_in, sgu_norm_g=sgu_norm_g, w_spatial=w_spatial, b_spatial=b_spatial, q_lat_norm_g=q_lat_norm_g, w_q_up=w_q_up, kv_lat_norm_g=kv_lat_norm_g, w_kv_up=w_kv_up, q_norm_g=q_norm_g, k_norm_g=k_norm_g, conv_w=conv_w, a_log=a_log, dt_bias=dt_bias, o_norm_g=o_norm_g, w_branch=w_branch, w_out=w_out, norm2_g=norm2_g, w_ff1=w_ff1, w_ff2=w_ff2, loss_target=loss_target, m_norm1_g=m_norm1_g, m_w_in=m_w_in, m_sgu_norm_g=m_sgu_norm_g, m_w_spatial=m_w_spatial, m_b_spatial=m_b_spatial, m_q_lat_norm_g=m_q_lat_norm_g, m_w_q_up=m_w_q_up, m_kv_lat_norm_g=m_kv_lat_norm_g, m_w_kv_up=m_w_kv_up, m_q_norm_g=m_q_norm_g, m_k_norm_g=m_k_norm_g, m_conv_w=m_conv_w, m_a_log=m_a_log, m_dt_bias=m_dt_bias, m_o_norm_g=m_o_norm_g, m_w_branch=m_w_branch, m_w_out=m_w_out, m_norm2_g=m_norm2_g, m_w_ff1=m_w_ff1, m_w_ff2=m_w_ff2, v_norm1_g=v_norm1_g, v_w_in=v_w_in, v_sgu_norm_g=v_sgu_norm_g, v_w_spatial=v_w_spatial, v_b_spatial=v_b_spatial, v_q_lat_norm_g=v_q_lat_norm_g, v_w_q_up=v_w_q_up, v_kv_lat_norm_g=v_kv_lat_norm_g, v_w_kv_up=v_w_kv_up, v_q_norm_g=v_q_norm_g, v_k_norm_g=v_k_norm_g, v_conv_w=v_conv_w, v_a_log=v_a_log, v_dt_bias=v_dt_bias, v_o_norm_g=v_o_norm_g, v_w_branch=v_w_branch, v_w_out=v_w_out, v_norm2_g=v_norm2_g, v_w_ff1=v_w_ff1, v_w_ff2=v_w_ff2)
    weights = {n: given[n] for n in TWIN_WEIGHTS}
    shared = {n: given[n] for n in SHARED_INPUTS}
    per_example = {n: given[n] for n in ['x', 'positions']}
    grad_fn = _jax.value_and_grad(_loss, argnums=(0, 1))

    def one_microbatch(ex, loss_target):
        ex = dict(ex)
        diff = ex.pop(TWIN_DIFF_INPUT)
        return grad_fn(weights, diff, {**shared, **ex}, loss_target)

    if N_MICROBATCH == 1:
        loss, (grad_w, grad_x) = one_microbatch(per_example, given["loss_target"])
    else:
        def body(carry, xs):
            loss_sum, grad_sum = carry
            l_k, (gw_k, gx_k) = one_microbatch(xs[0], xs[1])
            with _jax.named_scope("update"):
                return (loss_sum + l_k, _jax.tree.map(_jnp.add, grad_sum, gw_k)), gx_k

        init = (_jnp.zeros((), _jnp.float32), _jax.tree.map(_jnp.zeros_like, weights))
        (loss, grad_w), grad_x = _jax.lax.scan(body, init, (per_example, given["loss_target"]))
    with _jax.named_scope("update"):
        delta_w, new_m, new_v = {}, {}, {}
        for n in TWIN_WEIGHTS:
            delta_w[n], new_m[n], new_v[n] = _adamw(weights[n], grad_w[n], given["m_" + n], given["v_" + n])
    return (loss, grad_x, *[grad_w[n] for n in TWIN_WEIGHTS], *[delta_w[n] for n in TWIN_WEIGHTS],
            *[new_m[n] for n in TWIN_WEIGHTS], *[new_v[n] for n in TWIN_WEIGHTS])
```

```python
import functools
import math

import jax
import jax.numpy as jnp
from jax import lax
from jax.experimental import pallas as pl
from jax.experimental.pallas import tpu as pltpu

F32, BF16 = jnp.float32, jnp.bfloat16
HI = lax.Precision.HIGHEST

N_DEV = 8
D_MODEL = 1024
DEPTH = 2
N_HEADS = 8
HEAD_PAD = 128
A_WIDTH = 512
B_NOPE, B_ROPE, B_VDIM = 64, 32, 64
B_QK = B_NOPE + B_ROPE
B_Q_LORA, B_KV_LORA = 384, 256
ROPE_BASE = 10000.0
C_DK = 64
C_CHUNK = 64
C_QKV = 1536
C_Z = 512
SGU_CHUNK = 128
D_FF = 4096
EPS = 1e-6
ADAM_LR, ADAM_B1, ADAM_B2, ADAM_EPS, ADAM_WD, ADAM_STEP = 0.001, 0.9, 0.999, 1e-08, 0.01, 10
O_QLAT, O_KVLAT, O_KROPE, O_CQKV, O_GATES, D_IN = 1024, 1408, 1664, 1696, 3760, 6832
W_B, W_C = 768, 2176
VMEM_LIMIT = 56 * 2 ** 20
NEG = -1e30

SHARDED = ("w_in", "w_q_up", "w_kv_up", "conv_w", "w_branch", "w_out", "w_ff1", "w_ff2")
SHARD_AXIS = {"w_in": 2, "w_q_up": 2, "w_kv_up": 2, "conv_w": 2, "w_branch": 3, "w_out": 1, "w_ff1": 2, "w_ff2": 1}
REPLICATED = ("norm1_g", "sgu_norm_g", "w_spatial", "b_spatial", "q_lat_norm_g", "kv_lat_norm_g", "q_norm_g",
              "k_norm_g", "a_log", "dt_bias", "o_norm_g", "norm2_g")
WEIGHTS = ("norm1_g", "w_in", "sgu_norm_g", "w_spatial", "b_spatial", "q_lat_norm_g", "w_q_up", "kv_lat_norm_g",
           "w_kv_up", "q_norm_g", "k_norm_g", "conv_w", "a_log", "dt_bias", "o_norm_g", "w_branch", "w_out",
           "norm2_g", "w_ff1", "w_ff2")


def _params(sem, vmem=VMEM_LIMIT):
    return pltpu.CompilerParams(dimension_semantics=sem, vmem_limit_bytes=vmem)


_FORMS = {"nn": (1, 0), "nt": (1, 1), "tn": (0, 0)}


def _dot(form, a, b, batch, prec=None):
    ca, cb = _FORMS[form]
    o = 1 if batch else 0
    bd = ((0,), (0,)) if batch else ((), ())
    return lax.dot_general(a, b, (((ca + o,), (cb + o,)), bd), precision=prec, preferred_element_type=F32)


def _mm_raw(form, batch, a, b):
    return _dot(form, a.astype(BF16), b.astype(BF16), batch)


@functools.partial(jax.custom_vjp, nondiff_argnums=(0, 1))
def _mm(form, batch, a, b):
    return _mm_raw(form, batch, a, b)


def _mm_fwd(form, batch, a, b):
    return _mm_raw(form, batch, a, b), (a, b)


def _mm_bwd(form, batch, res, g):
    a, b = res
    if form == "nn":
        da, db = _mm_raw("nt", batch, g, b), _mm_raw("tn", batch, a, g)
    elif form == "nt":
        da, db = _mm_raw("nn", batch, g, b), _mm_raw("tn", batch, g, a)
    else:
        da, db = _mm_raw("nt", batch, b, g), _mm_raw("nn", batch, a, g)
    return da.astype(a.dtype), db.astype(b.dtype)


_mm.defvjp(_mm_fwd, _mm_bwd)


def _mmh(form, a, b, batch=False):
    return _dot(form, a, b, batch, HI)


@functools.partial(jax.custom_vjp, nondiff_argnums=(1, 2))
def _roll(x, shift, axis):
    return pltpu.roll(x, shift % x.shape[axis], axis)


def _roll_fwd(x, shift, axis):
    return _roll(x, shift, axis), None


def _roll_bwd(shift, axis, _, g):
    return (_roll(g, -shift, axis),)


_roll.defvjp(_roll_fwd, _roll_bwd)


@jax.custom_vjp
def _tile_heads(x):
    return jnp.concatenate([x] * N_HEADS, axis=1)


def _tile_heads_fwd(x):
    return _tile_heads(x), None


def _tile_heads_bwd(_, g):
    w = g.shape[1] // N_HEADS
    acc = g[:, :w]
    for h in range(1, N_HEADS):
        acc = acc + g[:, h * w:(h + 1) * w]
    return (acc,)


_tile_heads.defvjp(_tile_heads_fwd, _tile_heads_bwd)


def _iota(shape, dim):
    return lax.broadcasted_iota(jnp.int32, shape, dim)


def _head_indicator(width, per_head):
    return (_iota((width, N_HEADS), 0) // per_head == _iota((width, N_HEADS), 1)).astype(F32)


def _head_indicator_t(width, per_head):
    return (_iota((N_HEADS, width), 1) // per_head == _iota((N_HEADS, width), 0)).astype(F32)


def _rms(x, g):
    return x * lax.rsqrt(jnp.mean(x * x, axis=-1, keepdims=True) + EPS) * g


def _gelu(x):
    return 0.5 * x * (1.0 + lax.erf(x * (2.0 ** -0.5)))


def _head_rms(x, g_full, per_head, n_real):
    width = x.shape[1]
    ss = _mmh("nn", x * x, _head_indicator(width, per_head))
    r = lax.rsqrt(ss * (1.0 / n_real) + EPS)
    return x * _mmh("nn", r, _head_indicator_t(width, per_head)) * g_full


def _rope(x, cos_t, sin_hi, sin_lo):
    half = B_ROPE // 2
    return (x * _tile_heads(cos_t) + _roll(x, half, 1) * _tile_heads(sin_hi)
            + _roll(x, -half, 1) * _tile_heads(sin_lo))


def _tok_spec(t, width, col):
    return pl.BlockSpec((t, width), lambda i, c=col: (i, c))


def _par_spec(shape):
    nd = len(shape)
    return pl.BlockSpec(shape, lambda i: (0,) * nd)


def _local_fwd(name, f, toks, pars, outs, t):
    s = toks[0][0].shape[0]
    nt, npar = len(toks), len(pars)

    def body(*refs):
        vals = [r[...] for r in refs[:nt + npar]]
        for r, o in zip(refs[nt + npar:], f(*vals)):
            r[...] = o.astype(r.dtype)

    return pl.pallas_call(
        body, name=name, grid=(s // t,),
        in_specs=[_tok_spec(t, w, c) for _, w, c in toks] + [_par_spec(p.shape) for p in pars],
        out_specs=[_tok_spec(t, w, 0) for w, _ in outs],
        out_shape=[jax.ShapeDtypeStruct((s, w), dt) for w, dt in outs],
        compiler_params=_params(("parallel",)),
    )(*[a for a, _, _ in toks], *pars)


def _local_bwd(name, f, toks, pars, cots, t, tok_diff, par_diff):
    s = toks[0][0].shape[0]
    nt, npar, nc = len(toks), len(pars), len(cots)
    dt_idx = [k for k in range(nt) if tok_diff[k]]
    dp_idx = [k for k in range(npar) if par_diff[k]]

    def body(*refs):
        i = pl.program_id(0)
        tv = [r[...] for r in refs[:nt]]
        pv = [r[...] for r in refs[nt:nt + npar]]
        cv = [r[...].astype(F32) for r in refs[nt + npar:nt + npar + nc]]
        out_refs = refs[nt + npar + nc:]

        def g(*d):
            tt, pp = list(tv), list(pv)
            for k, val in zip(dt_idx, d[:len(dt_idx)]):
                tt[k] = val
            for k, val in zip(dp_idx, d[len(dt_idx):]):
                pp[k] = val
            return tuple(o.astype(F32) for o in f(*tt, *pp))

        _, vjp = jax.vjp(g, *[tv[k] for k in dt_idx], *[pv[k] for k in dp_idx])
        grads = vjp(tuple(cv))
        for r, gr in zip(out_refs[:len(dt_idx)], grads[:len(dt_idx)]):
            r[...] = gr.astype(r.dtype)
        par_refs = out_refs[len(dt_idx):]

        @pl.when(i == 0)
        def _():
            for r in par_refs:
                r[...] = jnp.zeros(r.shape, r.dtype)

        for r, gr in zip(par_refs, grads[len(dt_idx):]):
            r[...] += gr.astype(F32)

    res = pl.pallas_call(
        body, name=name, grid=(s // t,),
        in_specs=([_tok_spec(t, w, c) for _, w, c in toks] + [_par_spec(p.shape) for p in pars]
                  + [_tok_spec(t, c.shape[1], 0) for c in cots]),
        out_specs=([_tok_spec(t, toks[k][1], 0) for k in dt_idx] + [_par_spec(pars[k].shape) for k in dp_idx]),
        out_shape=([jax.ShapeDtypeStruct((s, toks[k][1]), F32) for k in dt_idx]
                   + [jax.ShapeDtypeStruct(pars[k].shape, F32) for k in dp_idx]),
        compiler_params=_params(("arbitrary",)),
    )(*[a for a, _, _ in toks], *pars, *cots)
    return res[:len(dt_idx)], res[len(dt_idx):]


def _f_norm(x, g):
    return (_rms(x, g),)


def _f_norm_res(x, g):
    return _rms(x, g), x


def _f_rope_tables(pos, inv_row):
    ang = pos.astype(F32) * inv_row
    lane = _iota(ang.shape, 1)
    sn = jnp.sin(ang)
    half = B_ROPE // 2
    sin_hi = jnp.where((lane >= B_NOPE + half) & (lane < B_QK), sn, 0.0)
    sin_lo = jnp.where((lane >= B_NOPE) & (lane < B_NOPE + half), -sn, 0.0)
    return jnp.cos(ang), sin_hi, sin_lo


def _f_sgu(p_a, g, w_s, b_s):
    u = _gelu(p_a[:, :A_WIDTH])
    v = _rms(_gelu(p_a[:, A_WIDTH:]), g)
    tril = _iota((SGU_CHUNK, SGU_CHUNK), 1) <= _iota((SGU_CHUNK, SGU_CHUNK), 0)
    w_cat = jnp.concatenate([jnp.where(tril, w_s[gi], 0.0) for gi in range(N_HEADS)], axis=1)
    group = _iota((1, A_WIDTH), 1) // (A_WIDTH // N_HEADS)
    v_stack = jnp.concatenate([jnp.where(group == gi, v, 0.0) for gi in range(N_HEADS)], axis=0)
    bias = _mmh("tn", b_s, _head_indicator_t(A_WIDTH, A_WIDTH // N_HEADS))
    return (u * (_mm("nn", False, w_cat, v_stack) + bias),)


def _f_mla_prep(p_b, cos_t, sin_hi, sin_lo, q_lat_g, kv_lat_g, wq, wk, wv, qn_g, kn_g):
    kv_lat, k_rope, q_lat = p_b[:, :B_KV_LORA], p_b[:, B_KV_LORA:B_KV_LORA + HEAD_PAD], p_b[:, B_KV_LORA + HEAD_PAD:]
    q = _mm("nn", False, _rms(q_lat, q_lat_g), wq)
    q = _rope(_head_rms(q, _tile_heads(qn_g), HEAD_PAD, B_QK), cos_t, sin_hi, sin_lo)
    kvn = _rms(kv_lat, kv_lat_g)
    k = _mm("nn", False, kvn, wk) + _tile_heads(k_rope)
    k = _rope(_head_rms(k, _tile_heads(kn_g), HEAD_PAD, B_QK), cos_t, sin_hi, sin_lo)
    return q, k, _mm("nn", False, kvn, wv)


def _f_gdn_pre(conv_pre, ba, a_log_row, dt_row):
    qkv = jax.nn.silu(conv_pre)
    ind, ind_t = _head_indicator(A_WIDTH, C_DK), _head_indicator_t(A_WIDTH, C_DK)

    def l2(x):
        return x * _mmh("nn", lax.rsqrt(_mmh("nn", x * x, ind) + EPS), ind_t)

    lane = _iota(ba.shape, 1)
    g = -jnp.exp(a_log_row) * jax.nn.softplus(ba + dt_row)
    gb = jnp.where(lane < N_HEADS, jax.nn.sigmoid(ba), jnp.where(lane < 2 * N_HEADS, g, 0.0))
    return l2(qkv[:, :A_WIDTH]), l2(qkv[:, A_WIDTH:2 * A_WIDTH]), qkv[:, 2 * A_WIDTH:], gb


def _f_gdn_post(o, c_z, o_g):
    place = (_iota((C_DK, A_WIDTH), 1) % C_DK == _iota((C_DK, A_WIDTH), 0)).astype(F32)
    return (_head_rms(o, _mmh("nn", o_g, place), C_DK, C_DK) * jax.nn.silu(c_z),)


def _f_merge(p_g, y0, y1, y2):
    d = D_MODEL
    return (jax.nn.sigmoid(p_g[:, :d]) * y0 + jax.nn.sigmoid(p_g[:, d:2 * d]) * y1
            + jax.nn.sigmoid(p_g[:, 2 * d:]) * y2,)


def _f_relu2(a):
    return (jnp.square(jnp.maximum(a, 0.0)),)


def _pick(n, whole_up_to, candidates):
    if n <= whole_up_to:
        return n
    for c in candidates:
        if n % c == 0:
            return c
    return n


def _matmul(name, a, w, add=None, out_dtype=F32, tm=512):
    m, k = a.shape
    n = w.shape[1]
    tm = min(tm, m)
    tk = _pick(k, 2304, (2048, 1536, 1024, 512))
    tn = _pick(n, 2304, (2048, 1536, 1024, 512))
    nk = k // tk

    def body(*refs):
        a_ref, w_ref = refs[0], refs[1]
        add_ref = refs[2] if add is not None else None
        o_ref, acc = refs[-2], refs[-1]
        kk = pl.program_id(2)
        part = jnp.dot(a_ref[...].astype(BF16), w_ref[...].astype(BF16), preferred_element_type=F32)

        @pl.when(kk == 0)
        def _():
            acc[...] = part

        @pl.when(kk > 0)
        def _():
            acc[...] += part

        @pl.when(kk == nk - 1)
        def _():
            r = acc[...]
            if add_ref is not None:
                r = r + add_ref[...]
            o_ref[...] = r.astype(o_ref.dtype)

    in_specs = [pl.BlockSpec((tm, tk), lambda i, j, kk: (i, kk)), pl.BlockSpec((tk, tn), lambda i, j, kk: (kk, j))]
    args = [a, w]
    if add is not None:
        in_specs.append(pl.BlockSpec((tm, tn), lambda i, j, kk: (i, j)))
        args.append(add)
    return pl.pallas_call(
        body, name=name, grid=(m // tm, n // tn, nk), in_specs=in_specs,
        out_specs=pl.BlockSpec((tm, tn), lambda i, j, kk: (i, j)),
        out_shape=jax.ShapeDtypeStruct((m, n), out_dtype),
        scratch_shapes=[pltpu.VMEM((tm, tn), F32)],
        compiler_params=_params(("parallel", "parallel", "arbitrary")),
    )(*args)


def _matmul_tn(name, a, b, a_col=None, tm=512):
    m = a.shape[0]
    k, acol = (a.shape[1], 0) if a_col is None else a_col
    n = b.shape[1]
    tm = min(tm, m)
    tk = _pick(k, 1536, (1024, 512))
    tn = _pick(n, 2304, (1024, 512))
    nm = m // tm

    def body(a_ref, b_ref, o_ref):
        mm = pl.program_id(2)
        part = lax.dot_general(a_ref[...].astype(BF16), b_ref[...].astype(BF16), (((0,), (0,)), ((), ())),
                               preferred_element_type=F32)

        @pl.when(mm == 0)
        def _():
            o_ref[...] = part

        @pl.when(mm > 0)
        def _():
            o_ref[...] += part

    kb = k // tk
    return pl.pallas_call(
        body, name=name, grid=(kb, n // tn, nm),
        in_specs=[pl.BlockSpec((tm, tk), lambda i, j, mm: (mm, acol * kb + i)),
                  pl.BlockSpec((tm, tn), lambda i, j, mm: (mm, j))],
        out_specs=pl.BlockSpec((tk, tn), lambda i, j, mm: (i, j)),
        out_shape=jax.ShapeDtypeStruct((k, n), F32),
        compiler_params=_params(("parallel", "parallel", "arbitrary")),
    )(a, b)


def _shift_down(x, prev, s):
    rolled = pltpu.roll(x, s, 0)
    pr = pltpu.roll(prev, s, 0)
    head = jnp.where(_iota((8, 1), 0) < s, pr, rolled[:8])
    return jnp.concatenate([head, rolled[8:]], axis=0)


def _shift_up(x, nxt, s):
    t = x.shape[0]
    rolled = pltpu.roll(x, t - s, 0)
    nr = pltpu.roll(nxt, 8 - s, 0)
    tail = jnp.where(_iota((8, 1), 0) >= 8 - s, nr, rolled[t - 8:])
    return jnp.concatenate([rolled[:t - 8], tail], axis=0)


def _conv_fwd(p_c, w8, t):
    s = p_c.shape[0]
    t = min(t, s)
    r = t // 8

    def body(x_ref, prev_ref, w_ref, o_ref):
        i = pl.program_id(0)
        x = x_ref[...]
        prev = jnp.where(i == 0, 0.0, prev_ref[...])
        acc = w_ref[3:4, :] * x
        for sh in range(1, 4):
            acc = acc + w_ref[3 - sh:4 - sh, :] * _shift_down(x, prev, sh)
        o_ref[...] = acc

    return pl.pallas_call(
        body, name="conv_fwd", grid=(s // t,),
        in_specs=[pl.BlockSpec((t, C_QKV), lambda i: (i, 0)),
                  pl.BlockSpec((8, C_QKV), lambda i: (jnp.maximum(i * r - 1, 0), 0)),
                  pl.BlockSpec((8, C_QKV), lambda i: (0, 0))],
        out_specs=pl.BlockSpec((t, C_QKV), lambda i: (i, 0)),
        out_shape=jax.ShapeDtypeStruct((s, C_QKV), F32),
        compiler_params=_params(("parallel",)),
    )(p_c, p_c, w8)


def _conv_bwd(p_c, dy, w8, t):
    s = p_c.shape[0]
    t = min(t, s)
    r = t // 8
    n = s // t

    def body(x_ref, prev_ref, dy_ref, next_ref, w_ref, dx_ref, dw_ref):
        i = pl.program_id(0)
        x, g = x_ref[...], dy_ref[...]
        prev = jnp.where(i == 0, 0.0, prev_ref[...])
        nxt = jnp.where(i == n - 1, 0.0, next_ref[...])

        @pl.when(i == 0)
        def _():
            dw_ref[...] = jnp.zeros(dw_ref.shape, F32)

        dx = w_ref[3:4, :] * g
        dw_ref[3:4, :] += jnp.sum(g * x, axis=0, keepdims=True)
        for sh in range(1, 4):
            dx = dx + w_ref[3 - sh:4 - sh, :] * _shift_up(g, nxt, sh)
            dw_ref[3 - sh:4 - sh, :] += jnp.sum(g * _shift_down(x, prev, sh), axis=0, keepdims=True)
        dx_ref[...] = dx

    return pl.pallas_call(
        body, name="conv_bwd", grid=(n,),
        in_specs=[pl.BlockSpec((t, C_QKV), lambda i: (i, 0)),
                  pl.BlockSpec((8, C_QKV), lambda i: (jnp.maximum(i * r - 1, 0), 0)),
                  pl.BlockSpec((t, C_QKV), lambda i: (i, 0)),
                  pl.BlockSpec((8, C_QKV), lambda i: (jnp.minimum((i + 1) * r, s // 8 - 1), 0)),
                  pl.BlockSpec((8, C_QKV), lambda i: (0, 0))],
        out_specs=[pl.BlockSpec((t, C_QKV), lambda i: (i, 0)), pl.BlockSpec((8, C_QKV), lambda i: (0, 0))],
        out_shape=[jax.ShapeDtypeStruct((s, C_QKV), F32), jax.ShapeDtypeStruct((8, C_QKV), F32)],
        compiler_params=_params(("arbitrary",)),
    )(p_c, p_c, dy, dy, w8)


def _delta_chunk(state, q, k, v, g, beta):
    c = C_CHUNK
    row, col = _iota((c, c), 0), _iota((c, c), 1)
    tril, strict = col <= row, col < row
    gc = _mmh("nn", g, (row <= col).astype(F32))
    g_last = jnp.sum(g, axis=1, keepdims=True)
    qs = q * (C_DK ** -0.5)
    decay = jnp.exp(jnp.where(tril, gc[:, :, None] - gc[:, None, :], NEG))
    k_beta, v_beta = k * beta[:, :, None], v * beta[:, :, None]
    x = -jnp.where(strict, _mm("nt", True, k_beta, k) * decay, 0.0)
    powers = [x]
    for _ in range(5):
        powers.append(_mmh("nn", powers[-1], powers[-1], True))
    u, w = v_beta, k_beta * jnp.exp(gc)[:, :, None]
    for p in reversed(powers):
        u, w = u + _mmh("nn", p, u, True), w + _mmh("nn", p, w, True)
    intra = jnp.where(tril, _mm("nt", True, qs, k) * decay, 0.0)
    v_new = u - _mm("nn", True, w, state)
    o = _mm("nn", True, qs * jnp.exp(gc)[:, :, None], state) + _mm("nn", True, intra, v_new)
    new_state = (state * jnp.exp(g_last)[:, :, None]
                 + _mm("tn", True, k * jnp.exp(g_last - gc)[:, :, None], v_new))
    return new_state, o


def _delta_specs(n, rev):
    def at(i):
        return n - 1 - i if rev else i
    tok = pl.BlockSpec((N_HEADS, C_CHUNK, C_DK), lambda i: (0, at(i), 0))
    vec = pl.BlockSpec((1, N_HEADS, C_CHUNK), lambda i: (at(i), 0, 0))
    st = pl.BlockSpec((1, N_HEADS, C_DK, C_DK), lambda i: (at(i), 0, 0, 0))
    return tok, vec, st


def _delta_fwd(q, k, v, g, beta):
    s = q.shape[1]
    n = s // C_CHUNK
    tok, vec, st = _delta_specs(n, False)

    def body(q_ref, k_ref, v_ref, g_ref, b_ref, o_ref, st_ref, state):
        @pl.when(pl.program_id(0) == 0)
        def _():
            state[...] = jnp.zeros(state.shape, F32)

        cur = state[...]
        st_ref[0] = cur
        new, o = _delta_chunk(cur, q_ref[...], k_ref[...], v_ref[...], g_ref[0], b_ref[0])
        o_ref[...] = o
        state[...] = new

    return pl.pallas_call(
        body, name="delta_fwd", grid=(n,), in_specs=[tok, tok, tok, vec, vec], out_specs=[tok, st],
        out_shape=[jax.ShapeDtypeStruct((N_HEADS, s, C_DK), F32), jax.ShapeDtypeStruct((n, N_HEADS, C_DK, C_DK), F32)],
        scratch_shapes=[pltpu.VMEM((N_HEADS, C_DK, C_DK), F32)],
        compiler_params=_params(("arbitrary",)),
    )(q, k, v, g, beta)


def _delta_bwd(q, k, v, g, beta, states, do):
    s = q.shape[1]
    n = s // C_CHUNK
    tok, vec, st = _delta_specs(n, True)

    def body(q_ref, k_ref, v_ref, g_ref, b_ref, st_ref, do_ref, dq_ref, dk_ref, dv_ref, dg_ref, db_ref, dstate):
        @pl.when(pl.program_id(0) == 0)
        def _():
            dstate[...] = jnp.zeros(dstate.shape, F32)

        _, vjp = jax.vjp(_delta_chunk, st_ref[0], q_ref[...], k_ref[...], v_ref[...], g_ref[0], b_ref[0])
        dst, dq, dk, dv, dg, db = vjp((dstate[...], do_ref[...]))
        dq_ref[...], dk_ref[...], dv_ref[...] = dq, dk, dv
        dg_ref[0], db_ref[0] = dg, db
        dstate[...] = dst

    tok_shape = jax.ShapeDtypeStruct((N_HEADS, s, C_DK), F32)
    vec_shape = jax.ShapeDtypeStruct((n, N_HEADS, C_CHUNK), F32)
    return pl.pallas_call(
        body, name="delta_bwd", grid=(n,), in_specs=[tok, tok, tok, vec, vec, st, tok],
        out_specs=[tok, tok, tok, vec, vec], out_shape=[tok_shape, tok_shape, tok_shape, vec_shape, vec_shape],
        scratch_shapes=[pltpu.VMEM((N_HEADS, C_DK, C_DK), F32)],
        compiler_params=_params(("arbitrary",)),
    )(q, k, v, g, beta, states, do)


ATT_SCALE = B_QK ** -0.5


def _causal_scores(q, k, i, j, tq, tk):
    s = _dot("nt", q, k, False) * ATT_SCALE
    q_pos = i * tq + _iota((tq, tk), 0)
    k_pos = j * tk + _iota((tq, tk), 1)
    return jnp.where(k_pos <= q_pos, s, -jnp.inf)


def _flash_fwd(q, k, v, tb):
    s = q.shape[0]
    tb = min(tb, s)
    nb = s // tb

    def body(q_ref, k_ref, v_ref, o_ref, lse_ref, m_s, l_s, acc):
        i, j = pl.program_id(1), pl.program_id(2)

        @pl.when(j == 0)
        def _():
            m_s[...] = jnp.full(m_s.shape, -jnp.inf, F32)
            l_s[...] = jnp.zeros(l_s.shape, F32)
            acc[...] = jnp.zeros(acc.shape, F32)

        @pl.when(j <= i)
        def _():
            sc = _causal_scores(q_ref[...], k_ref[...], i, j, tb, tb)
            m_new = jnp.maximum(m_s[...], jnp.max(sc, axis=1, keepdims=True))
            alpha = jnp.exp(m_s[...] - m_new)
            p = jnp.exp(sc - m_new)
            l_s[...] = alpha * l_s[...] + jnp.sum(p, axis=1, keepdims=True)
            acc[...] = alpha * acc[...] + jnp.dot(p.astype(BF16), v_ref[...], preferred_element_type=F32)
            m_s[...] = m_new

        @pl.when(j == i)
        def _():
            o_ref[...] = acc[...] / l_s[...]
            lse_ref[0] = m_s[...] + jnp.log(l_s[...])

    qs = pl.BlockSpec((tb, HEAD_PAD), lambda h, i, j: (i, h))
    ks = pl.BlockSpec((tb, HEAD_PAD), lambda h, i, j: (jnp.minimum(j, i), h))
    return pl.pallas_call(
        body, name="flash_fwd", grid=(N_HEADS, nb, nb), in_specs=[qs, ks, ks],
        out_specs=[qs, pl.BlockSpec((1, tb, 1), lambda h, i, j: (h, i, 0))],
        out_shape=[jax.ShapeDtypeStruct((s, N_HEADS * HEAD_PAD), F32), jax.ShapeDtypeStruct((N_HEADS, s, 1), F32)],
        scratch_shapes=[pltpu.VMEM((tb, 1), F32), pltpu.VMEM((tb, 1), F32), pltpu.VMEM((tb, HEAD_PAD), F32)],
        compiler_params=_params(("parallel", "parallel", "arbitrary")),
    )(q, k, v)


def _flash_bwd_dq(q, k, v, o, lse, do, tb):
    s = q.shape[0]
    tb = min(tb, s)
    nb = s // tb

    def body(q_ref, k_ref, v_ref, o_ref, lse_ref, do_ref, dq_ref, acc, delta):
        i, j = pl.program_id(1), pl.program_id(2)

        @pl.when(j == 0)
        def _():
            acc[...] = jnp.zeros(acc.shape, F32)
            delta[...] = jnp.sum(o_ref[...] * do_ref[...], axis=1, keepdims=True)

        @pl.when(j <= i)
        def _():
            p = jnp.exp(_causal_scores(q_ref[...], k_ref[...], i, j, tb, tb) - lse_ref[0])
            dp = _dot("nt", do_ref[...].astype(BF16), v_ref[...], False)
            ds = p * (dp - delta[...]) * ATT_SCALE
            acc[...] += jnp.dot(ds.astype(BF16), k_ref[...], preferred_element_type=F32)

        @pl.when(j == i)
        def _():
            dq_ref[...] = acc[...]

    qs = pl.BlockSpec((tb, HEAD_PAD), lambda h, i, j: (i, h))
    ks = pl.BlockSpec((tb, HEAD_PAD), lambda h, i, j: (jnp.minimum(j, i), h))
    return pl.pallas_call(
        body, name="flash_bwd_dq", grid=(N_HEADS, nb, nb),
        in_specs=[qs, ks, ks, qs, pl.BlockSpec((1, tb, 1), lambda h, i, j: (h, i, 0)), qs],
        out_specs=qs, out_shape=jax.ShapeDtypeStruct((s, N_HEADS * HEAD_PAD), F32),
        scratch_shapes=[pltpu.VMEM((tb, HEAD_PAD), F32), pltpu.VMEM((tb, 1), F32)],
        compiler_params=_params(("parallel", "parallel", "arbitrary")),
    )(q, k, v, o, lse, do)


def _flash_bwd_dkv(q, k, v, o, lse, do, tb):
    s = q.shape[0]
    tb = min(tb, s)
    nb = s // tb

    def body(q_ref, k_ref, v_ref, o_ref, lse_ref, do_ref, dk_ref, dv_ref, dk_acc, dv_acc):
        j, i = pl.program_id(1), pl.program_id(2)

        @pl.when(i == 0)
        def _():
            dk_acc[...] = jnp.zeros(dk_acc.shape, F32)
            dv_acc[...] = jnp.zeros(dv_acc.shape, F32)

        @pl.when(i >= j)
        def _():
            do = do_ref[...]
            p = jnp.exp(_causal_scores(q_ref[...], k_ref[...], i, j, tb, tb) - lse_ref[0])
            dp = _dot("nt", do.astype(BF16), v_ref[...], False)
            delta = jnp.sum(o_ref[...] * do, axis=1, keepdims=True)
            ds = p * (dp - delta) * ATT_SCALE
            dv_acc[...] += _dot("tn", p.astype(BF16), do.astype(BF16), False)
            dk_acc[...] += _dot("tn", ds.astype(BF16), q_ref[...], False)

        @pl.when(i == nb - 1)
        def _():
            dk_ref[...] = dk_acc[...]
            dv_ref[...] = dv_acc[...]

    qs = pl.BlockSpec((tb, HEAD_PAD), lambda h, j, i: (jnp.maximum(i, j), h))
    ks = pl.BlockSpec((tb, HEAD_PAD), lambda h, j, i: (j, h))
    shape = jax.ShapeDtypeStruct((s, N_HEADS * HEAD_PAD), F32)
    return pl.pallas_call(
        body, name="flash_bwd_dkv", grid=(N_HEADS, nb, nb),
        in_specs=[qs, ks, ks, qs, pl.BlockSpec((1, tb, 1), lambda h, j, i: (h, jnp.maximum(i, j), 0)), qs],
        out_specs=[ks, ks], out_shape=[shape, shape],
        scratch_shapes=[pltpu.VMEM((tb, HEAD_PAD), F32), pltpu.VMEM((tb, HEAD_PAD), F32)],
        compiler_params=_params(("parallel", "parallel", "arbitrary")),
    )(q, k, v, o, lse, do)


def _loss_head(y, target, t):
    s, d = y.shape
    t = min(t, s)

    def body(y_ref, t_ref, sum_ref, dy_ref):
        @pl.when(pl.program_id(0) == 0)
        def _():
            sum_ref[...] = jnp.zeros(sum_ref.shape, F32)

        err = y_ref[...] - t_ref[...]
        dy_ref[...] = err * (1.0 / d)
        sum_ref[...] += jnp.broadcast_to(jnp.sum(err * err), sum_ref.shape)

    return pl.pallas_call(
        body, name="loss_head", grid=(s // t,),
        in_specs=[pl.BlockSpec((t, d), lambda i: (i, 0))] * 2,
        out_specs=[pl.BlockSpec((1, 128), lambda i: (0, 0)), pl.BlockSpec((t, d), lambda i: (i, 0))],
        out_shape=[jax.ShapeDtypeStruct((1, 128), F32), jax.ShapeDtypeStruct((s, d), F32)],
        compiler_params=_params(("arbitrary",)),
    )(y, target)


def _pad_to(a, axis, size):
    pad = [(0, 0)] * a.ndim
    pad[axis] = (0, size - a.shape[axis])
    return jnp.pad(a, pad)


def _prep_layer(w, l):
    p = {}
    w_in = w["w_in"][l]
    z = lambda n: jnp.zeros((D_MODEL, n), F32)
    groups = {
        "a": w_in[:, :O_QLAT],
        "b": jnp.concatenate([w_in[:, O_KVLAT:O_KROPE], z(B_NOPE), w_in[:, O_KROPE:O_CQKV], z(HEAD_PAD - B_QK),
                              w_in[:, O_QLAT:O_KVLAT]], axis=1),
        "c": jnp.concatenate([w_in[:, O_CQKV:O_GATES], z(W_C - (O_GATES - O_CQKV))], axis=1),
        "g": w_in[:, O_GATES:],
    }
    for key, val in groups.items():
        p["w_" + key] = val.astype(BF16)
        p["wt_" + key] = val.T.astype(BF16)
    for name in ("norm1_g", "sgu_norm_g", "q_lat_norm_g", "kv_lat_norm_g", "o_norm_g", "norm2_g"):
        p[name] = w[name][l][None, :]
    p["w_spatial"], p["b_spatial"] = w["w_spatial"][l], w["b_spatial"][l]
    p["wq"] = _pad_to(w["w_q_up"][l].reshape(B_Q_LORA, N_HEADS, B_QK), 2, HEAD_PAD).reshape(B_Q_LORA, -1)
    kv = w["w_kv_up"][l].reshape(B_KV_LORA, N_HEADS, B_NOPE + B_VDIM)
    p["wk"] = _pad_to(kv[:, :, :B_NOPE], 2, HEAD_PAD).reshape(B_KV_LORA, -1)
    p["wv"] = _pad_to(kv[:, :, B_NOPE:], 2, HEAD_PAD).reshape(B_KV_LORA, -1)
    p["qn_g"] = _pad_to(w["q_norm_g"][l][None, :], 1, HEAD_PAD)
    p["kn_g"] = _pad_to(w["k_norm_g"][l][None, :], 1, HEAD_PAD)
    p["conv_w"] = _pad_to(w["conv_w"][l], 0, 8)
    row = lambda v: jnp.pad(v[None, :], ((0, 0), (N_HEADS, HEAD_PAD - 2 * N_HEADS)))
    p["a_log"], p["dt_bias"] = row(w["a_log"][l]), row(w["dt_bias"][l])
    wb = w["w_branch"][l]
    wb1 = _pad_to(wb[1].reshape(N_HEADS, B_VDIM, D_MODEL), 1, HEAD_PAD).reshape(-1, D_MODEL)
    for key, val in (("wb0", wb[0]), ("wb1", wb1), ("wb2", wb[2]), ("w_out", w["w_out"][l]),
                     ("w_ff1", w["w_ff1"][l]), ("w_ff2", w["w_ff2"][l])):
        p[key] = val.astype(BF16)
        p[key + "_t"] = val.T.astype(BF16)
    return p


def _unprep_grads(g):
    out = {}
    gb, gc = g["w_b"], g["w_c"]
    out["w_in"] = jnp.concatenate([g["w_a"], gb[:, B_KV_LORA + HEAD_PAD:], gb[:, :B_KV_LORA],
                                   gb[:, B_KV_LORA + B_NOPE:B_KV_LORA + B_QK], gc[:, :O_GATES - O_CQKV], g["w_g"]], axis=1)
    for name in ("norm1_g", "sgu_norm_g", "q_lat_norm_g", "kv_lat_norm_g", "o_norm_g", "norm2_g"):
        out[name] = g[name][0]
    out["w_spatial"], out["b_spatial"] = g["w_spatial"], g["b_spatial"]
    out["w_q_up"] = g["wq"].reshape(B_Q_LORA, N_HEADS, HEAD_PAD)[:, :, :B_QK].reshape(B_Q_LORA, -1)
    gk = g["wk"].reshape(B_KV_LORA, N_HEADS, HEAD_PAD)[:, :, :B_NOPE]
    gv = g["wv"].reshape(B_KV_LORA, N_HEADS, HEAD_PAD)[:, :, :B_VDIM]
    out["w_kv_up"] = jnp.concatenate([gk, gv], axis=2).reshape(B_KV_LORA, -1)
    out["q_norm_g"], out["k_norm_g"] = g["qn_g"][0, :B_QK], g["kn_g"][0, :B_QK]
    out["conv_w"] = g["conv_w"][:4]
    out["a_log"], out["dt_bias"] = g["a_log"][0, N_HEADS:2 * N_HEADS], g["dt_bias"][0, N_HEADS:2 * N_HEADS]
    gb1 = g["wb1"].reshape(N_HEADS, HEAD_PAD, D_MODEL)[:, :B_VDIM].reshape(-1, D_MODEL)
    out["w_branch"] = jnp.stack([g["wb0"], gb1, g["wb2"]], axis=0)
    out["w_out"], out["w_ff1"], out["w_ff2"] = g["w_out"], g["w_ff1"], g["w_ff2"]
    return out


def _heads_first(a):
    s = a.shape[0]
    return a.reshape(s, N_HEADS, C_DK).transpose(1, 0, 2)


def _heads_last(a):
    return a.transpose(1, 0, 2).reshape(a.shape[1], N_HEADS * C_DK)


def _chunk_vec(a):
    return a.reshape(-1, C_CHUNK, N_HEADS).transpose(0, 2, 1)


def _unchunk_vec(a):
    return a.transpose(0, 2, 1).reshape(-1, N_HEADS)


def _layer_fwd(x, p, tabs, t, tb):
    sv = {"x": x}
    h1, = _local_fwd("norm1", _f_norm, [(x, D_MODEL, 0)], [p["norm1_g"]], [(D_MODEL, BF16)], t)
    sv["h1"] = h1
    p_a, p_b, p_c, p_g = (_matmul("proj_" + key, h1, p["w_" + key]) for key in "abcg")
    sv.update(p_a=p_a, p_b=p_b, p_c=p_c, p_g=p_g)
    y_a, = _local_fwd("sgu", _f_sgu, [(p_a, 2 * A_WIDTH, 0)], [p["sgu_norm_g"], p["w_spatial"], p["b_spatial"]],
                      [(A_WIDTH, BF16)], SGU_CHUNK)
    q, k, v = _local_fwd("mla_prep", _f_mla_prep, [(p_b, W_B, 0)] + [(tb_, HEAD_PAD, 0) for tb_ in tabs],
                         [p["q_lat_norm_g"], p["kv_lat_norm_g"], p["wq"], p["wk"], p["wv"], p["qn_g"], p["kn_g"]],
                         [(N_HEADS * HEAD_PAD, BF16)] * 3, min(t, 256))
    o_b, lse = _flash_fwd(q, k, v, tb)
    sv.update(q=q, k=k, v=v, o_b=o_b, lse=lse)
    conv_pre = _conv_fwd(p_c, p["conv_w"], t)
    cq, ck, cv, gb = _local_fwd("gdn_pre", _f_gdn_pre, [(conv_pre, C_QKV, 0), (p_c, HEAD_PAD, (C_QKV + C_Z) // HEAD_PAD)],
                                [p["a_log"], p["dt_bias"]], [(A_WIDTH, F32)] * 3 + [(HEAD_PAD, F32)], t)
    cq, ck, cv = _heads_first(cq), _heads_first(ck), _heads_first(cv)
    beta, g = _chunk_vec(gb[:, :N_HEADS]), _chunk_vec(gb[:, N_HEADS:2 * N_HEADS])
    o_c, states = _delta_fwd(cq, ck, cv, g, beta)
    o_c = _heads_last(o_c)
    sv.update(conv_pre=conv_pre, cq=cq, ck=ck, cv=cv, beta=beta, g=g, states=states, o_c=o_c)
    y_c, = _local_fwd("gdn_post", _f_gdn_post, [(o_c, A_WIDTH, 0), (p_c, C_Z, C_QKV // C_Z)], [p["o_norm_g"]],
                      [(A_WIDTH, BF16)], t)
    y0 = _matmul("branch0", y_a, p["wb0"])
    y1 = _matmul("branch1", o_b, p["wb1"])
    y2 = _matmul("branch2", y_c, p["wb2"])
    merged, = _local_fwd("merge", _f_merge, [(p_g, 3 * D_MODEL, 0), (y0, D_MODEL, 0), (y1, D_MODEL, 0), (y2, D_MODEL, 0)],
                         [], [(D_MODEL, BF16)], t)
    x1 = _matmul("out_proj", merged, p["w_out"], add=x)
    sv.update(y_a=y_a, y_c=y_c, y0=y0, y1=y1, y2=y2, merged=merged, x1=x1)
    h2, = _local_fwd("norm2", _f_norm, [(x1, D_MODEL, 0)], [p["norm2_g"]], [(D_MODEL, BF16)], t)
    a = _matmul("ff1", h2, p["w_ff1"])
    r, = _local_fwd("relu2", _f_relu2, [(a, D_FF, 0)], [], [(D_FF, BF16)], min(t, 256))
    x2 = _matmul("ff2", r, p["w_ff2"], add=x1)
    sv.update(h2=h2, a=a, r=r)
    return x2, sv


def _layer_bwd(dx2, sv, p, tabs, t, tb):
    g = {}
    dr = _matmul("d_ff2", dx2, p["w_ff2_t"])
    g["w_ff2"] = _matmul_tn("dw_ff2", sv["r"], dx2)
    (da,), _ = _local_bwd("relu2_bwd", _f_relu2, [(sv["a"], D_FF, 0)], [], [dr], min(t, 256), [True], [])
    dh2 = _matmul("d_ff1", da, p["w_ff1_t"])
    g["w_ff1"] = _matmul_tn("dw_ff1", sv["h2"], da)
    (dx1,), (g["norm2_g"],) = _local_bwd("norm2_bwd", _f_norm_res, [(sv["x1"], D_MODEL, 0)], [p["norm2_g"]],
                                         [dh2, dx2], t, [True], [True])
    dmerged = _matmul("d_out_proj", dx1, p["w_out_t"])
    g["w_out"] = _matmul_tn("dw_out", sv["merged"], dx1)
    (dp_g, dy0, dy1, dy2), _ = _local_bwd(
        "merge_bwd", _f_merge, [(sv["p_g"], 3 * D_MODEL, 0), (sv["y0"], D_MODEL, 0), (sv["y1"], D_MODEL, 0),
                                (sv["y2"], D_MODEL, 0)], [], [dmerged], min(t, 256), [True] * 4, [])
    dy_a = _matmul("d_branch0", dy0, p["wb0_t"])
    do_b = _matmul("d_branch1", dy1, p["wb1_t"])
    dy_c = _matmul("d_branch2", dy2, p["wb2_t"])
    g["wb0"] = _matmul_tn("dw_branch0", sv["y_a"], dy0)
    g["wb1"] = _matmul_tn("dw_branch1", sv["o_b"], dy1)
    g["wb2"] = _matmul_tn("dw_branch2", sv["y_c"], dy2)
    p_c = sv["p_c"]
    (do_c, dc_z), (g["o_norm_g"],) = _local_bwd(
        "gdn_post_bwd", _f_gdn_post, [(sv["o_c"], A_WIDTH, 0), (p_c, C_Z, C_QKV // C_Z)], [p["o_norm_g"]], [dy_c], t,
        [True, True], [True])
    dcq, dck, dcv, dg, dbeta = _delta_bwd(sv["cq"], sv["ck"], sv["cv"], sv["g"], sv["beta"], sv["states"],
                                          _heads_first(do_c))
    dgb = jnp.pad(jnp.concatenate([_unchunk_vec(dbeta), _unchunk_vec(dg)], axis=1),
                  ((0, 0), (0, HEAD_PAD - 2 * N_HEADS)))
    (dconv, dba), (g["a_log"], g["dt_bias"]) = _local_bwd(
        "gdn_pre_bwd", _f_gdn_pre, [(sv["conv_pre"], C_QKV, 0), (p_c, HEAD_PAD, (C_QKV + C_Z) // HEAD_PAD)],
        [p["a_log"], p["dt_bias"]], [_heads_last(dcq), _heads_last(dck), _heads_last(dcv), dgb], t,
        [True, True], [True, True])
    dc_qkv, g["conv_w"] = _conv_bwd(p_c, dconv, p["conv_w"], t)
    dp_c = jnp.concatenate([dc_qkv, dc_z, dba], axis=1)
    dq, = (_flash_bwd_dq(sv["q"], sv["k"], sv["v"], sv["o_b"], sv["lse"], do_b, tb),)
    dk, dv = _flash_bwd_dkv(sv["q"], sv["k"], sv["v"], sv["o_b"], sv["lse"], do_b, tb)
    mla_pars = [p["q_lat_norm_g"], p["kv_lat_norm_g"], p["wq"], p["wk"], p["wv"], p["qn_g"], p["kn_g"]]
    (dp_b,), mla_g = _local_bwd(
        "mla_prep_bwd", _f_mla_prep, [(sv["p_b"], W_B, 0)] + [(tb_, HEAD_PAD, 0) for tb_ in tabs], mla_pars,
        [dq, dk, dv], min(t, 256), [True, False, False, False], [True] * 7)
    for name, val in zip(("q_lat_norm_g", "kv_lat_norm_g", "wq", "wk", "wv", "qn_g", "kn_g"), mla_g):
        g[name] = val
    (dp_a,), (g["sgu_norm_g"], g["w_spatial"], g["b_spatial"]) = _local_bwd(
        "sgu_bwd", _f_sgu, [(sv["p_a"], 2 * A_WIDTH, 0)], [p["sgu_norm_g"], p["w_spatial"], p["b_spatial"]], [dy_a],
        SGU_CHUNK, [True], [True] * 3)
    dh1 = None
    for key, dp in (("a", dp_a), ("b", dp_b), ("c", dp_c), ("g", dp_g)):
        dh1 = _matmul("d_proj_" + key, dp, p["wt_" + key], add=dh1)
        g["w_" + key] = _matmul_tn("dw_proj_" + key, sv["h1"], dp)
    (dx,), (g["norm1_g"],) = _local_bwd("norm1_bwd", _f_norm_res, [(sv["x"], D_MODEL, 0)], [p["norm1_g"]],
                                        [dh1, dx1], t, [True], [True])
    return dx, _unprep_grads(g)


def _local_step(x, positions, w, target, t=512, tb=512):
    s = x.shape[0]
    t = min(t, s)
    half = B_ROPE // 2
    inv_freq = 1.0 / (ROPE_BASE ** (jnp.arange(half, dtype=F32) / half))
    inv_row = jnp.concatenate([jnp.zeros((B_NOPE,), F32), inv_freq, inv_freq, jnp.zeros((HEAD_PAD - B_QK,), F32)])[None, :]
    tabs = _local_fwd("rope_tables", _f_rope_tables, [(positions, 1, 0)], [inv_row], [(HEAD_PAD, F32)] * 3, t)
    preps, saved = [], []
    for l in range(DEPTH):
        preps.append(_prep_layer(w, l))
        x, sv = _layer_fwd(x, preps[l], tabs, t, tb)
        saved.append(sv)
    sq, dx = _loss_head(x, target, t)
    grads = [None] * DEPTH
    for l in reversed(range(DEPTH)):
        dx, grads[l] = _layer_bwd(dx, saved[l], preps[l], tabs, t, tb)
    return sq, dx, {name: jnp.stack([grads[l][name] for l in range(DEPTH)], axis=0) for name in WEIGHTS}


def _exchange(name, send, gather):
    rows = send.shape[-2]

    def body(send_ref, recv_ref, send_sems, recv_sems, local_sem):
        x, y, c = lax.axis_index("x"), lax.axis_index("y"), lax.axis_index("c")
        me = 4 * x + 2 * y + c

        def src(idx):
            return send_ref if gather else send_ref.at[idx]

        mine = pltpu.make_async_copy(src(me), recv_ref.at[me], local_sem)
        mine.start()
        copies = []
        for d in range(1, N_DEV):
            px, py, pc = x ^ ((d >> 2) & 1), y ^ ((d >> 1) & 1), c ^ (d & 1)
            peer = 4 * px + 2 * py + pc
            cp = pltpu.make_async_remote_copy(src_ref=src(peer), dst_ref=recv_ref.at[me], send_sem=send_sems.at[d],
                                              recv_sem=recv_sems.at[d], device_id=(px, py, pc),
                                              device_id_type=pl.DeviceIdType.MESH)
            cp.start()
            copies.append((cp, peer, d))
        for cp, peer, d in copies:
            cp.wait_send()
            pltpu.make_async_remote_copy(src_ref=src(peer), dst_ref=recv_ref.at[peer], send_sem=send_sems.at[d],
                                         recv_sem=recv_sems.at[d], device_id=(x, y, c),
                                         device_id_type=pl.DeviceIdType.MESH).wait_recv()
        mine.wait()

    return pl.pallas_call(
        body, name=name,
        in_specs=[pl.BlockSpec(memory_space=pl.ANY)], out_specs=pl.BlockSpec(memory_space=pl.ANY),
        out_shape=jax.ShapeDtypeStruct((N_DEV, rows, 128), send.dtype),
        scratch_shapes=[pltpu.SemaphoreType.DMA((N_DEV,)), pltpu.SemaphoreType.DMA((N_DEV,)), pltpu.SemaphoreType.DMA],
    )(send)


def _reduce_adamw(recv, w, m, v, tr=512):
    rows = w.shape[0]
    tr = math.gcd(rows, tr)
    c1, c2 = 1.0 - ADAM_B1 ** ADAM_STEP, 1.0 - ADAM_B2 ** ADAM_STEP

    def body(r_ref, w_ref, m_ref, v_ref, g_ref, d_ref, nm_ref, nv_ref):
        g = r_ref[0]
        for j in range(1, N_DEV):
            g = g + r_ref[j]
        m_new = ADAM_B1 * m_ref[...] + (1.0 - ADAM_B1) * g
        v_new = ADAM_B2 * v_ref[...] + (1.0 - ADAM_B2) * jnp.square(g)
        d_ref[...] = -ADAM_LR * ((m_new / c1) / (jnp.sqrt(v_new / c2) + ADAM_EPS) + ADAM_WD * w_ref[...])
        g_ref[...], nm_ref[...], nv_ref[...] = g, m_new, v_new

    flat = pl.BlockSpec((tr, 128), lambda i: (i, 0))
    return pl.pallas_call(
        body, name="reduce_adamw", grid=(rows // tr,),
        in_specs=[pl.BlockSpec((N_DEV, tr, 128), lambda i: (0, i, 0)), flat, flat, flat], out_specs=[flat] * 4,
        out_shape=[jax.ShapeDtypeStruct((rows, 128), F32)] * 4, compiler_params=_params(("parallel",)),
    )(recv, w, m, v)


def _pack(pieces):
    flat = jnp.concatenate([p.reshape(-1) for p in pieces])
    return _pad_to(flat, 0, -(-flat.shape[0] // 1024) * 1024).reshape(-1, 128)


def _unpack(flat, shapes):
    flat = flat.reshape(-1)
    out, off = [], 0
    for shp in shapes:
        n = math.prod(shp)
        out.append(flat[off:off + n].reshape(shp))
        off += n
    return out


def _to_shards(name, full):
    ax = SHARD_AXIS[name]
    shp = full.shape
    return jnp.moveaxis(full.reshape(shp[:ax] + (N_DEV, shp[ax] // N_DEV) + shp[ax + 1:]), ax, 0)


def _from_shards(name, shards):
    ax = SHARD_AXIS[name]
    a = jnp.moveaxis(shards, 0, ax)
    return a.reshape(a.shape[:ax] + (a.shape[ax] * a.shape[ax + 1],) + a.shape[ax + 2:])


def kernel(x, positions, norm1_g, w_in, sgu_norm_g, w_spatial, b_spatial, q_lat_norm_g, w_q_up, kv_lat_norm_g, w_kv_up, q_norm_g, k_norm_g, conv_w, a_log, dt_bias, o_norm_g, w_branch, w_out, norm2_g, w_ff1, w_ff2, loss_target, m_norm1_g, m_w_in, m_sgu_norm_g, m_w_spatial, m_b_spatial, m_q_lat_norm_g, m_w_q_up, m_kv_lat_norm_g, m_w_kv_up, m_q_norm_g, m_k_norm_g, m_conv_w, m_a_log, m_dt_bias, m_o_norm_g, m_w_branch, m_w_out, m_norm2_g, m_w_ff1, m_w_ff2, v_norm1_g, v_w_in, v_sgu_norm_g, v_w_spatial, v_b_spatial, v_q_lat_norm_g, v_w_q_up, v_kv_lat_norm_g, v_w_kv_up, v_q_norm_g, v_k_norm_g, v_conv_w, v_a_log, v_dt_bias, v_o_norm_g, v_w_branch, v_w_out, v_norm2_g, v_w_ff1, v_w_ff2):
    args = locals()
    local_w = {n: args[n] for n in WEIGHTS}
    order = SHARDED + REPLICATED
    shard_shapes = [local_w[n].shape for n in SHARDED]
    gathered = _exchange("gather_weights", _pack([local_w[n] for n in SHARDED]), True)
    full = dict(local_w)
    for n, part in zip(SHARDED, _unpack_rows(gathered, shard_shapes)):
        full[n] = _from_shards(n, part)
    sq, grad_x, grads = _local_step(x[0], positions.reshape(-1, 1), full, loss_target[0])
    loss = lax.psum(sq[0, 0] * (0.5 / D_MODEL), ("x", "y", "c"))
    shard_grads = [_to_shards(n, grads[n]) for n in SHARDED]
    send = jnp.stack([_pack([sg[j] for sg in shard_grads] + [grads[n] for n in REPLICATED]) for j in range(N_DEV)])
    recv = _exchange("exchange_grads", send, False)
    outs = _reduce_adamw(recv, *[_pack([src[n] for n in order]) for src in (
        local_w, {n: args["m_" + n] for n in WEIGHTS}, {n: args["v_" + n] for n in WEIGHTS})])
    shapes = [local_w[n].shape for n in order]
    res = [dict(zip(order, _unpack(o, shapes))) for o in outs]
    return (loss, grad_x[None], *[r[n] for r in res for n in WEIGHTS])


def _unpack_rows(gathered, shapes):
    flat = gathered.reshape(N_DEV, -1)
    out, off = [], 0
    for shp in shapes:
        n = math.prod(shp)
        out.append(flat[:, off:off + n].reshape((N_DEV,) + tuple(shp)))
        off += n
    return out
```

```python
import functools
import math

import jax
import jax.numpy as jnp
from jax import lax
from jax.experimental import pallas as pl
from jax.experimental.pallas import tpu as pltpu

F32, BF16 = jnp.float32, jnp.bfloat16
HI = lax.Precision.HIGHEST
HIGH = lax.Precision.HIGH

N_DEV = 8
D_MODEL = 1024
DEPTH = 2
N_HEADS = 8
HEAD_PAD = 128
A_WIDTH = 512
B_NOPE, B_ROPE, B_VDIM = 64, 32, 64
B_QK = B_NOPE + B_ROPE
B_Q_LORA, B_KV_LORA = 384, 256
ROPE_BASE = 10000.0
C_DK = 64
C_CHUNK = 64
C_QKV = 1536
C_Z = 512
SGU_CHUNK = 128
D_FF = 4096
EPS = 1e-6
ADAM_LR, ADAM_B1, ADAM_B2, ADAM_EPS, ADAM_WD, ADAM_STEP = 0.001, 0.9, 0.999, 1e-08, 0.01, 10
O_QLAT, O_KVLAT, O_KROPE, O_CQKV, O_GATES, D_IN = 1024, 1408, 1664, 1696, 3760, 6832
W_B, W_C = 768, 2176
VMEM_LIMIT = 56 * 2 ** 20
NEG = -1e30

SHARDED = ("w_in", "w_q_up", "w_kv_up", "conv_w", "w_branch", "w_out", "w_ff1", "w_ff2")
EXACT_GATHER = ("conv_w",)
SHARD_AXIS = {"w_in": 2, "w_q_up": 2, "w_kv_up": 2, "conv_w": 2, "w_branch": 3, "w_out": 1, "w_ff1": 2, "w_ff2": 1}
REPLICATED = ("norm1_g", "sgu_norm_g", "w_spatial", "b_spatial", "q_lat_norm_g", "kv_lat_norm_g", "q_norm_g",
              "k_norm_g", "a_log", "dt_bias", "o_norm_g", "norm2_g")
WEIGHTS = ("norm1_g", "w_in", "sgu_norm_g", "w_spatial", "b_spatial", "q_lat_norm_g", "w_q_up", "kv_lat_norm_g",
           "w_kv_up", "q_norm_g", "k_norm_g", "conv_w", "a_log", "dt_bias", "o_norm_g", "w_branch", "w_out",
           "norm2_g", "w_ff1", "w_ff2")


def _params(sem, vmem=VMEM_LIMIT):
    return pltpu.CompilerParams(dimension_semantics=sem, vmem_limit_bytes=vmem)


_FORMS = {"nn": (1, 0), "nt": (1, 1), "tn": (0, 0)}


def _dot(form, a, b, batch, prec=None):
    ca, cb = _FORMS[form]
    o = 1 if batch else 0
    bd = ((0,), (0,)) if batch else ((), ())
    return lax.dot_general(a, b, (((ca + o,), (cb + o,)), bd), precision=prec, preferred_element_type=F32)


def _mm_raw(form, batch, a, b):
    return _dot(form, a.astype(BF16), b.astype(BF16), batch)


@functools.partial(jax.custom_vjp, nondiff_argnums=(0, 1))
def _mm(form, batch, a, b):
    return _mm_raw(form, batch, a, b)


def _mm_fwd(form, batch, a, b):
    return _mm_raw(form, batch, a, b), (a, b)


def _mm_bwd(form, batch, res, g):
    a, b = res
    if form == "nn":
        da, db = _mm_raw("nt", batch, g, b), _mm_raw("tn", batch, a, g)
    elif form == "nt":
        da, db = _mm_raw("nn", batch, g, b), _mm_raw("tn", batch, g, a)
    else:
        da, db = _mm_raw("nt", batch, b, g), _mm_raw("nn", batch, a, g)
    return da.astype(a.dtype), db.astype(b.dtype)


_mm.defvjp(_mm_fwd, _mm_bwd)


def _mmh(form, a, b, batch=False, prec=HI):
    return _dot(form, a, b, batch, prec)


@functools.partial(jax.custom_vjp, nondiff_argnums=(1, 2))
def _roll(x, shift, axis):
    return pltpu.roll(x, shift % x.shape[axis], axis)


def _roll_fwd(x, shift, axis):
    return _roll(x, shift, axis), None


def _roll_bwd(shift, axis, _, g):
    return (_roll(g, -shift, axis),)


_roll.defvjp(_roll_fwd, _roll_bwd)


@jax.custom_vjp
def _tile_heads(x):
    return jnp.concatenate([x] * N_HEADS, axis=1)


def _tile_heads_fwd(x):
    return _tile_heads(x), None


def _tile_heads_bwd(_, g):
    w = g.shape[1] // N_HEADS
    acc = g[:, :w]
    for h in range(1, N_HEADS):
        acc = acc + g[:, h * w:(h + 1) * w]
    return (acc,)


_tile_heads.defvjp(_tile_heads_fwd, _tile_heads_bwd)


def _iota(shape, dim):
    return lax.broadcasted_iota(jnp.int32, shape, dim)


def _head_indicator_t(width, per_head):
    return (_iota((N_HEADS, width), 1) // per_head == _iota((N_HEADS, width), 0)).astype(F32)


def _rms(x, g):
    return x * lax.rsqrt(jnp.mean(x * x, axis=-1, keepdims=True) + EPS) * g


def _gelu(x):
    return 0.5 * x * (1.0 + lax.erf(x * (2.0 ** -0.5)))


def _head_sums(x, per_head):
    blocks = []
    for b in range(x.shape[1] // 128):
        blk = x[:, b * 128:(b + 1) * 128]
        if per_head == 128:
            blocks.append(jnp.broadcast_to(jnp.sum(blk, axis=1, keepdims=True), blk.shape))
        else:
            low = _iota((1, 128), 1) < per_head
            s_low = jnp.sum(jnp.where(low, blk, 0.0), axis=1, keepdims=True)
            s_high = jnp.sum(jnp.where(low, 0.0, blk), axis=1, keepdims=True)
            blocks.append(jnp.where(low, s_low, s_high))
    return jnp.concatenate(blocks, axis=1)


def _head_rms(x, g_full, per_head, n_real):
    return x * lax.rsqrt(_head_sums(x * x, per_head) * (1.0 / n_real) + EPS) * g_full


def _rope(x, cos_t, sin_hi, sin_lo):
    half = B_ROPE // 2
    return (x * _tile_heads(cos_t) + _roll(x, half, 1) * _tile_heads(sin_hi)
            + _roll(x, -half, 1) * _tile_heads(sin_lo))


def _tok_spec(t, width, col):
    return pl.BlockSpec((t, width), lambda i, c=col: (i, c))


def _par_spec(shape):
    nd = len(shape)
    return pl.BlockSpec(shape, lambda i: (0,) * nd)


def _local_fwd(name, f, toks, pars, outs, t):
    s = toks[0][0].shape[0]
    nt, npar = len(toks), len(pars)

    def body(*refs):
        vals = [r[...] for r in refs[:nt + npar]]
        for r, o in zip(refs[nt + npar:], f(*vals)):
            r[...] = o.astype(r.dtype)

    return pl.pallas_call(
        body, name=name, grid=(s // t,),
        in_specs=[_tok_spec(t, w, c) for _, w, c in toks] + [_par_spec(p.shape) for p in pars],
        out_specs=[_tok_spec(t, w, 0) for w, _ in outs],
        out_shape=[jax.ShapeDtypeStruct((s, w), dt) for w, dt in outs],
        compiler_params=_params(("parallel",)),
    )(*[a for a, _, _ in toks], *pars)


def _local_bwd(name, f, toks, pars, cots, t, tok_diff, par_diff):
    s = toks[0][0].shape[0]
    nt, npar, nc = len(toks), len(pars), len(cots)
    dt_idx = [k for k in range(nt) if tok_diff[k]]
    dp_idx = [k for k in range(npar) if par_diff[k]]

    def body(*refs):
        i = pl.program_id(0)
        tv = [r[...] for r in refs[:nt]]
        pv = [r[...] for r in refs[nt:nt + npar]]
        cv = [r[...].astype(F32) for r in refs[nt + npar:nt + npar + nc]]
        out_refs = refs[nt + npar + nc:]

        def g(*d):
            tt, pp = list(tv), list(pv)
            for k, val in zip(dt_idx, d[:len(dt_idx)]):
                tt[k] = val
            for k, val in zip(dp_idx, d[len(dt_idx):]):
                pp[k] = val
            return tuple(o.astype(F32) for o in f(*tt, *pp))

        _, vjp = jax.vjp(g, *[tv[k] for k in dt_idx], *[pv[k] for k in dp_idx])
        grads = vjp(tuple(cv))
        for r, gr in zip(out_refs[:len(dt_idx)], grads[:len(dt_idx)]):
            r[...] = gr.astype(r.dtype)
        par_refs = out_refs[len(dt_idx):]

        @pl.when(i == 0)
        def _():
            for r in par_refs:
                r[...] = jnp.zeros(r.shape, r.dtype)

        for r, gr in zip(par_refs, grads[len(dt_idx):]):
            r[...] += gr.astype(F32)

    res = pl.pallas_call(
        body, name=name, grid=(s // t,),
        in_specs=([_tok_spec(t, w, c) for _, w, c in toks] + [_par_spec(p.shape) for p in pars]
                  + [_tok_spec(t, c.shape[1], 0) for c in cots]),
        out_specs=([_tok_spec(t, toks[k][1], 0) for k in dt_idx] + [_par_spec(pars[k].shape) for k in dp_idx]),
        out_shape=([jax.ShapeDtypeStruct((s, toks[k][1]), F32) for k in dt_idx]
                   + [jax.ShapeDtypeStruct(pars[k].shape, F32) for k in dp_idx]),
        compiler_params=_params(("arbitrary",)),
    )(*[a for a, _, _ in toks], *pars, *cots)
    return res[:len(dt_idx)], res[len(dt_idx):]


def _f_norm(x, g):
    return (_rms(x, g),)


def _f_norm_res(x, g):
    return _rms(x, g), x


def _f_rope_tables(pos, inv_row):
    ang = pos.astype(F32) * inv_row
    lane = _iota(ang.shape, 1)
    sn = jnp.sin(ang)
    half = B_ROPE // 2
    sin_hi = jnp.where((lane >= B_NOPE + half) & (lane < B_QK), sn, 0.0)
    sin_lo = jnp.where((lane >= B_NOPE) & (lane < B_NOPE + half), -sn, 0.0)
    return jnp.cos(ang), sin_hi, sin_lo


def _f_sgu(p_a, g, w_s, b_s):
    u = _gelu(p_a[:, :A_WIDTH])
    v = _rms(_gelu(p_a[:, A_WIDTH:]), g)
    tril = _iota((SGU_CHUNK, SGU_CHUNK), 1) <= _iota((SGU_CHUNK, SGU_CHUNK), 0)
    w_cat = jnp.concatenate([jnp.where(tril, w_s[gi], 0.0) for gi in range(N_HEADS)], axis=1)
    group = _iota((1, A_WIDTH), 1) // (A_WIDTH // N_HEADS)
    v_stack = jnp.concatenate([jnp.where(group == gi, v, 0.0) for gi in range(N_HEADS)], axis=0)
    bias = _mmh("tn", b_s, _head_indicator_t(A_WIDTH, A_WIDTH // N_HEADS))
    return (u * (_mm("nn", False, w_cat, v_stack) + bias),)


def _f_mla_prep(p_b, cos_t, sin_hi, sin_lo, q_lat_g, kv_lat_g, wq, wk, wv, qn_g, kn_g):
    kv_lat, k_rope, q_lat = p_b[:, :B_KV_LORA], p_b[:, B_KV_LORA:B_KV_LORA + HEAD_PAD], p_b[:, B_KV_LORA + HEAD_PAD:]
    q = _mm("nn", False, _rms(q_lat, q_lat_g), wq)
    q = _rope(_head_rms(q, _tile_heads(qn_g), HEAD_PAD, B_QK), cos_t, sin_hi, sin_lo)
    kvn = _rms(kv_lat, kv_lat_g)
    k = _mm("nn", False, kvn, wk) + _tile_heads(k_rope)
    k = _rope(_head_rms(k, _tile_heads(kn_g), HEAD_PAD, B_QK), cos_t, sin_hi, sin_lo)
    return q, k, _mm("nn", False, kvn, wv)


def _f_gdn_pre(conv_pre, ba, a_log_row, dt_row):
    qkv = jax.nn.silu(conv_pre)

    def l2(x):
        return x * lax.rsqrt(_head_sums(x * x, C_DK) + EPS)

    lane = _iota(ba.shape, 1)
    g = -jnp.exp(a_log_row) * jax.nn.softplus(ba + dt_row)
    gb = jnp.where(lane < N_HEADS, jax.nn.sigmoid(ba), jnp.where(lane < 2 * N_HEADS, g, 0.0))
    return l2(qkv[:, :A_WIDTH]), l2(qkv[:, A_WIDTH:2 * A_WIDTH]), qkv[:, 2 * A_WIDTH:], gb


def _f_gdn_post(o, c_z, o_g):
    place = (_iota((C_DK, A_WIDTH), 1) % C_DK == _iota((C_DK, A_WIDTH), 0)).astype(F32)
    return (_head_rms(o, _mmh("nn", o_g, place), C_DK, C_DK) * jax.nn.silu(c_z),)


def _f_merge(p_g, y0, y1, y2):
    d = D_MODEL
    return (jax.nn.sigmoid(p_g[:, :d]) * y0 + jax.nn.sigmoid(p_g[:, d:2 * d]) * y1
            + jax.nn.sigmoid(p_g[:, 2 * d:]) * y2,)


def _f_relu2(a):
    return (jnp.square(jnp.maximum(a, 0.0)),)


def _pick(n, whole_up_to, candidates):
    if n <= whole_up_to:
        return n
    for c in candidates:
        if n % c == 0:
            return c
    return n


def _matmul(name, a, w, add=None, out_dtype=F32, tm=512):
    m, k = a.shape
    n = w.shape[1]
    tm = min(tm, m)
    tk = _pick(k, 2304, (2048, 1536, 1024, 512))
    tn = _pick(n, 2304, (2048, 1536, 1024, 512))
    nk = k // tk

    def body(*refs):
        a_ref, w_ref = refs[0], refs[1]
        add_ref = refs[2] if add is not None else None
        o_ref, acc = refs[-2], refs[-1]
        kk = pl.program_id(2)
        part = jnp.dot(a_ref[...].astype(BF16), w_ref[...].astype(BF16), preferred_element_type=F32)

        @pl.when(kk == 0)
        def _():
            acc[...] = part

        @pl.when(kk > 0)
        def _():
            acc[...] += part

        @pl.when(kk == nk - 1)
        def _():
            r = acc[...]
            if add_ref is not None:
                r = r + add_ref[...]
            o_ref[...] = r.astype(o_ref.dtype)

    in_specs = [pl.BlockSpec((tm, tk), lambda i, j, kk: (i, kk)), pl.BlockSpec((tk, tn), lambda i, j, kk: (kk, j))]
    args = [a, w]
    if add is not None:
        in_specs.append(pl.BlockSpec((tm, tn), lambda i, j, kk: (i, j)))
        args.append(add)
    return pl.pallas_call(
        body, name=name, grid=(m // tm, n // tn, nk), in_specs=in_specs,
        out_specs=pl.BlockSpec((tm, tn), lambda i, j, kk: (i, j)),
        out_shape=jax.ShapeDtypeStruct((m, n), out_dtype),
        scratch_shapes=[pltpu.VMEM((tm, tn), F32)],
        compiler_params=_params(("parallel", "parallel", "arbitrary")),
    )(*args)


def _matmul_tn(name, a, b, a_col=None, tm=512):
    m = a.shape[0]
    k, acol = (a.shape[1], 0) if a_col is None else a_col
    n = b.shape[1]
    tm = min(tm, m)
    tk = _pick(k, 1536, (1024, 512))
    tn = _pick(n, 2304, (1024, 512))
    nm = m // tm

    def body(a_ref, b_ref, o_ref):
        mm = pl.program_id(2)
        part = lax.dot_general(a_ref[...].astype(BF16), b_ref[...].astype(BF16), (((0,), (0,)), ((), ())),
                               preferred_element_type=F32)

        @pl.when(mm == 0)
        def _():
            o_ref[...] = part

        @pl.when(mm > 0)
        def _():
            o_ref[...] += part

    kb = k // tk
    return pl.pallas_call(
        body, name=name, grid=(kb, n // tn, nm),
        in_specs=[pl.BlockSpec((tm, tk), lambda i, j, mm: (mm, acol * kb + i)),
                  pl.BlockSpec((tm, tn), lambda i, j, mm: (mm, j))],
        out_specs=pl.BlockSpec((tk, tn), lambda i, j, mm: (i, j)),
        out_shape=jax.ShapeDtypeStruct((k, n), F32),
        compiler_params=_params(("parallel", "parallel", "arbitrary")),
    )(a, b)


def _shift_down(x, prev, s):
    rolled = pltpu.roll(x, s, 0)
    pr = pltpu.roll(prev, s, 0)
    head = jnp.where(_iota((8, 1), 0) < s, pr, rolled[:8])
    return jnp.concatenate([head, rolled[8:]], axis=0)


def _shift_up(x, nxt, s):
    t = x.shape[0]
    rolled = pltpu.roll(x, t - s, 0)
    nr = pltpu.roll(nxt, 8 - s, 0)
    tail = jnp.where(_iota((8, 1), 0) >= 8 - s, nr, rolled[t - 8:])
    return jnp.concatenate([rolled[:t - 8], tail], axis=0)


def _conv_fwd(p_c, w8, t):
    s = p_c.shape[0]
    t = min(t, s)
    r = t // 8

    def body(x_ref, prev_ref, w_ref, o_ref):
        i = pl.program_id(0)
        x = x_ref[...]
        prev = jnp.where(i == 0, 0.0, prev_ref[...])
        acc = w_ref[3:4, :] * x
        for sh in range(1, 4):
            acc = acc + w_ref[3 - sh:4 - sh, :] * _shift_down(x, prev, sh)
        o_ref[...] = acc

    return pl.pallas_call(
        body, name="conv_fwd", grid=(s // t,),
        in_specs=[pl.BlockSpec((t, C_QKV), lambda i: (i, 0)),
                  pl.BlockSpec((8, C_QKV), lambda i: (jnp.maximum(i * r - 1, 0), 0)),
                  pl.BlockSpec((8, C_QKV), lambda i: (0, 0))],
        out_specs=pl.BlockSpec((t, C_QKV), lambda i: (i, 0)),
        out_shape=jax.ShapeDtypeStruct((s, C_QKV), F32),
        compiler_params=_params(("parallel",)),
    )(p_c, p_c, w8)


def _conv_bwd(p_c, dy, w8, t):
    s = p_c.shape[0]
    t = min(t, s)
    r = t // 8
    n = s // t

    def body(x_ref, prev_ref, dy_ref, next_ref, w_ref, dx_ref, dw_ref):
        i = pl.program_id(0)
        x, g = x_ref[...], dy_ref[...]
        prev = jnp.where(i == 0, 0.0, prev_ref[...])
        nxt = jnp.where(i == n - 1, 0.0, next_ref[...])

        @pl.when(i == 0)
        def _():
            dw_ref[...] = jnp.zeros(dw_ref.shape, F32)

        dx = w_ref[3:4, :] * g
        dw_ref[3:4, :] += jnp.sum(g * x, axis=0, keepdims=True)
        for sh in range(1, 4):
            dx = dx + w_ref[3 - sh:4 - sh, :] * _shift_up(g, nxt, sh)
            dw_ref[3 - sh:4 - sh, :] += jnp.sum(g * _shift_down(x, prev, sh), axis=0, keepdims=True)
        dx_ref[...] = dx

    return pl.pallas_call(
        body, name="conv_bwd", grid=(n,),
        in_specs=[pl.BlockSpec((t, C_QKV), lambda i: (i, 0)),
                  pl.BlockSpec((8, C_QKV), lambda i: (jnp.maximum(i * r - 1, 0), 0)),
                  pl.BlockSpec((t, C_QKV), lambda i: (i, 0)),
                  pl.BlockSpec((8, C_QKV), lambda i: (jnp.minimum((i + 1) * r, s // 8 - 1), 0)),
                  pl.BlockSpec((8, C_QKV), lambda i: (0, 0))],
        out_specs=[pl.BlockSpec((t, C_QKV), lambda i: (i, 0)), pl.BlockSpec((8, C_QKV), lambda i: (0, 0))],
        out_shape=[jax.ShapeDtypeStruct((s, C_QKV), F32), jax.ShapeDtypeStruct((8, C_QKV), F32)],
        compiler_params=_params(("arbitrary",)),
    )(p_c, p_c, dy, dy, w8)


def _delta_chunk(state, q, k, v, g, beta):
    c = C_CHUNK
    row, col = _iota((c, c), 0), _iota((c, c), 1)
    tril, strict = col <= row, col < row
    gc = _mmh("nn", g, (row <= col).astype(F32))
    g_last = jnp.sum(g, axis=1, keepdims=True)
    qs = q * (C_DK ** -0.5)
    decay = jnp.exp(jnp.where(tril, gc[:, :, None] - gc[:, None, :], NEG))
    k_beta, v_beta = k * beta[:, :, None], v * beta[:, :, None]
    x = -jnp.where(strict, _mm("nt", True, k_beta, k) * decay, 0.0)
    powers = [x]
    for _ in range(5):
        powers.append(_mmh("nn", powers[-1], powers[-1], True, HIGH))
    sol = jnp.concatenate([v_beta, k_beta * jnp.exp(gc)[:, :, None]], axis=2)
    for p in reversed(powers):
        sol = sol + _mmh("nn", p, sol, True, HIGH)
    u, w = sol[:, :, :C_DK], sol[:, :, C_DK:]
    intra = jnp.where(tril, _mm("nt", True, qs, k) * decay, 0.0)
    v_new = u - _mm("nn", True, w, state)
    o = _mm("nn", True, qs * jnp.exp(gc)[:, :, None], state) + _mm("nn", True, intra, v_new)
    new_state = (state * jnp.exp(g_last)[:, :, None]
                 + _mm("tn", True, k * jnp.exp(g_last - gc)[:, :, None], v_new))
    return new_state, o


def _delta_specs(n, rev):
    def at(i):
        return n - 1 - i if rev else i
    tok = pl.BlockSpec((N_HEADS, C_CHUNK, C_DK), lambda i: (0, at(i), 0))
    vec = pl.BlockSpec((1, N_HEADS, C_CHUNK), lambda i: (at(i), 0, 0))
    st = pl.BlockSpec((1, N_HEADS, C_DK, C_DK), lambda i: (at(i), 0, 0, 0))
    return tok, vec, st


def _delta_fwd(q, k, v, g, beta):
    s = q.shape[1]
    n = s // C_CHUNK
    tok, vec, st = _delta_specs(n, False)

    def body(q_ref, k_ref, v_ref, g_ref, b_ref, o_ref, st_ref, state):
        @pl.when(pl.program_id(0) == 0)
        def _():
            state[...] = jnp.zeros(state.shape, F32)

        cur = state[...]
        st_ref[0] = cur
        new, o = _delta_chunk(cur, q_ref[...], k_ref[...], v_ref[...], g_ref[0], b_ref[0])
        o_ref[...] = o
        state[...] = new

    return pl.pallas_call(
        body, name="delta_fwd", grid=(n,), in_specs=[tok, tok, tok, vec, vec], out_specs=[tok, st],
        out_shape=[jax.ShapeDtypeStruct((N_HEADS, s, C_DK), F32), jax.ShapeDtypeStruct((n, N_HEADS, C_DK, C_DK), F32)],
        scratch_shapes=[pltpu.VMEM((N_HEADS, C_DK, C_DK), F32)],
        compiler_params=_params(("arbitrary",)),
    )(q, k, v, g, beta)


def _delta_bwd(q, k, v, g, beta, states, do):
    s = q.shape[1]
    n = s // C_CHUNK
    tok, vec, st = _delta_specs(n, True)

    def body(q_ref, k_ref, v_ref, g_ref, b_ref, st_ref, do_ref, dq_ref, dk_ref, dv_ref, dg_ref, db_ref, dstate):
        @pl.when(pl.program_id(0) == 0)
        def _():
            dstate[...] = jnp.zeros(dstate.shape, F32)

        _, vjp = jax.vjp(_delta_chunk, st_ref[0], q_ref[...], k_ref[...], v_ref[...], g_ref[0], b_ref[0])
        dst, dq, dk, dv, dg, db = vjp((dstate[...], do_ref[...]))
        dq_ref[...], dk_ref[...], dv_ref[...] = dq, dk, dv
        dg_ref[0], db_ref[0] = dg, db
        dstate[...] = dst

    tok_shape = jax.ShapeDtypeStruct((N_HEADS, s, C_DK), F32)
    vec_shape = jax.ShapeDtypeStruct((n, N_HEADS, C_CHUNK), F32)
    return pl.pallas_call(
        body, name="delta_bwd", grid=(n,), in_specs=[tok, tok, tok, vec, vec, st, tok],
        out_specs=[tok, tok, tok, vec, vec], out_shape=[tok_shape, tok_shape, tok_shape, vec_shape, vec_shape],
        scratch_shapes=[pltpu.VMEM((N_HEADS, C_DK, C_DK), F32)],
        compiler_params=_params(("arbitrary",)),
    )(q, k, v, g, beta, states, do)


ATT_SCALE = B_QK ** -0.5


SCORE_SCALE_LOG2 = ATT_SCALE * math.log2(math.e)


def _lanes(x, n):
    return x if n == 1 else jnp.concatenate([x] * n, axis=1)


def _scores(a_ref, b_ref):
    return lax.dot_general(a_ref[...], b_ref[...], (((1,), (1,)), ((), ())), preferred_element_type=F32) * SCORE_SCALE_LOG2


def _flash_fwd(q, k, v, tb):
    s = q.shape[0]
    tb = min(tb, s)
    nb = s // tb
    nrep = tb // 128

    def body(q_ref, k_ref, v_ref, o_ref, lse_ref, m_s, l_s, acc):
        i, j = pl.program_id(1), pl.program_id(2)

        @pl.when(j == 0)
        def _():
            m_s[...] = jnp.full(m_s.shape, -jnp.inf, F32)
            l_s[...] = jnp.zeros(l_s.shape, F32)
            acc[...] = jnp.zeros(acc.shape, F32)

        def step(masked):
            sc = _scores(q_ref, k_ref)
            if masked:
                sc = jnp.where(_iota((tb, tb), 1) <= _iota((tb, tb), 0), sc, -jnp.inf)
            m_prev = m_s[...]
            m_new = jnp.maximum(m_prev, jnp.max(sc, axis=1, keepdims=True))
            alpha = jnp.exp2(m_prev - m_new)
            p = jnp.exp2(sc - _lanes(m_new, nrep))
            l_s[...] = alpha * l_s[...] + jnp.sum(p, axis=1, keepdims=True)
            acc[...] = alpha * acc[...] + jnp.dot(p.astype(BF16), v_ref[...], preferred_element_type=F32)
            m_s[...] = m_new

        @pl.when(j < i)
        def _():
            step(False)

        @pl.when(j == i)
        def _():
            step(True)
            o_ref[...] = acc[...] / l_s[...]
            lse_ref[...] = m_s[...] + jnp.log2(l_s[...])

    qs = pl.BlockSpec((tb, HEAD_PAD), lambda h, i, j: (i, h))
    ks = pl.BlockSpec((tb, HEAD_PAD), lambda h, i, j: (jnp.minimum(j, i), h))
    shape = jax.ShapeDtypeStruct((s, N_HEADS * HEAD_PAD), F32)
    return pl.pallas_call(
        body, name="flash_fwd", grid=(N_HEADS, nb, nb), in_specs=[qs, ks, ks],
        out_specs=[qs, qs], out_shape=[shape, shape],
        scratch_shapes=[pltpu.VMEM((tb, 128), F32), pltpu.VMEM((tb, 128), F32), pltpu.VMEM((tb, HEAD_PAD), F32)],
        compiler_params=_params(("parallel", "parallel", "arbitrary")),
    )(q, k, v)


def _f_attn_delta(o, do):
    return (_head_sums(o * do, HEAD_PAD),)


def _row_stats(rep):
    return rep[:, ::HEAD_PAD].T.reshape(N_HEADS, 1, -1)


def _flash_bwd_dq(q, k, v, do, lse_rep, delta_rep, tb):
    s = q.shape[0]
    tb = min(tb, s)
    nb = s // tb
    nrep = tb // 128

    def body(q_ref, k_ref, v_ref, do_ref, lse_ref, dl_ref, dq_ref, acc):
        i, j = pl.program_id(1), pl.program_id(2)

        @pl.when(j == 0)
        def _():
            acc[...] = jnp.zeros(acc.shape, F32)

        def step(masked):
            sc = _scores(q_ref, k_ref)
            if masked:
                sc = jnp.where(_iota((tb, tb), 1) <= _iota((tb, tb), 0), sc, -jnp.inf)
            p = jnp.exp2(sc - _lanes(lse_ref[...], nrep))
            dp = lax.dot_general(do_ref[...].astype(BF16), v_ref[...], (((1,), (1,)), ((), ())), preferred_element_type=F32)
            ds = p * (dp - _lanes(dl_ref[...], nrep))
            acc[...] += jnp.dot(ds.astype(BF16), k_ref[...], preferred_element_type=F32)

        @pl.when(j < i)
        def _():
            step(False)

        @pl.when(j == i)
        def _():
            step(True)
            dq_ref[...] = acc[...] * ATT_SCALE

    qs = pl.BlockSpec((tb, HEAD_PAD), lambda h, i, j: (i, h))
    ks = pl.BlockSpec((tb, HEAD_PAD), lambda h, i, j: (jnp.minimum(j, i), h))
    return pl.pallas_call(
        body, name="flash_bwd_dq", grid=(N_HEADS, nb, nb), in_specs=[qs, ks, ks, qs, qs, qs],
        out_specs=qs, out_shape=jax.ShapeDtypeStruct((s, N_HEADS * HEAD_PAD), F32),
        scratch_shapes=[pltpu.VMEM((tb, HEAD_PAD), F32)],
        compiler_params=_params(("parallel", "parallel", "arbitrary")),
    )(q, k, v, do, lse_rep, delta_rep)


def _flash_bwd_dkv(q, k, v, do, lse_row, delta_row, tb):
    s = q.shape[0]
    tb = min(tb, s)
    nb = s // tb

    def body(q_ref, k_ref, v_ref, do_ref, lse_ref, dl_ref, dk_ref, dv_ref, dk_acc, dv_acc):
        j, i = pl.program_id(1), pl.program_id(2)

        @pl.when(i == 0)
        def _():
            dk_acc[...] = jnp.zeros(dk_acc.shape, F32)
            dv_acc[...] = jnp.zeros(dv_acc.shape, F32)

        def step(masked):
            st = _scores(k_ref, q_ref)
            if masked:
                st = jnp.where(_iota((tb, tb), 0) <= _iota((tb, tb), 1), st, -jnp.inf)
            do = do_ref[...].astype(BF16)
            pt = jnp.exp2(st - lse_ref[0])
            dpt = lax.dot_general(v_ref[...], do, (((1,), (1,)), ((), ())), preferred_element_type=F32)
            dst = pt * (dpt - dl_ref[0])
            dv_acc[...] += jnp.dot(pt.astype(BF16), do, preferred_element_type=F32)
            dk_acc[...] += jnp.dot(dst.astype(BF16), q_ref[...], preferred_element_type=F32)

        @pl.when(i == j)
        def _():
            step(True)

        @pl.when(i > j)
        def _():
            step(False)

        @pl.when(i == nb - 1)
        def _():
            dk_ref[...] = dk_acc[...] * ATT_SCALE
            dv_ref[...] = dv_acc[...]

    qs = pl.BlockSpec((tb, HEAD_PAD), lambda h, j, i: (jnp.maximum(i, j), h))
    ks = pl.BlockSpec((tb, HEAD_PAD), lambda h, j, i: (j, h))
    rs = pl.BlockSpec((1, 1, tb), lambda h, j, i: (h, 0, jnp.maximum(i, j)))
    shape = jax.ShapeDtypeStruct((s, N_HEADS * HEAD_PAD), F32)
    return pl.pallas_call(
        body, name="flash_bwd_dkv", grid=(N_HEADS, nb, nb), in_specs=[qs, ks, ks, qs, rs, rs],
        out_specs=[ks, ks], out_shape=[shape, shape],
        scratch_shapes=[pltpu.VMEM((tb, HEAD_PAD), F32), pltpu.VMEM((tb, HEAD_PAD), F32)],
        compiler_params=_params(("parallel", "parallel", "arbitrary")),
    )(q, k, v, do, lse_row, delta_row)


def _loss_head(y, target, t):
    s, d = y.shape
    t = min(t, s)

    def body(y_ref, t_ref, sum_ref, dy_ref):
        @pl.when(pl.program_id(0) == 0)
        def _():
            sum_ref[...] = jnp.zeros(sum_ref.shape, F32)

        err = y_ref[...] - t_ref[...]
        dy_ref[...] = err * (1.0 / d)
        sum_ref[...] += jnp.broadcast_to(jnp.sum(err * err), sum_ref.shape)

    return pl.pallas_call(
        body, name="loss_head", grid=(s // t,),
        in_specs=[pl.BlockSpec((t, d), lambda i: (i, 0))] * 2,
        out_specs=[pl.BlockSpec((1, 128), lambda i: (0, 0)), pl.BlockSpec((t, d), lambda i: (i, 0))],
        out_shape=[jax.ShapeDtypeStruct((1, 128), F32), jax.ShapeDtypeStruct((s, d), F32)],
        compiler_params=_params(("arbitrary",)),
    )(y, target)


def _pad_to(a, axis, size):
    pad = [(0, 0)] * a.ndim
    pad[axis] = (0, size - a.shape[axis])
    return jnp.pad(a, pad)


def _prep_layer(w, l):
    p = {}
    w_in = w["w_in"][l]
    z = lambda n: jnp.zeros((D_MODEL, n), F32)
    groups = {
        "a": w_in[:, :O_QLAT],
        "b": jnp.concatenate([w_in[:, O_KVLAT:O_KROPE], z(B_NOPE), w_in[:, O_KROPE:O_CQKV], z(HEAD_PAD - B_QK),
                              w_in[:, O_QLAT:O_KVLAT]], axis=1),
        "c": jnp.concatenate([w_in[:, O_CQKV:O_GATES], z(W_C - (O_GATES - O_CQKV))], axis=1),
        "g": w_in[:, O_GATES:],
    }
    for key, val in groups.items():
        p["w_" + key] = val.astype(BF16)
        p["wt_" + key] = val.T.astype(BF16)
    for name in ("norm1_g", "sgu_norm_g", "q_lat_norm_g", "kv_lat_norm_g", "o_norm_g", "norm2_g"):
        p[name] = w[name][l][None, :]
    p["w_spatial"], p["b_spatial"] = w["w_spatial"][l], w["b_spatial"][l]
    p["wq"] = _pad_to(w["w_q_up"][l].astype(F32).reshape(B_Q_LORA, N_HEADS, B_QK), 2, HEAD_PAD).reshape(B_Q_LORA, -1)
    kv = w["w_kv_up"][l].astype(F32).reshape(B_KV_LORA, N_HEADS, B_NOPE + B_VDIM)
    p["wk"] = _pad_to(kv[:, :, :B_NOPE], 2, HEAD_PAD).reshape(B_KV_LORA, -1)
    p["wv"] = _pad_to(kv[:, :, B_NOPE:], 2, HEAD_PAD).reshape(B_KV_LORA, -1)
    p["qn_g"] = _pad_to(w["q_norm_g"][l][None, :], 1, HEAD_PAD)
    p["kn_g"] = _pad_to(w["k_norm_g"][l][None, :], 1, HEAD_PAD)
    p["conv_w"] = _pad_to(w["conv_w"][l], 0, 8)
    row = lambda v: jnp.pad(v[None, :], ((0, 0), (N_HEADS, HEAD_PAD - 2 * N_HEADS)))
    p["a_log"], p["dt_bias"] = row(w["a_log"][l]), row(w["dt_bias"][l])
    wb = w["w_branch"][l]
    wb1 = _pad_to(wb[1].reshape(N_HEADS, B_VDIM, D_MODEL), 1, HEAD_PAD).reshape(-1, D_MODEL)
    for key, val in (("wb0", wb[0]), ("wb1", wb1), ("wb2", wb[2]), ("w_out", w["w_out"][l]),
                     ("w_ff1", w["w_ff1"][l]), ("w_ff2", w["w_ff2"][l])):
        p[key] = val.astype(BF16)
        p[key + "_t"] = val.T.astype(BF16)
    return p


def _unprep_grads(g):
    out = {}
    gb, gc = g["w_b"], g["w_c"]
    out["w_in"] = jnp.concatenate([g["w_a"], gb[:, B_KV_LORA + HEAD_PAD:], gb[:, :B_KV_LORA],
                                   gb[:, B_KV_LORA + B_NOPE:B_KV_LORA + B_QK], gc[:, :O_GATES - O_CQKV], g["w_g"]], axis=1)
    for name in ("norm1_g", "sgu_norm_g", "q_lat_norm_g", "kv_lat_norm_g", "o_norm_g", "norm2_g"):
        out[name] = g[name][0]
    out["w_spatial"], out["b_spatial"] = g["w_spatial"], g["b_spatial"]
    out["w_q_up"] = g["wq"].reshape(B_Q_LORA, N_HEADS, HEAD_PAD)[:, :, :B_QK].reshape(B_Q_LORA, -1)
    gk = g["wk"].reshape(B_KV_LORA, N_HEADS, HEAD_PAD)[:, :, :B_NOPE]
    gv = g["wv"].reshape(B_KV_LORA, N_HEADS, HEAD_PAD)[:, :, :B_VDIM]
    out["w_kv_up"] = jnp.concatenate([gk, gv], axis=2).reshape(B_KV_LORA, -1)
    out["q_norm_g"], out["k_norm_g"] = g["qn_g"][0, :B_QK], g["kn_g"][0, :B_QK]
    out["conv_w"] = g["conv_w"][:4]
    out["a_log"], out["dt_bias"] = g["a_log"][0, N_HEADS:2 * N_HEADS], g["dt_bias"][0, N_HEADS:2 * N_HEADS]
    gb1 = g["wb1"].reshape(N_HEADS, HEAD_PAD, D_MODEL)[:, :B_VDIM].reshape(-1, D_MODEL)
    out["w_branch"] = jnp.stack([g["wb0"], gb1, g["wb2"]], axis=0)
    out["w_out"], out["w_ff1"], out["w_ff2"] = g["w_out"], g["w_ff1"], g["w_ff2"]
    return out


def _heads_first(a):
    s = a.shape[0]
    return a.reshape(s, N_HEADS, C_DK).transpose(1, 0, 2)


def _heads_last(a):
    return a.transpose(1, 0, 2).reshape(a.shape[1], N_HEADS * C_DK)


def _chunk_vec(a):
    return a.reshape(-1, C_CHUNK, N_HEADS).transpose(0, 2, 1)


def _unchunk_vec(a):
    return a.transpose(0, 2, 1).reshape(-1, N_HEADS)


def _layer_fwd(x, p, tabs, t, tb):
    sv = {"x": x}
    h1, = _local_fwd("norm1", _f_norm, [(x, D_MODEL, 0)], [p["norm1_g"]], [(D_MODEL, BF16)], t)
    sv["h1"] = h1
    p_a, p_b, p_c, p_g = (_matmul("proj_" + key, h1, p["w_" + key]) for key in "abcg")
    sv.update(p_a=p_a, p_b=p_b, p_c=p_c, p_g=p_g)
    y_a, = _local_fwd("sgu", _f_sgu, [(p_a, 2 * A_WIDTH, 0)], [p["sgu_norm_g"], p["w_spatial"], p["b_spatial"]],
                      [(A_WIDTH, BF16)], SGU_CHUNK)
    q, k, v = _local_fwd("mla_prep", _f_mla_prep, [(p_b, W_B, 0)] + [(tb_, HEAD_PAD, 0) for tb_ in tabs],
                         [p["q_lat_norm_g"], p["kv_lat_norm_g"], p["wq"], p["wk"], p["wv"], p["qn_g"], p["kn_g"]],
                         [(N_HEADS * HEAD_PAD, BF16)] * 3, min(t, 256))
    o_b, lse = _flash_fwd(q, k, v, tb)
    sv.update(q=q, k=k, v=v, o_b=o_b, lse=lse)
    conv_pre = _conv_fwd(p_c, p["conv_w"], t)
    cq, ck, cv, gb = _local_fwd("gdn_pre", _f_gdn_pre, [(conv_pre, C_QKV, 0), (p_c, HEAD_PAD, (C_QKV + C_Z) // HEAD_PAD)],
                                [p["a_log"], p["dt_bias"]], [(A_WIDTH, F32)] * 3 + [(HEAD_PAD, F32)], t)
    cq, ck, cv = _heads_first(cq), _heads_first(ck), _heads_first(cv)
    beta, g = _chunk_vec(gb[:, :N_HEADS]), _chunk_vec(gb[:, N_HEADS:2 * N_HEADS])
    o_c, states = _delta_fwd(cq, ck, cv, g, beta)
    o_c = _heads_last(o_c)
    sv.update(conv_pre=conv_pre, cq=cq, ck=ck, cv=cv, beta=beta, g=g, states=states, o_c=o_c)
    y_c, = _local_fwd("gdn_post", _f_gdn_post, [(o_c, A_WIDTH, 0), (p_c, C_Z, C_QKV // C_Z)], [p["o_norm_g"]],
                      [(A_WIDTH, BF16)], t)
    y0 = _matmul("branch0", y_a, p["wb0"])
    y1 = _matmul("branch1", o_b, p["wb1"])
    y2 = _matmul("branch2", y_c, p["wb2"])
    merged, = _local_fwd("merge", _f_merge, [(p_g, 3 * D_MODEL, 0), (y0, D_MODEL, 0), (y1, D_MODEL, 0), (y2, D_MODEL, 0)],
                         [], [(D_MODEL, BF16)], t)
    x1 = _matmul("out_proj", merged, p["w_out"], add=x)
    sv.update(y_a=y_a, y_c=y_c, y0=y0, y1=y1, y2=y2, merged=merged, x1=x1)
    h2, = _local_fwd("norm2", _f_norm, [(x1, D_MODEL, 0)], [p["norm2_g"]], [(D_MODEL, BF16)], t)
    a = _matmul("ff1", h2, p["w_ff1"])
    r, = _local_fwd("relu2", _f_relu2, [(a, D_FF, 0)], [], [(D_FF, BF16)], min(t, 256))
    x2 = _matmul("ff2", r, p["w_ff2"], add=x1)
    sv.update(h2=h2, a=a, r=r)
    return x2, sv


def _layer_bwd(dx2, sv, p, tabs, t, tb):
    g = {}
    dr = _matmul("d_ff2", dx2, p["w_ff2_t"])
    g["w_ff2"] = _matmul_tn("dw_ff2", sv["r"], dx2)
    (da,), _ = _local_bwd("relu2_bwd", _f_relu2, [(sv["a"], D_FF, 0)], [], [dr], min(t, 256), [True], [])
    dh2 = _matmul("d_ff1", da, p["w_ff1_t"])
    g["w_ff1"] = _matmul_tn("dw_ff1", sv["h2"], da)
    (dx1,), (g["norm2_g"],) = _local_bwd("norm2_bwd", _f_norm_res, [(sv["x1"], D_MODEL, 0)], [p["norm2_g"]],
                                         [dh2, dx2], t, [True], [True])
    dmerged = _matmul("d_out_proj", dx1, p["w_out_t"])
    g["w_out"] = _matmul_tn("dw_out", sv["merged"], dx1)
    (dp_g, dy0, dy1, dy2), _ = _local_bwd(
        "merge_bwd", _f_merge, [(sv["p_g"], 3 * D_MODEL, 0), (sv["y0"], D_MODEL, 0), (sv["y1"], D_MODEL, 0),
                                (sv["y2"], D_MODEL, 0)], [], [dmerged], min(t, 256), [True] * 4, [])
    dy_a = _matmul("d_branch0", dy0, p["wb0_t"])
    do_b = _matmul("d_branch1", dy1, p["wb1_t"])
    dy_c = _matmul("d_branch2", dy2, p["wb2_t"])
    g["wb0"] = _matmul_tn("dw_branch0", sv["y_a"], dy0)
    g["wb1"] = _matmul_tn("dw_branch1", sv["o_b"], dy1)
    g["wb2"] = _matmul_tn("dw_branch2", sv["y_c"], dy2)
    p_c = sv["p_c"]
    (do_c, dc_z), (g["o_norm_g"],) = _local_bwd(
        "gdn_post_bwd", _f_gdn_post, [(sv["o_c"], A_WIDTH, 0), (p_c, C_Z, C_QKV // C_Z)], [p["o_norm_g"]], [dy_c], t,
        [True, True], [True])
    dcq, dck, dcv, dg, dbeta = _delta_bwd(sv["cq"], sv["ck"], sv["cv"], sv["g"], sv["beta"], sv["states"],
                                          _heads_first(do_c))
    dgb = jnp.pad(jnp.concatenate([_unchunk_vec(dbeta), _unchunk_vec(dg)], axis=1),
                  ((0, 0), (0, HEAD_PAD - 2 * N_HEADS)))
    (dconv, dba), (g["a_log"], g["dt_bias"]) = _local_bwd(
        "gdn_pre_bwd", _f_gdn_pre, [(sv["conv_pre"], C_QKV, 0), (p_c, HEAD_PAD, (C_QKV + C_Z) // HEAD_PAD)],
        [p["a_log"], p["dt_bias"]], [_heads_last(dcq), _heads_last(dck), _heads_last(dcv), dgb], t,
        [True, True], [True, True])
    dc_qkv, g["conv_w"] = _conv_bwd(p_c, dconv, p["conv_w"], t)
    dp_c = jnp.concatenate([dc_qkv, dc_z, dba], axis=1)
    delta, = _local_fwd("attn_delta", _f_attn_delta, [(sv["o_b"], N_HEADS * HEAD_PAD, 0), (do_b, N_HEADS * HEAD_PAD, 0)], [],
                        [(N_HEADS * HEAD_PAD, F32)], t)
    dq = _flash_bwd_dq(sv["q"], sv["k"], sv["v"], do_b, sv["lse"], delta, tb)
    dk, dv = _flash_bwd_dkv(sv["q"], sv["k"], sv["v"], do_b, _row_stats(sv["lse"]), _row_stats(delta), tb)
    mla_pars = [p["q_lat_norm_g"], p["kv_lat_norm_g"], p["wq"], p["wk"], p["wv"], p["qn_g"], p["kn_g"]]
    (dp_b,), mla_g = _local_bwd(
        "mla_prep_bwd", _f_mla_prep, [(sv["p_b"], W_B, 0)] + [(tb_, HEAD_PAD, 0) for tb_ in tabs], mla_pars,
        [dq, dk, dv], min(t, 256), [True, False, False, False], [True] * 7)
    for name, val in zip(("q_lat_norm_g", "kv_lat_norm_g", "wq", "wk", "wv", "qn_g", "kn_g"), mla_g):
        g[name] = val
    (dp_a,), (g["sgu_norm_g"], g["w_spatial"], g["b_spatial"]) = _local_bwd(
        "sgu_bwd", _f_sgu, [(sv["p_a"], 2 * A_WIDTH, 0)], [p["sgu_norm_g"], p["w_spatial"], p["b_spatial"]], [dy_a],
        SGU_CHUNK, [True], [True] * 3)
    dh1 = None
    for key, dp in (("a", dp_a), ("b", dp_b), ("c", dp_c), ("g", dp_g)):
        dh1 = _matmul("d_proj_" + key, dp, p["wt_" + key], add=dh1)
        g["w_" + key] = _matmul_tn("dw_proj_" + key, sv["h1"], dp)
    (dx,), (g["norm1_g"],) = _local_bwd("norm1_bwd", _f_norm_res, [(sv["x"], D_MODEL, 0)], [p["norm1_g"]],
                                        [dh1, dx1], t, [True], [True])
    return dx, _unprep_grads(g)


def _local_step(x, positions, w, target, t=512, tb=1024):
    s = x.shape[0]
    t = min(t, s)
    half = B_ROPE // 2
    inv_freq = 1.0 / (ROPE_BASE ** (jnp.arange(half, dtype=F32) / half))
    inv_row = jnp.concatenate([jnp.zeros((B_NOPE,), F32), inv_freq, inv_freq, jnp.zeros((HEAD_PAD - B_QK,), F32)])[None, :]
    tabs = _local_fwd("rope_tables", _f_rope_tables, [(positions, 1, 0)], [inv_row], [(HEAD_PAD, F32)] * 3, t)
    preps, saved = [], []
    for l in range(DEPTH):
        preps.append(_prep_layer(w, l))
        x, sv = _layer_fwd(x, preps[l], tabs, t, tb)
        saved.append(sv)
    sq, dx = _loss_head(x, target, t)
    grads = [None] * DEPTH
    for l in reversed(range(DEPTH)):
        dx, grads[l] = _layer_bwd(dx, saved[l], preps[l], tabs, t, tb)
    return sq, dx, {name: jnp.stack([grads[l][name] for l in range(DEPTH)], axis=0) for name in WEIGHTS}


def _exchange(name, send, gather):
    rows = send.shape[-2]

    def body(send_ref, recv_ref, send_sems, recv_sems, local_sem):
        x, y, c = lax.axis_index("x"), lax.axis_index("y"), lax.axis_index("c")
        me = 4 * x + 2 * y + c

        def src(idx):
            return send_ref if gather else send_ref.at[idx]

        mine = pltpu.make_async_copy(src(me), recv_ref.at[me], local_sem)
        mine.start()
        copies = []
        for d in range(1, N_DEV):
            px, py, pc = x ^ ((d >> 2) & 1), y ^ ((d >> 1) & 1), c ^ (d & 1)
            peer = 4 * px + 2 * py + pc
            cp = pltpu.make_async_remote_copy(src_ref=src(peer), dst_ref=recv_ref.at[me], send_sem=send_sems.at[d],
                                              recv_sem=recv_sems.at[d], device_id=(px, py, pc),
                                              device_id_type=pl.DeviceIdType.MESH)
            cp.start()
            copies.append((cp, peer, d))
        for cp, peer, d in copies:
            cp.wait_send()
            pltpu.make_async_remote_copy(src_ref=src(peer), dst_ref=recv_ref.at[peer], send_sem=send_sems.at[d],
                                         recv_sem=recv_sems.at[d], device_id=(x, y, c),
                                         device_id_type=pl.DeviceIdType.MESH).wait_recv()
        mine.wait()

    return pl.pallas_call(
        body, name=name,
        in_specs=[pl.BlockSpec(memory_space=pl.ANY)], out_specs=pl.BlockSpec(memory_space=pl.ANY),
        out_shape=jax.ShapeDtypeStruct((N_DEV, rows, 128), send.dtype),
        scratch_shapes=[pltpu.SemaphoreType.DMA((N_DEV,)), pltpu.SemaphoreType.DMA((N_DEV,)), pltpu.SemaphoreType.DMA],
    )(send)


def _reduce_adamw(recv, w, m, v, tr):
    rows = w.shape[0]
    assert rows % tr == 0
    c1, c2 = 1.0 - ADAM_B1 ** ADAM_STEP, 1.0 - ADAM_B2 ** ADAM_STEP

    def body(r_ref, w_ref, m_ref, v_ref, g_ref, d_ref, nm_ref, nv_ref):
        g = r_ref[0]
        for j in range(1, N_DEV):
            g = g + r_ref[j]
        m_new = ADAM_B1 * m_ref[...] + (1.0 - ADAM_B1) * g
        v_new = ADAM_B2 * v_ref[...] + (1.0 - ADAM_B2) * jnp.square(g)
        d_ref[...] = -ADAM_LR * ((m_new / c1) / (jnp.sqrt(v_new / c2) + ADAM_EPS) + ADAM_WD * w_ref[...])
        g_ref[...], nm_ref[...], nv_ref[...] = g, m_new, v_new

    flat = pl.BlockSpec((tr, 128), lambda i: (i, 0))
    return pl.pallas_call(
        body, name="reduce_adamw", grid=(rows // tr,),
        in_specs=[pl.BlockSpec((N_DEV, tr, 128), lambda i: (0, i, 0)), flat, flat, flat], out_specs=[flat] * 4,
        out_shape=[jax.ShapeDtypeStruct((rows, 128), F32)] * 4, compiler_params=_params(("parallel",)),
    )(recv, w, m, v)


PACK_ROWS = 512


def _pack(cols):
    flat = jnp.concatenate(cols, axis=-1)
    tile = PACK_ROWS * 128
    flat = _pad_to(flat, flat.ndim - 1, -(-flat.shape[-1] // tile) * tile)
    return flat.reshape(flat.shape[:-1] + (-1, 128))


def _unpack(packed, shapes, lead=()):
    flat = packed.reshape(lead + (-1,))
    out, off = [], 0
    for shp in shapes:
        n = math.prod(shp)
        out.append(flat[..., off:off + n].reshape(lead + tuple(shp)))
        off += n
    return out


def _to_shards(name, full):
    ax = SHARD_AXIS[name]
    shp = full.shape
    return jnp.moveaxis(full.reshape(shp[:ax] + (N_DEV, shp[ax] // N_DEV) + shp[ax + 1:]), ax, 0)


def _from_shards(name, shards):
    ax = SHARD_AXIS[name]
    a = jnp.moveaxis(shards, 0, ax)
    return a.reshape(a.shape[:ax] + (a.shape[ax] * a.shape[ax + 1],) + a.shape[ax + 2:])


def kernel(x, positions, norm1_g, w_in, sgu_norm_g, w_spatial, b_spatial, q_lat_norm_g, w_q_up, kv_lat_norm_g, w_kv_up, q_norm_g, k_norm_g, conv_w, a_log, dt_bias, o_norm_g, w_branch, w_out, norm2_g, w_ff1, w_ff2, loss_target, m_norm1_g, m_w_in, m_sgu_norm_g, m_w_spatial, m_b_spatial, m_q_lat_norm_g, m_w_q_up, m_kv_lat_norm_g, m_w_kv_up, m_q_norm_g, m_k_norm_g, m_conv_w, m_a_log, m_dt_bias, m_o_norm_g, m_w_branch, m_w_out, m_norm2_g, m_w_ff1, m_w_ff2, v_norm1_g, v_w_in, v_sgu_norm_g, v_w_spatial, v_b_spatial, v_q_lat_norm_g, v_w_q_up, v_kv_lat_norm_g, v_w_kv_up, v_q_norm_g, v_k_norm_g, v_conv_w, v_a_log, v_dt_bias, v_o_norm_g, v_w_branch, v_w_out, v_norm2_g, v_w_ff1, v_w_ff2):
    args = locals()
    local_w = {n: args[n] for n in WEIGHTS}
    order = SHARDED + REPLICATED
    wire = [lax.bitcast_convert_type(local_w[n], BF16) if n in EXACT_GATHER else local_w[n].astype(BF16)
            for n in SHARDED]
    gathered = _exchange("gather_weights", _pack([a.reshape(-1) for a in wire]), True)
    full = dict(local_w)
    for n, part in zip(SHARDED, _unpack(gathered, [a.shape for a in wire], (N_DEV,))):
        full[n] = _from_shards(n, lax.bitcast_convert_type(part, F32) if n in EXACT_GATHER else part)
    sq, grad_x, grads = _local_step(x[0], positions.reshape(-1, 1), full, loss_target[0])
    loss = lax.psum(sq[0, 0] * (0.5 / D_MODEL), ("x", "y", "c"))
    rep = jnp.concatenate([grads[n].reshape(-1) for n in REPLICATED])
    send = _pack([_to_shards(n, grads[n]).reshape(N_DEV, -1) for n in SHARDED]
                 + [jnp.broadcast_to(rep[None], (N_DEV, rep.shape[0]))])
    recv = _exchange("exchange_grads", send, False)
    outs = _reduce_adamw(recv, *[_pack([src[n].reshape(-1) for n in order]) for src in (
        local_w, {n: args["m_" + n] for n in WEIGHTS}, {n: args["v_" + n] for n in WEIGHTS})], PACK_ROWS)
    shapes = [local_w[n].shape for n in order]
    res = [dict(zip(order, _unpack(o, shapes))) for o in outs]
    return (loss, grad_x[None], *[r[n] for r in res for n in WEIGHTS])
```

```python
import functools
import math

import jax
import jax.numpy as jnp
from jax import lax
from jax.experimental import pallas as pl
from jax.experimental.pallas import tpu as pltpu

F32, BF16 = jnp.float32, jnp.bfloat16
HI = lax.Precision.HIGHEST

N_DEV = 8
D_MODEL = 1024
DEPTH = 2
N_HEADS = 8
HEAD_PAD = 128
A_WIDTH = 512
B_NOPE, B_ROPE, B_VDIM = 64, 32, 64
B_QK = B_NOPE + B_ROPE
B_Q_LORA, B_KV_LORA = 384, 256
ROPE_BASE = 10000.0
C_DK = 64
C_CHUNK = 64
DELTA_SUB = 2
C_QKV = 1536
C_Z = 512
SGU_CHUNK = 128
D_FF = 4096
EPS = 1e-6
ADAM_LR, ADAM_B1, ADAM_B2, ADAM_EPS, ADAM_WD, ADAM_STEP = 0.001, 0.9, 0.999, 1e-08, 0.01, 10
O_QLAT, O_KVLAT, O_KROPE, O_CQKV, O_GATES, D_IN = 1024, 1408, 1664, 1696, 3760, 6832
W_B, W_C = 768, 2176
VMEM_LIMIT = 56 * 2 ** 20
NEG = -1e30

SHARDED = ("w_in", "w_q_up", "w_kv_up", "conv_w", "w_branch", "w_out", "w_ff1", "w_ff2")
EXACT_GATHER = ("conv_w",)
SHARD_AXIS = {"w_in": 2, "w_q_up": 2, "w_kv_up": 2, "conv_w": 2, "w_branch": 3, "w_out": 1, "w_ff1": 2, "w_ff2": 1}
REPLICATED = ("norm1_g", "sgu_norm_g", "w_spatial", "b_spatial", "q_lat_norm_g", "kv_lat_norm_g", "q_norm_g",
              "k_norm_g", "a_log", "dt_bias", "o_norm_g", "norm2_g")
WEIGHTS = ("norm1_g", "w_in", "sgu_norm_g", "w_spatial", "b_spatial", "q_lat_norm_g", "w_q_up", "kv_lat_norm_g",
           "w_kv_up", "q_norm_g", "k_norm_g", "conv_w", "a_log", "dt_bias", "o_norm_g", "w_branch", "w_out",
           "norm2_g", "w_ff1", "w_ff2")


def _params(sem, vmem=VMEM_LIMIT):
    return pltpu.CompilerParams(dimension_semantics=sem, vmem_limit_bytes=vmem)


_FORMS = {"nn": (1, 0), "nt": (1, 1), "tn": (0, 0)}


def _dot(form, a, b, batch, prec=None):
    ca, cb = _FORMS[form]
    o = 1 if batch else 0
    bd = ((0,), (0,)) if batch else ((), ())
    return lax.dot_general(a, b, (((ca + o,), (cb + o,)), bd), precision=prec, preferred_element_type=F32)


def _mm_raw(form, batch, a, b):
    return _dot(form, a.astype(BF16), b.astype(BF16), batch)


@functools.partial(jax.custom_vjp, nondiff_argnums=(0, 1))
def _mm(form, batch, a, b):
    return _mm_raw(form, batch, a, b)


def _mm_fwd(form, batch, a, b):
    return _mm_raw(form, batch, a, b), (a, b)


def _mm_bwd(form, batch, res, g):
    a, b = res
    if form == "nn":
        da, db = _mm_raw("nt", batch, g, b), _mm_raw("tn", batch, a, g)
    elif form == "nt":
        da, db = _mm_raw("nn", batch, g, b), _mm_raw("tn", batch, g, a)
    else:
        da, db = _mm_raw("nt", batch, b, g), _mm_raw("nn", batch, a, g)
    return da.astype(a.dtype), db.astype(b.dtype)


_mm.defvjp(_mm_fwd, _mm_bwd)


def _mmh(form, a, b, batch=False):
    return _dot(form, a, b, batch, HI)


@functools.partial(jax.custom_vjp, nondiff_argnums=(1, 2))
def _roll(x, shift, axis):
    return pltpu.roll(x, shift % x.shape[axis], axis)


def _roll_fwd(x, shift, axis):
    return _roll(x, shift, axis), None


def _roll_bwd(shift, axis, _, g):
    return (_roll(g, -shift, axis),)


_roll.defvjp(_roll_fwd, _roll_bwd)


@jax.custom_vjp
def _tile_heads(x):
    return jnp.concatenate([x] * N_HEADS, axis=1)


def _tile_heads_fwd(x):
    return _tile_heads(x), None


def _tile_heads_bwd(_, g):
    w = g.shape[1] // N_HEADS
    acc = g[:, :w]
    for h in range(1, N_HEADS):
        acc = acc + g[:, h * w:(h + 1) * w]
    return (acc,)


_tile_heads.defvjp(_tile_heads_fwd, _tile_heads_bwd)


def _iota(shape, dim):
    return lax.broadcasted_iota(jnp.int32, shape, dim)


def _head_indicator_t(width, per_head):
    return (_iota((N_HEADS, width), 1) // per_head == _iota((N_HEADS, width), 0)).astype(F32)


def _rms(x, g):
    return x * lax.rsqrt(jnp.mean(x * x, axis=-1, keepdims=True) + EPS) * g


def _gelu(x):
    return 0.5 * x * (1.0 + lax.erf(x * (2.0 ** -0.5)))


def _head_sums(x, per_head):
    blocks = []
    for b in range(x.shape[1] // 128):
        blk = x[:, b * 128:(b + 1) * 128]
        if per_head == 128:
            blocks.append(jnp.broadcast_to(jnp.sum(blk, axis=1, keepdims=True), blk.shape))
        else:
            low = _iota((1, 128), 1) < per_head
            s_low = jnp.sum(jnp.where(low, blk, 0.0), axis=1, keepdims=True)
            s_high = jnp.sum(jnp.where(low, 0.0, blk), axis=1, keepdims=True)
            blocks.append(jnp.where(low, s_low, s_high))
    return jnp.concatenate(blocks, axis=1)


def _head_rms(x, g_full, per_head, n_real):
    return x * lax.rsqrt(_head_sums(x * x, per_head) * (1.0 / n_real) + EPS) * g_full


def _rope(x, cos_t, sin_hi, sin_lo):
    half = B_ROPE // 2
    return (x * _tile_heads(cos_t) + _roll(x, half, 1) * _tile_heads(sin_hi)
            + _roll(x, -half, 1) * _tile_heads(sin_lo))


def _tok_spec(t, width, col):
    return pl.BlockSpec((t, width), lambda i, c=col: (i, c))


def _par_spec(shape):
    nd = len(shape)
    return pl.BlockSpec(shape, lambda i: (0,) * nd)


def _local_fwd(name, f, toks, pars, outs, t):
    s = toks[0][0].shape[0]
    nt, npar = len(toks), len(pars)

    def body(*refs):
        vals = [r[...] for r in refs[:nt + npar]]
        for r, o in zip(refs[nt + npar:], f(*vals)):
            r[...] = o.astype(r.dtype)

    return pl.pallas_call(
        body, name=name, grid=(s // t,),
        in_specs=[_tok_spec(t, w, c) for _, w, c in toks] + [_par_spec(p.shape) for p in pars],
        out_specs=[_tok_spec(t, w, 0) for w, _ in outs],
        out_shape=[jax.ShapeDtypeStruct((s, w), dt) for w, dt in outs],
        compiler_params=_params(("parallel",)),
    )(*[a for a, _, _ in toks], *pars)


def _local_bwd(name, f, toks, pars, cots, t, tok_diff, par_diff):
    s = toks[0][0].shape[0]
    nt, npar, nc = len(toks), len(pars), len(cots)
    dt_idx = [k for k in range(nt) if tok_diff[k]]
    dp_idx = [k for k in range(npar) if par_diff[k]]

    def body(*refs):
        i = pl.program_id(0)
        tv = [r[...] for r in refs[:nt]]
        pv = [r[...] for r in refs[nt:nt + npar]]
        cv = [r[...].astype(F32) for r in refs[nt + npar:nt + npar + nc]]
        out_refs = refs[nt + npar + nc:]

        def g(*d):
            tt, pp = list(tv), list(pv)
            for k, val in zip(dt_idx, d[:len(dt_idx)]):
                tt[k] = val
            for k, val in zip(dp_idx, d[len(dt_idx):]):
                pp[k] = val
            return tuple(o.astype(F32) for o in f(*tt, *pp))

        _, vjp = jax.vjp(g, *[tv[k] for k in dt_idx], *[pv[k] for k in dp_idx])
        grads = vjp(tuple(cv))
        for r, gr in zip(out_refs[:len(dt_idx)], grads[:len(dt_idx)]):
            r[...] = gr.astype(r.dtype)
        par_refs = out_refs[len(dt_idx):]

        @pl.when(i == 0)
        def _():
            for r in par_refs:
                r[...] = jnp.zeros(r.shape, r.dtype)

        for r, gr in zip(par_refs, grads[len(dt_idx):]):
            r[...] += gr.astype(F32)

    res = pl.pallas_call(
        body, name=name, grid=(s // t,),
        in_specs=([_tok_spec(t, w, c) for _, w, c in toks] + [_par_spec(p.shape) for p in pars]
                  + [_tok_spec(t, c.shape[1], 0) for c in cots]),
        out_specs=([_tok_spec(t, toks[k][1], 0) for k in dt_idx] + [_par_spec(pars[k].shape) for k in dp_idx]),
        out_shape=([jax.ShapeDtypeStruct((s, toks[k][1]), F32) for k in dt_idx]
                   + [jax.ShapeDtypeStruct(pars[k].shape, F32) for k in dp_idx]),
        compiler_params=_params(("arbitrary",)),
    )(*[a for a, _, _ in toks], *pars, *cots)
    return res[:len(dt_idx)], res[len(dt_idx):]


def _f_norm(x, g):
    return (_rms(x, g),)


def _f_norm_res(x, g):
    return _rms(x, g), x


def _f_rope_tables(pos, inv_row):
    ang = pos.astype(F32) * inv_row
    lane = _iota(ang.shape, 1)
    sn = jnp.sin(ang)
    half = B_ROPE // 2
    sin_hi = jnp.where((lane >= B_NOPE + half) & (lane < B_QK), sn, 0.0)
    sin_lo = jnp.where((lane >= B_NOPE) & (lane < B_NOPE + half), -sn, 0.0)
    return jnp.cos(ang), sin_hi, sin_lo


def _f_sgu(p_a, g, w_s, b_s):
    u = _gelu(p_a[:, :A_WIDTH])
    v = _rms(_gelu(p_a[:, A_WIDTH:]), g)
    tril = _iota((SGU_CHUNK, SGU_CHUNK), 1) <= _iota((SGU_CHUNK, SGU_CHUNK), 0)
    w_cat = jnp.concatenate([jnp.where(tril, w_s[gi], 0.0) for gi in range(N_HEADS)], axis=1)
    group = _iota((1, A_WIDTH), 1) // (A_WIDTH // N_HEADS)
    v_stack = jnp.concatenate([jnp.where(group == gi, v, 0.0) for gi in range(N_HEADS)], axis=0)
    bias = _mmh("tn", b_s, _head_indicator_t(A_WIDTH, A_WIDTH // N_HEADS))
    return (u * (_mm("nn", False, w_cat, v_stack) + bias),)


def _f_mla_prep(p_b, cos_t, sin_hi, sin_lo, q_lat_g, kv_lat_g, wq, wk, wv, qn_g, kn_g):
    kv_lat, k_rope, q_lat = p_b[:, :B_KV_LORA], p_b[:, B_KV_LORA:B_KV_LORA + HEAD_PAD], p_b[:, B_KV_LORA + HEAD_PAD:]
    q = _mm("nn", False, _rms(q_lat, q_lat_g), wq)
    q = _rope(_head_rms(q, _tile_heads(qn_g), HEAD_PAD, B_QK), cos_t, sin_hi, sin_lo)
    kvn = _rms(kv_lat, kv_lat_g)
    k = _mm("nn", False, kvn, wk) + _tile_heads(k_rope)
    k = _rope(_head_rms(k, _tile_heads(kn_g), HEAD_PAD, B_QK), cos_t, sin_hi, sin_lo)
    return q, k, _mm("nn", False, kvn, wv)


def _f_gdn_pre(conv_pre, ba, a_log_row, dt_row):
    qkv = jax.nn.silu(conv_pre)

    def l2(x):
        return x * lax.rsqrt(_head_sums(x * x, C_DK) + EPS)

    lane = _iota(ba.shape, 1)
    g = -jnp.exp(a_log_row) * jax.nn.softplus(ba + dt_row)
    gb = jnp.where(lane < N_HEADS, jax.nn.sigmoid(ba), jnp.where(lane < 2 * N_HEADS, g, 0.0))
    return l2(qkv[:, :A_WIDTH]), l2(qkv[:, A_WIDTH:2 * A_WIDTH]), qkv[:, 2 * A_WIDTH:], gb


def _f_gdn_post(o, c_z, o_g):
    place = (_iota((C_DK, A_WIDTH), 1) % C_DK == _iota((C_DK, A_WIDTH), 0)).astype(F32)
    return (_head_rms(o, _mmh("nn", o_g, place), C_DK, C_DK) * jax.nn.silu(c_z),)


def _f_merge(p_g, y0, y1, y2):
    d = D_MODEL
    return (jax.nn.sigmoid(p_g[:, :d]) * y0 + jax.nn.sigmoid(p_g[:, d:2 * d]) * y1
            + jax.nn.sigmoid(p_g[:, 2 * d:]) * y2,)


def _pick(n, whole_up_to, candidates):
    if n <= whole_up_to:
        return n
    for c in candidates:
        if n % c == 0:
            return c
    return n


def _matmul(name, a, w, add=None, out_dtype=F32, tm=512, extras=(), epilogue=None, out_dtypes=None):
    m, k = a.shape
    n = w.shape[1]
    tm = min(tm, m)
    tk = _pick(k, 2304, (2048, 1536, 1024, 512))
    tn = _pick(n, 2304, (2048, 1536, 1024, 512))
    nk = k // tk
    if add is not None:
        extras, epilogue = (add,), lambda r, x: (r + x,)
    if epilogue is None:
        epilogue = lambda r: (r,)
    out_dtypes = out_dtypes or (out_dtype,)
    n_ex, n_out = len(extras), len(out_dtypes)

    def body(*refs):
        a_ref, w_ref = refs[0], refs[1]
        ex_refs, o_refs, acc = refs[2:2 + n_ex], refs[2 + n_ex:2 + n_ex + n_out], refs[-1]
        kk = pl.program_id(2)
        part = jnp.dot(a_ref[...].astype(BF16), w_ref[...].astype(BF16), preferred_element_type=F32)

        @pl.when(kk == 0)
        def _():
            acc[...] = part

        @pl.when(kk > 0)
        def _():
            acc[...] += part

        @pl.when(kk == nk - 1)
        def _():
            for o_ref, r in zip(o_refs, epilogue(acc[...], *[e[...] for e in ex_refs])):
                o_ref[...] = r.astype(o_ref.dtype)

    tile = pl.BlockSpec((tm, tn), lambda i, j, kk: (i, j))
    res = pl.pallas_call(
        body, name=name, grid=(m // tm, n // tn, nk),
        in_specs=[pl.BlockSpec((tm, tk), lambda i, j, kk: (i, kk)), pl.BlockSpec((tk, tn), lambda i, j, kk: (kk, j))]
        + [tile] * n_ex,
        out_specs=[tile] * n_out, out_shape=[jax.ShapeDtypeStruct((m, n), dt) for dt in out_dtypes],
        scratch_shapes=[pltpu.VMEM((tm, tn), F32)],
        compiler_params=_params(("parallel", "parallel", "arbitrary")),
    )(a, w, *extras)
    return res[0] if n_out == 1 else res


def _matmul_tn(name, a, b, a_col=None, tm=512):
    m = a.shape[0]
    k, acol = (a.shape[1], 0) if a_col is None else a_col
    n = b.shape[1]
    tm = min(tm, m)
    tk = _pick(k, 1536, (1024, 512))
    tn = _pick(n, 2304, (1024, 512))
    nm = m // tm

    def body(a_ref, b_ref, o_ref):
        mm = pl.program_id(2)
        part = lax.dot_general(a_ref[...].astype(BF16), b_ref[...].astype(BF16), (((0,), (0,)), ((), ())),
                               preferred_element_type=F32)

        @pl.when(mm == 0)
        def _():
            o_ref[...] = part

        @pl.when(mm > 0)
        def _():
            o_ref[...] += part

    kb = k // tk
    return pl.pallas_call(
        body, name=name, grid=(kb, n // tn, nm),
        in_specs=[pl.BlockSpec((tm, tk), lambda i, j, mm: (mm, acol * kb + i)),
                  pl.BlockSpec((tm, tn), lambda i, j, mm: (mm, j))],
        out_specs=pl.BlockSpec((tk, tn), lambda i, j, mm: (i, j)),
        out_shape=jax.ShapeDtypeStruct((k, n), F32),
        compiler_params=_params(("parallel", "parallel", "arbitrary")),
    )(a, b)


def _shift_down(x, prev, s):
    rolled = pltpu.roll(x, s, 0)
    pr = pltpu.roll(prev, s, 0)
    head = jnp.where(_iota((8, 1), 0) < s, pr, rolled[:8])
    return jnp.concatenate([head, rolled[8:]], axis=0)


def _shift_up(x, nxt, s):
    t = x.shape[0]
    rolled = pltpu.roll(x, t - s, 0)
    nr = pltpu.roll(nxt, 8 - s, 0)
    tail = jnp.where(_iota((8, 1), 0) >= 8 - s, nr, rolled[t - 8:])
    return jnp.concatenate([rolled[:t - 8], tail], axis=0)


def _conv_fwd(p_c, w8, t):
    s = p_c.shape[0]
    t = min(t, s)
    r = t // 8

    def body(x_ref, prev_ref, w_ref, o_ref):
        i = pl.program_id(0)
        x = x_ref[...]
        prev = jnp.where(i == 0, 0.0, prev_ref[...])
        acc = w_ref[3:4, :] * x
        for sh in range(1, 4):
            acc = acc + w_ref[3 - sh:4 - sh, :] * _shift_down(x, prev, sh)
        o_ref[...] = acc

    return pl.pallas_call(
        body, name="conv_fwd", grid=(s // t,),
        in_specs=[pl.BlockSpec((t, C_QKV), lambda i: (i, 0)),
                  pl.BlockSpec((8, C_QKV), lambda i: (jnp.maximum(i * r - 1, 0), 0)),
                  pl.BlockSpec((8, C_QKV), lambda i: (0, 0))],
        out_specs=pl.BlockSpec((t, C_QKV), lambda i: (i, 0)),
        out_shape=jax.ShapeDtypeStruct((s, C_QKV), F32),
        compiler_params=_params(("parallel",)),
    )(p_c, p_c, w8)


def _conv_bwd(p_c, dy, w8, t):
    s = p_c.shape[0]
    t = min(t, s)
    r = t // 8
    n = s // t

    def body(x_ref, prev_ref, dy_ref, next_ref, w_ref, dx_ref, dw_ref):
        i = pl.program_id(0)
        x, g = x_ref[...], dy_ref[...]
        prev = jnp.where(i == 0, 0.0, prev_ref[...])
        nxt = jnp.where(i == n - 1, 0.0, next_ref[...])

        @pl.when(i == 0)
        def _():
            dw_ref[...] = jnp.zeros(dw_ref.shape, F32)

        dx = w_ref[3:4, :] * g
        dw_ref[3:4, :] += jnp.sum(g * x, axis=0, keepdims=True)
        for sh in range(1, 4):
            dx = dx + w_ref[3 - sh:4 - sh, :] * _shift_up(g, nxt, sh)
            dw_ref[3 - sh:4 - sh, :] += jnp.sum(g * _shift_down(x, prev, sh), axis=0, keepdims=True)
        dx_ref[...] = dx

    return pl.pallas_call(
        body, name="conv_bwd", grid=(n,),
        in_specs=[pl.BlockSpec((t, C_QKV), lambda i: (i, 0)),
                  pl.BlockSpec((8, C_QKV), lambda i: (jnp.maximum(i * r - 1, 0), 0)),
                  pl.BlockSpec((t, C_QKV), lambda i: (i, 0)),
                  pl.BlockSpec((8, C_QKV), lambda i: (jnp.minimum((i + 1) * r, s // 8 - 1), 0)),
                  pl.BlockSpec((8, C_QKV), lambda i: (0, 0))],
        out_specs=[pl.BlockSpec((t, C_QKV), lambda i: (i, 0)), pl.BlockSpec((8, C_QKV), lambda i: (0, 0))],
        out_shape=[jax.ShapeDtypeStruct((s, C_QKV), F32), jax.ShapeDtypeStruct((8, C_QKV), F32)],
        compiler_params=_params(("arbitrary",)),
    )(p_c, p_c, dy, dy, w8)


def _delta_chunks(state, q, k, v, g, beta):
    pre = [_delta_pre(q[:, i * C_CHUNK:(i + 1) * C_CHUNK], k[:, i * C_CHUNK:(i + 1) * C_CHUNK],
                      v[:, i * C_CHUNK:(i + 1) * C_CHUNK], g[i], beta[i]) for i in range(DELTA_SUB)]
    outs = []
    for qg, intra, u, w, k_dec, dec in pre:
        v_new = u - _mm("nn", True, w, state)
        outs.append(_mm("nn", True, qg, state) + _mm("nn", True, intra, v_new))
        state = state * dec + _mm("tn", True, k_dec, v_new)
    return state, jnp.concatenate(outs, axis=1)


def _delta_pre(q, k, v, g, beta):
    c = C_CHUNK
    row, col = _iota((c, c), 0), _iota((c, c), 1)
    tril, strict = col <= row, col < row
    gc = _mmh("nn", g, (row <= col).astype(F32))
    g_last = jnp.sum(g, axis=1, keepdims=True)
    qs = q * (C_DK ** -0.5)
    decay = jnp.exp(jnp.where(tril, gc[:, :, None] - gc[:, None, :], NEG))
    k_beta, v_beta = k * beta[:, :, None], v * beta[:, :, None]
    x = -jnp.where(strict, _mm("nt", True, k_beta, k) * decay, 0.0)
    powers = [x]
    for _ in range(5):
        powers.append(_mm("nn", True, powers[-1], powers[-1]))
    sol = jnp.concatenate([v_beta, k_beta * jnp.exp(gc)[:, :, None]], axis=2)
    for p in reversed(powers):
        sol = sol + _mm("nn", True, p, sol)
    u, w = sol[:, :, :C_DK], sol[:, :, C_DK:]
    intra = jnp.where(tril, _mm("nt", True, qs, k) * decay, 0.0)
    return (qs * jnp.exp(gc)[:, :, None], intra, u, w, k * jnp.exp(g_last - gc)[:, :, None],
            jnp.exp(g_last)[:, :, None])


def _delta_specs(n, rev):
    def at(i):
        return n - 1 - i if rev else i
    tok = pl.BlockSpec((N_HEADS, DELTA_SUB * C_CHUNK, C_DK), lambda i: (0, at(i), 0))
    vec = pl.BlockSpec((DELTA_SUB, N_HEADS, C_CHUNK), lambda i: (at(i), 0, 0))
    st = pl.BlockSpec((1, N_HEADS, C_DK, C_DK), lambda i: (at(i), 0, 0, 0))
    return tok, vec, st


def _delta_fwd(q, k, v, g, beta):
    s = q.shape[1]
    n = s // (DELTA_SUB * C_CHUNK)
    tok, vec, st = _delta_specs(n, False)

    def body(q_ref, k_ref, v_ref, g_ref, b_ref, o_ref, st_ref, state):
        @pl.when(pl.program_id(0) == 0)
        def _():
            state[...] = jnp.zeros(state.shape, F32)

        cur = state[...]
        st_ref[0] = cur
        new, o = _delta_chunks(cur, q_ref[...], k_ref[...], v_ref[...], g_ref[...], b_ref[...])
        o_ref[...] = o
        state[...] = new

    return pl.pallas_call(
        body, name="delta_fwd", grid=(n,), in_specs=[tok, tok, tok, vec, vec], out_specs=[tok, st],
        out_shape=[jax.ShapeDtypeStruct((N_HEADS, s, C_DK), F32), jax.ShapeDtypeStruct((n, N_HEADS, C_DK, C_DK), F32)],
        scratch_shapes=[pltpu.VMEM((N_HEADS, C_DK, C_DK), F32)],
        compiler_params=_params(("arbitrary",)),
    )(q, k, v, g, beta)


def _delta_bwd(q, k, v, g, beta, states, do):
    s = q.shape[1]
    n = s // (DELTA_SUB * C_CHUNK)
    tok, vec, st = _delta_specs(n, True)

    def body(q_ref, k_ref, v_ref, g_ref, b_ref, st_ref, do_ref, dq_ref, dk_ref, dv_ref, dg_ref, db_ref, dstate):
        @pl.when(pl.program_id(0) == 0)
        def _():
            dstate[...] = jnp.zeros(dstate.shape, F32)

        _, vjp = jax.vjp(_delta_chunks, st_ref[0], q_ref[...], k_ref[...], v_ref[...], g_ref[...], b_ref[...])
        dst, dq, dk, dv, dg, db = vjp((dstate[...], do_ref[...]))
        dq_ref[...], dk_ref[...], dv_ref[...] = dq, dk, dv
        dg_ref[...], db_ref[...] = dg, db
        dstate[...] = dst

    tok_shape = jax.ShapeDtypeStruct((N_HEADS, s, C_DK), F32)
    vec_shape = jax.ShapeDtypeStruct((n * DELTA_SUB, N_HEADS, C_CHUNK), F32)
    return pl.pallas_call(
        body, name="delta_bwd", grid=(n,), in_specs=[tok, tok, tok, vec, vec, st, tok],
        out_specs=[tok, tok, tok, vec, vec], out_shape=[tok_shape, tok_shape, tok_shape, vec_shape, vec_shape],
        scratch_shapes=[pltpu.VMEM((N_HEADS, C_DK, C_DK), F32)],
        compiler_params=_params(("arbitrary",)),
    )(q, k, v, g, beta, states, do)


ATT_SCALE = B_QK ** -0.5


SCORE_SCALE_LOG2 = ATT_SCALE * math.log2(math.e)


def _lanes(x, n):
    return x if n == 1 else jnp.concatenate([x] * n, axis=1)


def _scores(a_ref, b_ref):
    return lax.dot_general(a_ref[...], b_ref[...], (((1,), (1,)), ((), ())), preferred_element_type=F32) * SCORE_SCALE_LOG2


def _flash_fwd(q, k, v, tb):
    s = q.shape[0]
    tb = min(tb, s)
    nb = s // tb
    nrep = tb // 128

    def body(q_ref, k_ref, v_ref, o_ref, lse_ref, m_s, l_s, acc):
        i, j = pl.program_id(1), pl.program_id(2)

        @pl.when(j == 0)
        def _():
            m_s[...] = jnp.full(m_s.shape, -jnp.inf, F32)
            l_s[...] = jnp.zeros(l_s.shape, F32)
            acc[...] = jnp.zeros(acc.shape, F32)

        def step(masked):
            sc = _scores(q_ref, k_ref)
            if masked:
                sc = jnp.where(_iota((tb, tb), 1) <= _iota((tb, tb), 0), sc, -jnp.inf)
            m_prev = m_s[...]
            m_new = jnp.maximum(m_prev, jnp.max(sc, axis=1, keepdims=True))
            alpha = jnp.exp2(m_prev - m_new)
            p = jnp.exp2(sc - _lanes(m_new, nrep))
            l_s[...] = alpha * l_s[...] + jnp.sum(p, axis=1, keepdims=True)
            acc[...] = alpha * acc[...] + jnp.dot(p.astype(BF16), v_ref[...], preferred_element_type=F32)
            m_s[...] = m_new

        @pl.when(j < i)
        def _():
            step(False)

        @pl.when(j == i)
        def _():
            step(True)
            o_ref[...] = acc[...] / l_s[...]
            lse_ref[...] = m_s[...] + jnp.log2(l_s[...])

    qs = pl.BlockSpec((tb, HEAD_PAD), lambda h, i, j: (i, h))
    ks = pl.BlockSpec((tb, HEAD_PAD), lambda h, i, j: (jnp.minimum(j, i), h))
    shape = jax.ShapeDtypeStruct((s, N_HEADS * HEAD_PAD), F32)
    return pl.pallas_call(
        body, name="flash_fwd", grid=(N_HEADS, nb, nb), in_specs=[qs, ks, ks],
        out_specs=[qs, qs], out_shape=[shape, shape],
        scratch_shapes=[pltpu.VMEM((tb, 128), F32), pltpu.VMEM((tb, 128), F32), pltpu.VMEM((tb, HEAD_PAD), F32)],
        compiler_params=_params(("parallel", "parallel", "arbitrary")),
    )(q, k, v)


def _f_attn_delta(o, do):
    return (_head_sums(o * do, HEAD_PAD),)


def _row_stats(rep):
    return rep[:, ::HEAD_PAD].T.reshape(N_HEADS, 1, -1)


def _flash_bwd_dq(q, k, v, do, lse_rep, delta_rep, tb):
    s = q.shape[0]
    tb = min(tb, s)
    nb = s // tb
    nrep = tb // 128

    def body(q_ref, k_ref, v_ref, do_ref, lse_ref, dl_ref, dq_ref, acc):
        i, j = pl.program_id(1), pl.program_id(2)

        @pl.when(j == 0)
        def _():
            acc[...] = jnp.zeros(acc.shape, F32)

        def step(masked):
            sc = _scores(q_ref, k_ref)
            if masked:
                sc = jnp.where(_iota((tb, tb), 1) <= _iota((tb, tb), 0), sc, -jnp.inf)
            p = jnp.exp2(sc - _lanes(lse_ref[...], nrep))
            dp = lax.dot_general(do_ref[...].astype(BF16), v_ref[...], (((1,), (1,)), ((), ())), preferred_element_type=F32)
            ds = p * (dp - _lanes(dl_ref[...], nrep))
            acc[...] += jnp.dot(ds.astype(BF16), k_ref[...], preferred_element_type=F32)

        @pl.when(j < i)
        def _():
            step(False)

        @pl.when(j == i)
        def _():
            step(True)
            dq_ref[...] = acc[...] * ATT_SCALE

    qs = pl.BlockSpec((tb, HEAD_PAD), lambda h, i, j: (i, h))
    ks = pl.BlockSpec((tb, HEAD_PAD), lambda h, i, j: (jnp.minimum(j, i), h))
    return pl.pallas_call(
        body, name="flash_bwd_dq", grid=(N_HEADS, nb, nb), in_specs=[qs, ks, ks, qs, qs, qs],
        out_specs=qs, out_shape=jax.ShapeDtypeStruct((s, N_HEADS * HEAD_PAD), F32),
        scratch_shapes=[pltpu.VMEM((tb, HEAD_PAD), F32)],
        compiler_params=_params(("parallel", "parallel", "arbitrary")),
    )(q, k, v, do, lse_rep, delta_rep)


def _flash_bwd_dkv(q, k, v, do, lse_row, delta_row, tb):
    s = q.shape[0]
    tb = min(tb, s)
    nb = s // tb

    def body(q_ref, k_ref, v_ref, do_ref, lse_ref, dl_ref, dk_ref, dv_ref, dk_acc, dv_acc):
        j, i = pl.program_id(1), pl.program_id(2)

        @pl.when(i == 0)
        def _():
            dk_acc[...] = jnp.zeros(dk_acc.shape, F32)
            dv_acc[...] = jnp.zeros(dv_acc.shape, F32)

        def step(masked):
            st = _scores(k_ref, q_ref)
            if masked:
                st = jnp.where(_iota((tb, tb), 0) <= _iota((tb, tb), 1), st, -jnp.inf)
            do = do_ref[...].astype(BF16)
            pt = jnp.exp2(st - lse_ref[0])
            dpt = lax.dot_general(v_ref[...], do, (((1,), (1,)), ((), ())), preferred_element_type=F32)
            dst = pt * (dpt - dl_ref[0])
            dv_acc[...] += jnp.dot(pt.astype(BF16), do, preferred_element_type=F32)
            dk_acc[...] += jnp.dot(dst.astype(BF16), q_ref[...], preferred_element_type=F32)

        @pl.when(i == j)
        def _():
            step(True)

        @pl.when(i > j)
        def _():
            step(False)

        @pl.when(i == nb - 1)
        def _():
            dk_ref[...] = dk_acc[...] * ATT_SCALE
            dv_ref[...] = dv_acc[...]

    qs = pl.BlockSpec((tb, HEAD_PAD), lambda h, j, i: (jnp.maximum(i, j), h))
    ks = pl.BlockSpec((tb, HEAD_PAD), lambda h, j, i: (j, h))
    rs = pl.BlockSpec((1, 1, tb), lambda h, j, i: (h, 0, jnp.maximum(i, j)))
    shape = jax.ShapeDtypeStruct((s, N_HEADS * HEAD_PAD), F32)
    return pl.pallas_call(
        body, name="flash_bwd_dkv", grid=(N_HEADS, nb, nb), in_specs=[qs, ks, ks, qs, rs, rs],
        out_specs=[ks, ks], out_shape=[shape, shape],
        scratch_shapes=[pltpu.VMEM((tb, HEAD_PAD), F32), pltpu.VMEM((tb, HEAD_PAD), F32)],
        compiler_params=_params(("parallel", "parallel", "arbitrary")),
    )(q, k, v, do, lse_row, delta_row)


def _loss_head(y, target, t):
    s, d = y.shape
    t = min(t, s)

    def body(y_ref, t_ref, sum_ref, dy_ref):
        @pl.when(pl.program_id(0) == 0)
        def _():
            sum_ref[...] = jnp.zeros(sum_ref.shape, F32)

        err = y_ref[...] - t_ref[...]
        dy_ref[...] = err * (1.0 / d)
        sum_ref[...] += jnp.broadcast_to(jnp.sum(err * err), sum_ref.shape)

    return pl.pallas_call(
        body, name="loss_head", grid=(s // t,),
        in_specs=[pl.BlockSpec((t, d), lambda i: (i, 0))] * 2,
        out_specs=[pl.BlockSpec((1, 128), lambda i: (0, 0)), pl.BlockSpec((t, d), lambda i: (i, 0))],
        out_shape=[jax.ShapeDtypeStruct((1, 128), F32), jax.ShapeDtypeStruct((s, d), F32)],
        compiler_params=_params(("arbitrary",)),
    )(y, target)


def _pad_to(a, axis, size):
    pad = [(0, 0)] * a.ndim
    pad[axis] = (0, size - a.shape[axis])
    return jnp.pad(a, pad)


W_IN_GROUPS = {
    "a": [(0, O_QLAT)],
    "b": [(O_KVLAT, O_KROPE), B_NOPE, (O_KROPE, O_CQKV), HEAD_PAD - B_QK, (O_QLAT, O_KVLAT)],
    "c": [(O_CQKV, O_GATES), W_C - (O_GATES - O_CQKV)],
    "g": [(O_GATES, D_IN)],
}
W_IN_SHARD = D_IN // N_DEV
W_IN_ROWS = 128


def _group_width(key):
    return sum(e if isinstance(e, int) else e[1] - e[0] for e in W_IN_GROUPS[key])


def _assemble_w_in(shards):
    depth = shards.shape[1]
    keys = list(W_IN_GROUPS)

    def body(s_ref, *o_refs):
        vals = [s_ref[k, 0] for k in range(N_DEV)]
        for key, o_ref in zip(keys, o_refs):
            parts = []
            for e in W_IN_GROUPS[key]:
                if isinstance(e, int):
                    parts.append(jnp.zeros((W_IN_ROWS, e), shards.dtype))
                    continue
                lo, hi = e
                while lo < hi:
                    k = lo // W_IN_SHARD
                    end = min(hi, (k + 1) * W_IN_SHARD)
                    parts.append(vals[k][:, lo - k * W_IN_SHARD:end - k * W_IN_SHARD])
                    lo = end
            o_ref[0] = parts[0] if len(parts) == 1 else jnp.concatenate(parts, axis=1)

    return pl.pallas_call(
        body, name="assemble_w_in", grid=(depth, D_MODEL // W_IN_ROWS),
        in_specs=[pl.BlockSpec((N_DEV, 1, W_IN_ROWS, W_IN_SHARD), lambda l, i: (0, l, i, 0))],
        out_specs=[pl.BlockSpec((1, W_IN_ROWS, _group_width(key)), lambda l, i: (l, i, 0)) for key in keys],
        out_shape=[jax.ShapeDtypeStruct((depth, D_MODEL, _group_width(key)), shards.dtype) for key in keys],
        compiler_params=_params(("parallel", "parallel")),
    )(shards)


def _split_dw_in(groups):
    keys = list(W_IN_GROUPS)
    depth = groups[0].shape[0]
    runs = []
    for gi, key in enumerate(keys):
        col = 0
        for e in W_IN_GROUPS[key]:
            if not isinstance(e, int):
                runs.append((e[0], e[1], gi, col))
            col += e if isinstance(e, int) else e[1] - e[0]
    runs.sort()

    def body(*refs):
        vals = [r[0] for r in refs[:len(keys)]]
        o_ref = refs[len(keys)]
        for k in range(N_DEV):
            lo, hi = k * W_IN_SHARD, (k + 1) * W_IN_SHARD
            parts = []
            for a, b, gi, col in runs:
                s, e = max(a, lo), min(b, hi)
                if s < e:
                    parts.append(vals[gi][:, col + s - a:col + e - a])
            o_ref[k, 0] = jnp.concatenate(parts, axis=1)

    return pl.pallas_call(
        body, name="split_dw_in", grid=(depth, D_MODEL // W_IN_ROWS),
        in_specs=[pl.BlockSpec((1, W_IN_ROWS, _group_width(key)), lambda l, i: (l, i, 0)) for key in keys],
        out_specs=pl.BlockSpec((N_DEV, 1, W_IN_ROWS, W_IN_SHARD), lambda l, i: (0, l, i, 0)),
        out_shape=jax.ShapeDtypeStruct((N_DEV, depth, D_MODEL, W_IN_SHARD), F32),
        compiler_params=_params(("parallel", "parallel")),
    )(*groups)


def _prep_layer(w, l):
    p = {}
    for key in W_IN_GROUPS:
        val = w["w_in_" + key][l].astype(BF16)
        p["w_" + key] = val
        p["wt_" + key] = val.T
    for name in ("norm1_g", "sgu_norm_g", "q_lat_norm_g", "kv_lat_norm_g", "o_norm_g", "norm2_g"):
        p[name] = w[name][l][None, :]
    p["w_spatial"], p["b_spatial"] = w["w_spatial"][l], w["b_spatial"][l]
    p["wq"] = _pad_to(w["w_q_up"][l].astype(F32).reshape(B_Q_LORA, N_HEADS, B_QK), 2, HEAD_PAD).reshape(B_Q_LORA, -1)
    kv = w["w_kv_up"][l].astype(F32).reshape(B_KV_LORA, N_HEADS, B_NOPE + B_VDIM)
    p["wk"] = _pad_to(kv[:, :, :B_NOPE], 2, HEAD_PAD).reshape(B_KV_LORA, -1)
    p["wv"] = _pad_to(kv[:, :, B_NOPE:], 2, HEAD_PAD).reshape(B_KV_LORA, -1)
    p["qn_g"] = _pad_to(w["q_norm_g"][l][None, :], 1, HEAD_PAD)
    p["kn_g"] = _pad_to(w["k_norm_g"][l][None, :], 1, HEAD_PAD)
    p["conv_w"] = _pad_to(w["conv_w"][l], 0, 8)
    row = lambda v: jnp.pad(v[None, :], ((0, 0), (N_HEADS, HEAD_PAD - 2 * N_HEADS)))
    p["a_log"], p["dt_bias"] = row(w["a_log"][l]), row(w["dt_bias"][l])
    wb = w["w_branch"][l]
    wb1 = _pad_to(wb[1].reshape(N_HEADS, B_VDIM, D_MODEL), 1, HEAD_PAD).reshape(-1, D_MODEL)
    for key, val in (("wb0", wb[0]), ("wb1", wb1), ("wb2", wb[2]), ("w_out", w["w_out"][l]),
                     ("w_ff1", w["w_ff1"][l]), ("w_ff2", w["w_ff2"][l])):
        p[key] = val.astype(BF16)
        p[key + "_t"] = val.T.astype(BF16)
    return p


def _unprep_grads(g):
    out = {"w_in_" + key: g["w_" + key] for key in W_IN_GROUPS}
    for name in ("norm1_g", "sgu_norm_g", "q_lat_norm_g", "kv_lat_norm_g", "o_norm_g", "norm2_g"):
        out[name] = g[name][0]
    out["w_spatial"], out["b_spatial"] = g["w_spatial"], g["b_spatial"]
    out["w_q_up"] = g["wq"].reshape(B_Q_LORA, N_HEADS, HEAD_PAD)[:, :, :B_QK].reshape(B_Q_LORA, -1)
    gk = g["wk"].reshape(B_KV_LORA, N_HEADS, HEAD_PAD)[:, :, :B_NOPE]
    gv = g["wv"].reshape(B_KV_LORA, N_HEADS, HEAD_PAD)[:, :, :B_VDIM]
    out["w_kv_up"] = jnp.concatenate([gk, gv], axis=2).reshape(B_KV_LORA, -1)
    out["q_norm_g"], out["k_norm_g"] = g["qn_g"][0, :B_QK], g["kn_g"][0, :B_QK]
    out["conv_w"] = g["conv_w"][:4]
    out["a_log"], out["dt_bias"] = g["a_log"][0, N_HEADS:2 * N_HEADS], g["dt_bias"][0, N_HEADS:2 * N_HEADS]
    gb1 = g["wb1"].reshape(N_HEADS, HEAD_PAD, D_MODEL)[:, :B_VDIM].reshape(-1, D_MODEL)
    out["w_branch"] = jnp.stack([g["wb0"], gb1, g["wb2"]], axis=0)
    out["w_out"], out["w_ff1"], out["w_ff2"] = g["w_out"], g["w_ff1"], g["w_ff2"]
    return out


def _heads_first(a):
    s = a.shape[0]
    return a.reshape(s, N_HEADS, C_DK).transpose(1, 0, 2)


def _heads_last(a):
    return a.transpose(1, 0, 2).reshape(a.shape[1], N_HEADS * C_DK)


def _chunk_vec(a):
    return a.reshape(-1, C_CHUNK, N_HEADS).transpose(0, 2, 1)


def _unchunk_vec(a):
    return a.transpose(0, 2, 1).reshape(-1, N_HEADS)


def _layer_fwd(x, p, tabs, t, tb):
    sv = {"x": x}
    h1, = _local_fwd("norm1", _f_norm, [(x, D_MODEL, 0)], [p["norm1_g"]], [(D_MODEL, BF16)], t)
    sv["h1"] = h1
    p_a, p_b, p_c, p_g = (_matmul("proj_" + key, h1, p["w_" + key]) for key in "abcg")
    sv.update(p_a=p_a, p_b=p_b, p_c=p_c, p_g=p_g)
    y_a, = _local_fwd("sgu", _f_sgu, [(p_a, 2 * A_WIDTH, 0)], [p["sgu_norm_g"], p["w_spatial"], p["b_spatial"]],
                      [(A_WIDTH, BF16)], SGU_CHUNK)
    q, k, v = _local_fwd("mla_prep", _f_mla_prep, [(p_b, W_B, 0)] + [(tb_, HEAD_PAD, 0) for tb_ in tabs],
                         [p["q_lat_norm_g"], p["kv_lat_norm_g"], p["wq"], p["wk"], p["wv"], p["qn_g"], p["kn_g"]],
                         [(N_HEADS * HEAD_PAD, BF16)] * 3, min(t, 256))
    o_b, lse = _flash_fwd(q, k, v, tb)
    sv.update(q=q, k=k, v=v, o_b=o_b, lse=lse)
    conv_pre = _conv_fwd(p_c, p["conv_w"], t)
    cq, ck, cv, gb = _local_fwd("gdn_pre", _f_gdn_pre, [(conv_pre, C_QKV, 0), (p_c, HEAD_PAD, (C_QKV + C_Z) // HEAD_PAD)],
                                [p["a_log"], p["dt_bias"]], [(A_WIDTH, F32)] * 3 + [(HEAD_PAD, F32)], t)
    cq, ck, cv = _heads_first(cq), _heads_first(ck), _heads_first(cv)
    beta, g = _chunk_vec(gb[:, :N_HEADS]), _chunk_vec(gb[:, N_HEADS:2 * N_HEADS])
    o_c, states = _delta_fwd(cq, ck, cv, g, beta)
    o_c = _heads_last(o_c)
    sv.update(conv_pre=conv_pre, cq=cq, ck=ck, cv=cv, beta=beta, g=g, states=states, o_c=o_c)
    y_c, = _local_fwd("gdn_post", _f_gdn_post, [(o_c, A_WIDTH, 0), (p_c, C_Z, C_QKV // C_Z)], [p["o_norm_g"]],
                      [(A_WIDTH, BF16)], t)
    y0 = _matmul("branch0", y_a, p["wb0"])
    y1 = _matmul("branch1", o_b, p["wb1"])
    y2 = _matmul("branch2", y_c, p["wb2"])
    merged, = _local_fwd("merge", _f_merge, [(p_g, 3 * D_MODEL, 0), (y0, D_MODEL, 0), (y1, D_MODEL, 0), (y2, D_MODEL, 0)],
                         [], [(D_MODEL, BF16)], t)
    x1 = _matmul("out_proj", merged, p["w_out"], add=x)
    sv.update(y_a=y_a, y_c=y_c, y0=y0, y1=y1, y2=y2, merged=merged, x1=x1)
    h2, = _local_fwd("norm2", _f_norm, [(x1, D_MODEL, 0)], [p["norm2_g"]], [(D_MODEL, BF16)], t)
    a, r = _matmul("ff1", h2, p["w_ff1"], epilogue=lambda acc: (acc, jnp.square(jnp.maximum(acc, 0.0))),
                   out_dtypes=(BF16, BF16))
    x2 = _matmul("ff2", r, p["w_ff2"], add=x1)
    sv.update(h2=h2, a=a, r=r)
    return x2, sv


def _layer_bwd(dx2, sv, p, tabs, t, tb):
    g = {}
    da = _matmul("d_ff2", dx2, p["w_ff2_t"], extras=(sv["a"],), out_dtypes=(BF16,),
                 epilogue=lambda dr, a: (dr * (2.0 * jnp.maximum(a.astype(F32), 0.0)),))
    g["w_ff2"] = _matmul_tn("dw_ff2", sv["r"], dx2)
    dh2 = _matmul("d_ff1", da, p["w_ff1_t"])
    g["w_ff1"] = _matmul_tn("dw_ff1", sv["h2"], da)
    (dx1,), (g["norm2_g"],) = _local_bwd("norm2_bwd", _f_norm_res, [(sv["x1"], D_MODEL, 0)], [p["norm2_g"]],
                                         [dh2, dx2], t, [True], [True])
    dmerged = _matmul("d_out_proj", dx1, p["w_out_t"])
    g["w_out"] = _matmul_tn("dw_out", sv["merged"], dx1)
    (dp_g, dy0, dy1, dy2), _ = _local_bwd(
        "merge_bwd", _f_merge, [(sv["p_g"], 3 * D_MODEL, 0), (sv["y0"], D_MODEL, 0), (sv["y1"], D_MODEL, 0),
                                (sv["y2"], D_MODEL, 0)], [], [dmerged], min(t, 256), [True] * 4, [])
    dy_a = _matmul("d_branch0", dy0, p["wb0_t"])
    do_b = _matmul("d_branch1", dy1, p["wb1_t"])
    dy_c = _matmul("d_branch2", dy2, p["wb2_t"])
    g["wb0"] = _matmul_tn("dw_branch0", sv["y_a"], dy0)
    g["wb1"] = _matmul_tn("dw_branch1", sv["o_b"], dy1)
    g["wb2"] = _matmul_tn("dw_branch2", sv["y_c"], dy2)
    p_c = sv["p_c"]
    (do_c, dc_z), (g["o_norm_g"],) = _local_bwd(
        "gdn_post_bwd", _f_gdn_post, [(sv["o_c"], A_WIDTH, 0), (p_c, C_Z, C_QKV // C_Z)], [p["o_norm_g"]], [dy_c], t,
        [True, True], [True])
    dcq, dck, dcv, dg, dbeta = _delta_bwd(sv["cq"], sv["ck"], sv["cv"], sv["g"], sv["beta"], sv["states"],
                                          _heads_first(do_c))
    dgb = jnp.pad(jnp.concatenate([_unchunk_vec(dbeta), _unchunk_vec(dg)], axis=1),
                  ((0, 0), (0, HEAD_PAD - 2 * N_HEADS)))
    (dconv, dba), (g["a_log"], g["dt_bias"]) = _local_bwd(
        "gdn_pre_bwd", _f_gdn_pre, [(sv["conv_pre"], C_QKV, 0), (p_c, HEAD_PAD, (C_QKV + C_Z) // HEAD_PAD)],
        [p["a_log"], p["dt_bias"]], [_heads_last(dcq), _heads_last(dck), _heads_last(dcv), dgb], t,
        [True, True], [True, True])
    dc_qkv, g["conv_w"] = _conv_bwd(p_c, dconv, p["conv_w"], t)
    dp_c = jnp.concatenate([dc_qkv, dc_z, dba], axis=1)
    delta, = _local_fwd("attn_delta", _f_attn_delta, [(sv["o_b"], N_HEADS * HEAD_PAD, 0), (do_b, N_HEADS * HEAD_PAD, 0)], [],
                        [(N_HEADS * HEAD_PAD, F32)], t)
    dq = _flash_bwd_dq(sv["q"], sv["k"], sv["v"], do_b, sv["lse"], delta, tb)
    dk, dv = _flash_bwd_dkv(sv["q"], sv["k"], sv["v"], do_b, _row_stats(sv["lse"]), _row_stats(delta), tb)
    mla_pars = [p["q_lat_norm_g"], p["kv_lat_norm_g"], p["wq"], p["wk"], p["wv"], p["qn_g"], p["kn_g"]]
    (dp_b,), mla_g = _local_bwd(
        "mla_prep_bwd", _f_mla_prep, [(sv["p_b"], W_B, 0)] + [(tb_, HEAD_PAD, 0) for tb_ in tabs], mla_pars,
        [dq, dk, dv], min(t, 256), [True, False, False, False], [True] * 7)
    for name, val in zip(("q_lat_norm_g", "kv_lat_norm_g", "wq", "wk", "wv", "qn_g", "kn_g"), mla_g):
        g[name] = val
    (dp_a,), (g["sgu_norm_g"], g["w_spatial"], g["b_spatial"]) = _local_bwd(
        "sgu_bwd", _f_sgu, [(sv["p_a"], 2 * A_WIDTH, 0)], [p["sgu_norm_g"], p["w_spatial"], p["b_spatial"]], [dy_a],
        SGU_CHUNK, [True], [True] * 3)
    dh1 = None
    for key, dp in (("a", dp_a), ("b", dp_b), ("c", dp_c), ("g", dp_g)):
        dh1 = _matmul("d_proj_" + key, dp, p["wt_" + key], add=dh1)
        g["w_" + key] = _matmul_tn("dw_proj_" + key, sv["h1"], dp)
    (dx,), (g["norm1_g"],) = _local_bwd("norm1_bwd", _f_norm_res, [(sv["x"], D_MODEL, 0)], [p["norm1_g"]],
                                        [dh1, dx1], t, [True], [True])
    return dx, _unprep_grads(g)


def _local_step(x, positions, w, target, t=512, tb=1024):
    s = x.shape[0]
    t = min(t, s)
    w = dict(w)
    for key, val in zip(W_IN_GROUPS, _assemble_w_in(w["w_in"])):
        w["w_in_" + key] = val
    half = B_ROPE // 2
    inv_freq = 1.0 / (ROPE_BASE ** (jnp.arange(half, dtype=F32) / half))
    inv_row = jnp.concatenate([jnp.zeros((B_NOPE,), F32), inv_freq, inv_freq, jnp.zeros((HEAD_PAD - B_QK,), F32)])[None, :]
    tabs = _local_fwd("rope_tables", _f_rope_tables, [(positions, 1, 0)], [inv_row], [(HEAD_PAD, F32)] * 3, t)
    preps, saved = [], []
    for l in range(DEPTH):
        preps.append(_prep_layer(w, l))
        x, sv = _layer_fwd(x, preps[l], tabs, t, tb)
        saved.append(sv)
    sq, dx = _loss_head(x, target, t)
    grads = [None] * DEPTH
    for l in reversed(range(DEPTH)):
        dx, grads[l] = _layer_bwd(dx, saved[l], preps[l], tabs, t, tb)
    stacked = lambda name: jnp.stack([grads[l][name] for l in range(DEPTH)], axis=0)
    out = {name: stacked(name) for name in WEIGHTS if name != "w_in"}
    out["w_in"] = _split_dw_in([stacked("w_in_" + key) for key in W_IN_GROUPS])
    return sq, dx, out


def _exchange(name, arrays, gather):
    n = len(arrays)

    def body(*refs):
        send, recv = refs[:n], refs[n:2 * n]
        send_sems, recv_sems, local_sems = refs[2 * n:]
        x, y, c = lax.axis_index("x"), lax.axis_index("y"), lax.axis_index("c")
        me = 4 * x + 2 * y + c

        def src(a, idx):
            return send[a] if gather[a] else send[a].at[idx]

        local = [pltpu.make_async_copy(src(a, me), recv[a].at[me], local_sems.at[a]) for a in range(n)]
        for cp in local:
            cp.start()
        copies = []
        for d in range(1, N_DEV):
            px, py, pc = x ^ ((d >> 2) & 1), y ^ ((d >> 1) & 1), c ^ (d & 1)
            peer = 4 * px + 2 * py + pc
            for a in range(n):
                cp = pltpu.make_async_remote_copy(
                    src_ref=src(a, peer), dst_ref=recv[a].at[me], send_sem=send_sems.at[a, d],
                    recv_sem=recv_sems.at[a, d], device_id=(px, py, pc), device_id_type=pl.DeviceIdType.MESH)
                cp.start()
                copies.append((cp, a, peer, d))
        for cp, a, peer, d in copies:
            cp.wait_send()
            pltpu.make_async_remote_copy(
                src_ref=src(a, peer), dst_ref=recv[a].at[peer], send_sem=send_sems.at[a, d],
                recv_sem=recv_sems.at[a, d], device_id=(x, y, c), device_id_type=pl.DeviceIdType.MESH).wait_recv()
        for cp in local:
            cp.wait()

    any_spec = pl.BlockSpec(memory_space=pl.ANY)
    return pl.pallas_call(
        body, name=name, in_specs=[any_spec] * n, out_specs=[any_spec] * n,
        out_shape=[jax.ShapeDtypeStruct(((N_DEV,) + a.shape) if gather[i] else a.shape, a.dtype)
                   for i, a in enumerate(arrays)],
        scratch_shapes=[pltpu.SemaphoreType.DMA((n, N_DEV)), pltpu.SemaphoreType.DMA((n, N_DEV)),
                        pltpu.SemaphoreType.DMA((n,))],
    )(*arrays)


REDUCE_BLOCK_ELEMS = 64 * 1024


def _reduce_adamw(name, recv, w, m, v):
    rows, cols = w.shape
    tr = rows
    while tr % 16 == 0 and tr * (-(-cols // 128) * 128) > REDUCE_BLOCK_ELEMS:
        tr //= 2
    c1, c2 = 1.0 - ADAM_B1 ** ADAM_STEP, 1.0 - ADAM_B2 ** ADAM_STEP

    def body(r_ref, w_ref, m_ref, v_ref, g_ref, d_ref, nm_ref, nv_ref):
        g = r_ref[0]
        for j in range(1, N_DEV):
            g = g + r_ref[j]
        m_new = ADAM_B1 * m_ref[...] + (1.0 - ADAM_B1) * g
        v_new = ADAM_B2 * v_ref[...] + (1.0 - ADAM_B2) * jnp.square(g)
        d_ref[...] = -ADAM_LR * ((m_new / c1) / (jnp.sqrt(v_new / c2) + ADAM_EPS) + ADAM_WD * w_ref[...])
        g_ref[...], nm_ref[...], nv_ref[...] = g, m_new, v_new

    flat = pl.BlockSpec((tr, cols), lambda i: (i, 0))
    return pl.pallas_call(
        body, name=name, grid=(rows // tr,),
        in_specs=[pl.BlockSpec((N_DEV, tr, cols), lambda i: (0, i, 0)), flat, flat, flat], out_specs=[flat] * 4,
        out_shape=[jax.ShapeDtypeStruct((rows, cols), F32)] * 4, compiler_params=_params(("parallel",)),
    )(recv, w, m, v)


PACK_ROWS = 512


def _pack(cols):
    flat = jnp.concatenate(cols, axis=-1)
    tile = PACK_ROWS * 128
    flat = _pad_to(flat, flat.ndim - 1, -(-flat.shape[-1] // tile) * tile)
    return flat.reshape(flat.shape[:-1] + (-1, 128))


def _unpack(packed, shapes, lead=()):
    flat = packed.reshape(lead + (-1,))
    out, off = [], 0
    for shp in shapes:
        n = math.prod(shp)
        out.append(flat[..., off:off + n].reshape(lead + tuple(shp)))
        off += n
    return out


def _to_shards(name, full):
    ax = SHARD_AXIS[name]
    shp = full.shape
    return jnp.moveaxis(full.reshape(shp[:ax] + (N_DEV, shp[ax] // N_DEV) + shp[ax + 1:]), ax, 0)


def _from_shards(name, shards):
    ax = SHARD_AXIS[name]
    a = jnp.moveaxis(shards, 0, ax)
    return a.reshape(a.shape[:ax] + (a.shape[ax] * a.shape[ax + 1],) + a.shape[ax + 2:])


def kernel(x, positions, norm1_g, w_in, sgu_norm_g, w_spatial, b_spatial, q_lat_norm_g, w_q_up, kv_lat_norm_g, w_kv_up, q_norm_g, k_norm_g, conv_w, a_log, dt_bias, o_norm_g, w_branch, w_out, norm2_g, w_ff1, w_ff2, loss_target, m_norm1_g, m_w_in, m_sgu_norm_g, m_w_spatial, m_b_spatial, m_q_lat_norm_g, m_w_q_up, m_kv_lat_norm_g, m_w_kv_up, m_q_norm_g, m_k_norm_g, m_conv_w, m_a_log, m_dt_bias, m_o_norm_g, m_w_branch, m_w_out, m_norm2_g, m_w_ff1, m_w_ff2, v_norm1_g, v_w_in, v_sgu_norm_g, v_w_spatial, v_b_spatial, v_q_lat_norm_g, v_w_q_up, v_kv_lat_norm_g, v_w_kv_up, v_q_norm_g, v_k_norm_g, v_conv_w, v_a_log, v_dt_bias, v_o_norm_g, v_w_branch, v_w_out, v_norm2_g, v_w_ff1, v_w_ff2):
    args = locals()
    local_w = {n: args[n] for n in WEIGHTS}
    state = (local_w, {n: args["m_" + n] for n in WEIGHTS}, {n: args["v_" + n] for n in WEIGHTS})
    wire = [local_w[n] if n in EXACT_GATHER else local_w[n].astype(BF16) for n in SHARDED]
    full = dict(local_w)
    for n, part in zip(SHARDED, _exchange("gather_weights", wire, [True] * len(wire))):
        full[n] = part if n == "w_in" else _from_shards(n, part)
    sq, grad_x, grads = _local_step(x[0], positions.reshape(-1, 1), full, loss_target[0])
    loss = lax.psum(sq[0, 0] * (0.5 / D_MODEL), ("x", "y", "c"))
    send = [grads[n] if n == "w_in" else _to_shards(n, grads[n]) for n in SHARDED]
    rep = _pack([grads[n].reshape(-1) for n in REPLICATED])
    recv = _exchange("exchange_grads", send + [rep], [False] * len(send) + [True])
    results = {}
    for n, r in zip(SHARDED, recv):
        shp = local_w[n].shape
        flat = (math.prod(shp[:-1]), shp[-1])
        outs = _reduce_adamw("adamw_" + n, r.reshape((N_DEV,) + flat), *[src[n].reshape(flat) for src in state])
        results[n] = [o.reshape(shp) for o in outs]
    outs = _reduce_adamw("adamw_replicated", recv[-1],
                         *[_pack([src[n].reshape(-1) for n in REPLICATED]) for src in state])
    rep_shapes = [local_w[n].shape for n in REPLICATED]
    for n, vals in zip(REPLICATED, zip(*[_unpack(o, rep_shapes) for o in outs])):
        results[n] = vals
    return (loss, grad_x[None], *[results[n][k] for k in range(4) for n in WEIGHTS])
```

```python
import functools
import math

import jax
import jax.numpy as jnp
from jax import lax
from jax.experimental import pallas as pl
from jax.experimental.pallas import tpu as pltpu

F32, BF16 = jnp.float32, jnp.bfloat16
HI = lax.Precision.HIGHEST

N_DEV = 8
D_MODEL = 1024
DEPTH = 2
N_HEADS = 8
HEAD_PAD = 128
A_WIDTH = 512
B_NOPE, B_ROPE, B_VDIM = 64, 32, 64
B_QK = B_NOPE + B_ROPE
B_Q_LORA, B_KV_LORA = 384, 256
ROPE_BASE = 10000.0
C_DK = 64
C_CHUNK = 64
DELTA_SUB = 2
DELTA_BLOCK = 16
C_QKV = 1536
C_Z = 512
SGU_CHUNK = 128
D_FF = 4096
EPS = 1e-6
ADAM_LR, ADAM_B1, ADAM_B2, ADAM_EPS, ADAM_WD, ADAM_STEP = 0.001, 0.9, 0.999, 1e-08, 0.01, 10
O_QLAT, O_KVLAT, O_KROPE, O_CQKV, O_GATES, D_IN = 1024, 1408, 1664, 1696, 3760, 6832
W_B, W_C = 768, 2176
VMEM_LIMIT = 56 * 2 ** 20
NEG = -1e30

SHARDED = ("w_in", "w_q_up", "w_kv_up", "conv_w", "w_branch", "w_out", "w_ff1", "w_ff2")
EXACT_GATHER = ("conv_w",)
SHARD_AXIS = {"w_in": 2, "w_q_up": 2, "w_kv_up": 2, "conv_w": 2, "w_branch": 3, "w_out": 1, "w_ff1": 2, "w_ff2": 1}
REPLICATED = ("norm1_g", "sgu_norm_g", "w_spatial", "b_spatial", "q_lat_norm_g", "kv_lat_norm_g", "q_norm_g",
              "k_norm_g", "a_log", "dt_bias", "o_norm_g", "norm2_g")
WEIGHTS = ("norm1_g", "w_in", "sgu_norm_g", "w_spatial", "b_spatial", "q_lat_norm_g", "w_q_up", "kv_lat_norm_g",
           "w_kv_up", "q_norm_g", "k_norm_g", "conv_w", "a_log", "dt_bias", "o_norm_g", "w_branch", "w_out",
           "norm2_g", "w_ff1", "w_ff2")


def _params(sem, vmem=VMEM_LIMIT):
    return pltpu.CompilerParams(dimension_semantics=sem, vmem_limit_bytes=vmem)


_FORMS = {"nn": (1, 0), "nt": (1, 1), "tn": (0, 0)}


def _dot(form, a, b, batch, prec=None):
    ca, cb = _FORMS[form]
    o = 1 if batch else 0
    bd = ((0,), (0,)) if batch else ((), ())
    return lax.dot_general(a, b, (((ca + o,), (cb + o,)), bd), precision=prec, preferred_element_type=F32)


def _mm_raw(form, batch, a, b):
    return _dot(form, a.astype(BF16), b.astype(BF16), batch)


@functools.partial(jax.custom_vjp, nondiff_argnums=(0, 1))
def _mm(form, batch, a, b):
    return _mm_raw(form, batch, a, b)


def _mm_fwd(form, batch, a, b):
    return _mm_raw(form, batch, a, b), (a, b)


def _mm_bwd(form, batch, res, g):
    a, b = res
    if form == "nn":
        da, db = _mm_raw("nt", batch, g, b), _mm_raw("tn", batch, a, g)
    elif form == "nt":
        da, db = _mm_raw("nn", batch, g, b), _mm_raw("tn", batch, g, a)
    else:
        da, db = _mm_raw("nt", batch, b, g), _mm_raw("nn", batch, a, g)
    return da.astype(a.dtype), db.astype(b.dtype)


_mm.defvjp(_mm_fwd, _mm_bwd)


def _mmh(form, a, b, batch=False):
    return _dot(form, a, b, batch, HI)


@functools.partial(jax.custom_vjp, nondiff_argnums=(1, 2))
def _roll(x, shift, axis):
    return pltpu.roll(x, shift % x.shape[axis], axis)


def _roll_fwd(x, shift, axis):
    return _roll(x, shift, axis), None


def _roll_bwd(shift, axis, _, g):
    return (_roll(g, -shift, axis),)


_roll.defvjp(_roll_fwd, _roll_bwd)


@jax.custom_vjp
def _tile_heads(x):
    return jnp.concatenate([x] * N_HEADS, axis=1)


def _tile_heads_fwd(x):
    return _tile_heads(x), None


def _tile_heads_bwd(_, g):
    w = g.shape[1] // N_HEADS
    acc = g[:, :w]
    for h in range(1, N_HEADS):
        acc = acc + g[:, h * w:(h + 1) * w]
    return (acc,)


_tile_heads.defvjp(_tile_heads_fwd, _tile_heads_bwd)


def _iota(shape, dim):
    return lax.broadcasted_iota(jnp.int32, shape, dim)


def _head_indicator_t(width, per_head):
    return (_iota((N_HEADS, width), 1) // per_head == _iota((N_HEADS, width), 0)).astype(F32)


def _rms(x, g):
    return x * lax.rsqrt(jnp.mean(x * x, axis=-1, keepdims=True) + EPS) * g


def _gelu(x):
    return 0.5 * x * (1.0 + lax.erf(x * (2.0 ** -0.5)))


def _head_sums(x, per_head):
    blocks = []
    for b in range(x.shape[1] // 128):
        blk = x[:, b * 128:(b + 1) * 128]
        if per_head == 128:
            blocks.append(jnp.broadcast_to(jnp.sum(blk, axis=1, keepdims=True), blk.shape))
        else:
            low = _iota((1, 128), 1) < per_head
            s_low = jnp.sum(jnp.where(low, blk, 0.0), axis=1, keepdims=True)
            s_high = jnp.sum(jnp.where(low, 0.0, blk), axis=1, keepdims=True)
            blocks.append(jnp.where(low, s_low, s_high))
    return jnp.concatenate(blocks, axis=1)


def _head_rms(x, g_full, per_head, n_real):
    return x * lax.rsqrt(_head_sums(x * x, per_head) * (1.0 / n_real) + EPS) * g_full


def _rope(x, cos_t, sin_hi, sin_lo):
    half = B_ROPE // 2
    return (x * _tile_heads(cos_t) + _roll(x, half, 1) * _tile_heads(sin_hi)
            + _roll(x, -half, 1) * _tile_heads(sin_lo))


def _tok_spec(t, width, col):
    return pl.BlockSpec((t, width), lambda i, c=col: (i, c))


def _par_spec(shape):
    nd = len(shape)
    return pl.BlockSpec(shape, lambda i: (0,) * nd)


def _local_fwd(name, f, toks, pars, outs, t):
    s = toks[0][0].shape[0]
    nt, npar = len(toks), len(pars)

    def body(*refs):
        vals = [r[...] for r in refs[:nt + npar]]
        for r, o in zip(refs[nt + npar:], f(*vals)):
            r[...] = o.astype(r.dtype)

    return pl.pallas_call(
        body, name=name, grid=(s // t,),
        in_specs=[_tok_spec(t, w, c) for _, w, c in toks] + [_par_spec(p.shape) for p in pars],
        out_specs=[_tok_spec(t, w, 0) for w, _ in outs],
        out_shape=[jax.ShapeDtypeStruct((s, w), dt) for w, dt in outs],
        compiler_params=_params(("parallel",)),
    )(*[a for a, _, _ in toks], *pars)


def _local_bwd(name, f, toks, pars, cots, t, tok_diff, par_diff):
    s = toks[0][0].shape[0]
    nt, npar, nc = len(toks), len(pars), len(cots)
    dt_idx = [k for k in range(nt) if tok_diff[k]]
    dp_idx = [k for k in range(npar) if par_diff[k]]

    def body(*refs):
        i = pl.program_id(0)
        tv = [r[...] for r in refs[:nt]]
        pv = [r[...] for r in refs[nt:nt + npar]]
        cv = [r[...].astype(F32) for r in refs[nt + npar:nt + npar + nc]]
        out_refs = refs[nt + npar + nc:]

        def g(*d):
            tt, pp = list(tv), list(pv)
            for k, val in zip(dt_idx, d[:len(dt_idx)]):
                tt[k] = val
            for k, val in zip(dp_idx, d[len(dt_idx):]):
                pp[k] = val
            return tuple(o.astype(F32) for o in f(*tt, *pp))

        _, vjp = jax.vjp(g, *[tv[k] for k in dt_idx], *[pv[k] for k in dp_idx])
        grads = vjp(tuple(cv))
        for r, gr in zip(out_refs[:len(dt_idx)], grads[:len(dt_idx)]):
            r[...] = gr.astype(r.dtype)
        par_refs = out_refs[len(dt_idx):]

        @pl.when(i == 0)
        def _():
            for r in par_refs:
                r[...] = jnp.zeros(r.shape, r.dtype)

        for r, gr in zip(par_refs, grads[len(dt_idx):]):
            r[...] += gr.astype(F32)

    res = pl.pallas_call(
        body, name=name, grid=(s // t,),
        in_specs=([_tok_spec(t, w, c) for _, w, c in toks] + [_par_spec(p.shape) for p in pars]
                  + [_tok_spec(t, c.shape[1], 0) for c in cots]),
        out_specs=([_tok_spec(t, toks[k][1], 0) for k in dt_idx] + [_par_spec(pars[k].shape) for k in dp_idx]),
        out_shape=([jax.ShapeDtypeStruct((s, toks[k][1]), F32) for k in dt_idx]
                   + [jax.ShapeDtypeStruct(pars[k].shape, F32) for k in dp_idx]),
        compiler_params=_params(("arbitrary",)),
    )(*[a for a, _, _ in toks], *pars, *cots)
    return res[:len(dt_idx)], res[len(dt_idx):]


def _f_norm(x, g):
    return (_rms(x, g),)


def _f_norm_res(x, g):
    return _rms(x, g), x


def _f_rope_tables(pos, inv_row):
    ang = pos.astype(F32) * inv_row
    lane = _iota(ang.shape, 1)
    sn = jnp.sin(ang)
    half = B_ROPE // 2
    sin_hi = jnp.where((lane >= B_NOPE + half) & (lane < B_QK), sn, 0.0)
    sin_lo = jnp.where((lane >= B_NOPE) & (lane < B_NOPE + half), -sn, 0.0)
    return jnp.cos(ang), sin_hi, sin_lo


def _f_sgu(p_a, g, w_s, b_s):
    u = _gelu(p_a[:, :A_WIDTH])
    v = _rms(_gelu(p_a[:, A_WIDTH:]), g)
    tril = _iota((SGU_CHUNK, SGU_CHUNK), 1) <= _iota((SGU_CHUNK, SGU_CHUNK), 0)
    w_cat = jnp.concatenate([jnp.where(tril, w_s[gi], 0.0) for gi in range(N_HEADS)], axis=1)
    group = _iota((1, A_WIDTH), 1) // (A_WIDTH // N_HEADS)
    v_stack = jnp.concatenate([jnp.where(group == gi, v, 0.0) for gi in range(N_HEADS)], axis=0)
    bias = _mmh("tn", b_s, _head_indicator_t(A_WIDTH, A_WIDTH // N_HEADS))
    return (u * (_mm("nn", False, w_cat, v_stack) + bias),)


def _f_mla_prep(p_b, cos_t, sin_hi, sin_lo, q_lat_g, kv_lat_g, wq, wk, wv, qn_g, kn_g):
    kv_lat, k_rope, q_lat = p_b[:, :B_KV_LORA], p_b[:, B_KV_LORA:B_KV_LORA + HEAD_PAD], p_b[:, B_KV_LORA + HEAD_PAD:]
    q = _mm("nn", False, _rms(q_lat, q_lat_g), wq)
    q = _rope(_head_rms(q, _tile_heads(qn_g), HEAD_PAD, B_QK), cos_t, sin_hi, sin_lo)
    kvn = _rms(kv_lat, kv_lat_g)
    k = _mm("nn", False, kvn, wk) + _tile_heads(k_rope)
    k = _rope(_head_rms(k, _tile_heads(kn_g), HEAD_PAD, B_QK), cos_t, sin_hi, sin_lo)
    return q, k, _mm("nn", False, kvn, wv)


def _f_gdn_pre(conv_pre, ba, a_log_row, dt_row):
    qkv = jax.nn.silu(conv_pre)

    def l2(x):
        return x * lax.rsqrt(_head_sums(x * x, C_DK) + EPS)

    lane = _iota(ba.shape, 1)
    g = -jnp.exp(a_log_row) * jax.nn.softplus(ba + dt_row)
    gb = jnp.where(lane < N_HEADS, jax.nn.sigmoid(ba), jnp.where(lane < 2 * N_HEADS, g, 0.0))
    return l2(qkv[:, :A_WIDTH]), l2(qkv[:, A_WIDTH:2 * A_WIDTH]), qkv[:, 2 * A_WIDTH:], gb


def _f_gdn_post(o, c_z, o_g):
    place = (_iota((C_DK, A_WIDTH), 1) % C_DK == _iota((C_DK, A_WIDTH), 0)).astype(F32)
    return (_head_rms(o, _mmh("nn", o_g, place), C_DK, C_DK) * jax.nn.silu(c_z),)


def _f_merge(p_g, y0, y1, y2):
    d = D_MODEL
    return (jax.nn.sigmoid(p_g[:, :d]) * y0 + jax.nn.sigmoid(p_g[:, d:2 * d]) * y1
            + jax.nn.sigmoid(p_g[:, 2 * d:]) * y2,)


def _pick(n, whole_up_to, candidates):
    if n <= whole_up_to:
        return n
    for c in candidates:
        if n % c == 0:
            return c
    return n


def _matmul(name, a, w, add=None, out_dtype=F32, tm=512, extras=(), epilogue=None, out_dtypes=None):
    m, k = a.shape
    n = w.shape[1]
    tm = min(tm, m)
    tk = _pick(k, 2304, (2048, 1536, 1024, 512))
    tn = _pick(n, 2304, (2048, 1536, 1024, 512))
    nk = k // tk
    if add is not None:
        extras, epilogue = (add,), lambda r, x: (r + x,)
    if epilogue is None:
        epilogue = lambda r: (r,)
    out_dtypes = out_dtypes or (out_dtype,)
    n_ex, n_out = len(extras), len(out_dtypes)

    def body(*refs):
        a_ref, w_ref = refs[0], refs[1]
        ex_refs, o_refs = refs[2:2 + n_ex], refs[2 + n_ex:2 + n_ex + n_out]
        kk = pl.program_id(2)
        part = jnp.dot(a_ref[...].astype(BF16), w_ref[...].astype(BF16), preferred_element_type=F32)

        def finish(total):
            for o_ref, r in zip(o_refs, epilogue(total, *[e[...] for e in ex_refs])):
                o_ref[...] = r.astype(o_ref.dtype)

        if nk == 1:
            finish(part)
            return
        acc = refs[-1]

        @pl.when(kk == 0)
        def _():
            acc[...] = part

        @pl.when((kk > 0) & (kk < nk - 1))
        def _():
            acc[...] += part

        @pl.when(kk == nk - 1)
        def _():
            finish(acc[...] + part)

    tile = pl.BlockSpec((tm, tn), lambda i, j, kk: (i, j))
    res = pl.pallas_call(
        body, name=name, grid=(m // tm, n // tn, nk),
        in_specs=[pl.BlockSpec((tm, tk), lambda i, j, kk: (i, kk)), pl.BlockSpec((tk, tn), lambda i, j, kk: (kk, j))]
        + [tile] * n_ex,
        out_specs=[tile] * n_out, out_shape=[jax.ShapeDtypeStruct((m, n), dt) for dt in out_dtypes],
        scratch_shapes=[pltpu.VMEM((tm, tn), F32)] if nk > 1 else [],
        compiler_params=_params(("parallel", "parallel", "arbitrary")),
    )(a, w, *extras)
    return res[0] if n_out == 1 else res


def _matmul_tn(name, a, b, a_col=None, tm=512):
    m = a.shape[0]
    k, acol = (a.shape[1], 0) if a_col is None else a_col
    n = b.shape[1]
    tm = min(tm, m)
    tk = _pick(k, 1536, (1024, 512))
    tn = _pick(n, 2304, (1024, 512))
    nm = m // tm

    def body(a_ref, b_ref, o_ref):
        mm = pl.program_id(2)
        part = lax.dot_general(a_ref[...].astype(BF16), b_ref[...].astype(BF16), (((0,), (0,)), ((), ())),
                               preferred_element_type=F32)

        @pl.when(mm == 0)
        def _():
            o_ref[...] = part

        @pl.when(mm > 0)
        def _():
            o_ref[...] += part

    kb = k // tk
    return pl.pallas_call(
        body, name=name, grid=(kb, n // tn, nm),
        in_specs=[pl.BlockSpec((tm, tk), lambda i, j, mm: (mm, acol * kb + i)),
                  pl.BlockSpec((tm, tn), lambda i, j, mm: (mm, j))],
        out_specs=pl.BlockSpec((tk, tn), lambda i, j, mm: (i, j)),
        out_shape=jax.ShapeDtypeStruct((k, n), F32),
        compiler_params=_params(("parallel", "parallel", "arbitrary")),
    )(a, b)


def _shift_down(x, prev, s):
    rolled = pltpu.roll(x, s, 0)
    pr = pltpu.roll(prev, s, 0)
    head = jnp.where(_iota((8, 1), 0) < s, pr, rolled[:8])
    return jnp.concatenate([head, rolled[8:]], axis=0)


def _shift_up(x, nxt, s):
    t = x.shape[0]
    rolled = pltpu.roll(x, t - s, 0)
    nr = pltpu.roll(nxt, 8 - s, 0)
    tail = jnp.where(_iota((8, 1), 0) >= 8 - s, nr, rolled[t - 8:])
    return jnp.concatenate([rolled[:t - 8], tail], axis=0)


def _conv_fwd(p_c, w8, t):
    s = p_c.shape[0]
    t = min(t, s)
    r = t // 8

    def body(x_ref, prev_ref, w_ref, o_ref):
        i = pl.program_id(0)
        x = x_ref[...]
        prev = jnp.where(i == 0, 0.0, prev_ref[...])
        acc = w_ref[3:4, :] * x
        for sh in range(1, 4):
            acc = acc + w_ref[3 - sh:4 - sh, :] * _shift_down(x, prev, sh)
        o_ref[...] = acc

    return pl.pallas_call(
        body, name="conv_fwd", grid=(s // t,),
        in_specs=[pl.BlockSpec((t, C_QKV), lambda i: (i, 0)),
                  pl.BlockSpec((8, C_QKV), lambda i: (jnp.maximum(i * r - 1, 0), 0)),
                  pl.BlockSpec((8, C_QKV), lambda i: (0, 0))],
        out_specs=pl.BlockSpec((t, C_QKV), lambda i: (i, 0)),
        out_shape=jax.ShapeDtypeStruct((s, C_QKV), F32),
        compiler_params=_params(("parallel",)),
    )(p_c, p_c, w8)


def _conv_bwd(p_c, dy, w8, t):
    s = p_c.shape[0]
    t = min(t, s)
    r = t // 8
    n = s // t

    def body(x_ref, prev_ref, dy_ref, next_ref, w_ref, dx_ref, dw_ref):
        i = pl.program_id(0)
        x, g = x_ref[...], dy_ref[...]
        prev = jnp.where(i == 0, 0.0, prev_ref[...])
        nxt = jnp.where(i == n - 1, 0.0, next_ref[...])

        @pl.when(i == 0)
        def _():
            dw_ref[...] = jnp.zeros(dw_ref.shape, F32)

        dx = w_ref[3:4, :] * g
        dw_ref[3:4, :] += jnp.sum(g * x, axis=0, keepdims=True)
        for sh in range(1, 4):
            dx = dx + w_ref[3 - sh:4 - sh, :] * _shift_up(g, nxt, sh)
            dw_ref[3 - sh:4 - sh, :] += jnp.sum(g * _shift_down(x, prev, sh), axis=0, keepdims=True)
        dx_ref[...] = dx

    return pl.pallas_call(
        body, name="conv_bwd", grid=(n,),
        in_specs=[pl.BlockSpec((t, C_QKV), lambda i: (i, 0)),
                  pl.BlockSpec((8, C_QKV), lambda i: (jnp.maximum(i * r - 1, 0), 0)),
                  pl.BlockSpec((t, C_QKV), lambda i: (i, 0)),
                  pl.BlockSpec((8, C_QKV), lambda i: (jnp.minimum((i + 1) * r, s // 8 - 1), 0)),
                  pl.BlockSpec((8, C_QKV), lambda i: (0, 0))],
        out_specs=[pl.BlockSpec((t, C_QKV), lambda i: (i, 0)), pl.BlockSpec((8, C_QKV), lambda i: (0, 0))],
        out_shape=[jax.ShapeDtypeStruct((s, C_QKV), F32), jax.ShapeDtypeStruct((8, C_QKV), F32)],
        compiler_params=_params(("arbitrary",)),
    )(p_c, p_c, dy, dy, w8)


def _delta_chunks(state, q, k, v, g, beta):
    pre = [_delta_pre(q[:, i * C_CHUNK:(i + 1) * C_CHUNK], k[:, i * C_CHUNK:(i + 1) * C_CHUNK],
                      v[:, i * C_CHUNK:(i + 1) * C_CHUNK], g[i], beta[i]) for i in range(DELTA_SUB)]
    outs = []
    for qg, intra, u, w, k_dec, dec in pre:
        v_new = u - _mm("nn", True, w, state)
        outs.append(_mm("nn", True, qg, state) + _mm("nn", True, intra, v_new))
        state = state * dec + _mm("tn", True, k_dec, v_new)
    return state, jnp.concatenate(outs, axis=1)


def _delta_pre(q, k, v, g, beta):
    c = C_CHUNK
    row, col = _iota((c, c), 0), _iota((c, c), 1)
    tril, strict = col <= row, col < row
    gc = _mmh("nn", g, (row <= col).astype(F32))
    g_last = jnp.sum(g, axis=1, keepdims=True)
    qs = q * (C_DK ** -0.5)
    decay = jnp.exp(jnp.where(tril, gc[:, :, None] - gc[:, None, :], NEG))
    k_beta, v_beta = k * beta[:, :, None], v * beta[:, :, None]
    x = -jnp.where(strict, _mm("nt", True, k_beta, k) * decay, 0.0)
    xd = jnp.where(row // DELTA_BLOCK == col // DELTA_BLOCK, x, 0.0)
    powers = [xd]
    for _ in range(3):
        powers.append(_mm("nn", True, powers[-1], powers[-1]))
    sol = jnp.concatenate([x - xd, v_beta, k_beta * jnp.exp(gc)[:, :, None]], axis=2)
    for p in reversed(powers):
        sol = sol + _mm("nn", True, p, sol)
    y, sol = sol[:, :, :c], sol[:, :, c:]
    sol = sol + _mm("nn", True, _mm("nn", True, y, y), sol)
    sol = sol + _mm("nn", True, y, sol)
    u, w = sol[:, :, :C_DK], sol[:, :, C_DK:]
    intra = jnp.where(tril, _mm("nt", True, qs, k) * decay, 0.0)
    return (qs * jnp.exp(gc)[:, :, None], intra, u, w, k * jnp.exp(g_last - gc)[:, :, None],
            jnp.exp(g_last)[:, :, None])


def _delta_specs(n, rev):
    def at(i):
        return n - 1 - i if rev else i
    tok = pl.BlockSpec((N_HEADS, DELTA_SUB * C_CHUNK, C_DK), lambda i: (0, at(i), 0))
    vec = pl.BlockSpec((DELTA_SUB, N_HEADS, C_CHUNK), lambda i: (at(i), 0, 0))
    st = pl.BlockSpec((1, N_HEADS, C_DK, C_DK), lambda i: (at(i), 0, 0, 0))
    return tok, vec, st


def _delta_fwd(q, k, v, g, beta):
    s = q.shape[1]
    n = s // (DELTA_SUB * C_CHUNK)
    tok, vec, st = _delta_specs(n, False)

    def body(q_ref, k_ref, v_ref, g_ref, b_ref, o_ref, st_ref, state):
        @pl.when(pl.program_id(0) == 0)
        def _():
            state[...] = jnp.zeros(state.shape, F32)

        cur = state[...]
        st_ref[0] = cur
        new, o = _delta_chunks(cur, q_ref[...], k_ref[...], v_ref[...], g_ref[...], b_ref[...])
        o_ref[...] = o
        state[...] = new

    return pl.pallas_call(
        body, name="delta_fwd", grid=(n,), in_specs=[tok, tok, tok, vec, vec], out_specs=[tok, st],
        out_shape=[jax.ShapeDtypeStruct((N_HEADS, s, C_DK), F32), jax.ShapeDtypeStruct((n, N_HEADS, C_DK, C_DK), F32)],
        scratch_shapes=[pltpu.VMEM((N_HEADS, C_DK, C_DK), F32)],
        compiler_params=_params(("arbitrary",)),
    )(q, k, v, g, beta)


def _delta_bwd(q, k, v, g, beta, states, do):
    s = q.shape[1]
    n = s // (DELTA_SUB * C_CHUNK)
    tok, vec, st = _delta_specs(n, True)

    def body(q_ref, k_ref, v_ref, g_ref, b_ref, st_ref, do_ref, dq_ref, dk_ref, dv_ref, dg_ref, db_ref, dstate):
        @pl.when(pl.program_id(0) == 0)
        def _():
            dstate[...] = jnp.zeros(dstate.shape, F32)

        _, vjp = jax.vjp(_delta_chunks, st_ref[0], q_ref[...], k_ref[...], v_ref[...], g_ref[...], b_ref[...])
        dst, dq, dk, dv, dg, db = vjp((dstate[...], do_ref[...]))
        dq_ref[...], dk_ref[...], dv_ref[...] = dq, dk, dv
        dg_ref[...], db_ref[...] = dg, db
        dstate[...] = dst

    tok_shape = jax.ShapeDtypeStruct((N_HEADS, s, C_DK), F32)
    vec_shape = jax.ShapeDtypeStruct((n * DELTA_SUB, N_HEADS, C_CHUNK), F32)
    return pl.pallas_call(
        body, name="delta_bwd", grid=(n,), in_specs=[tok, tok, tok, vec, vec, st, tok],
        out_specs=[tok, tok, tok, vec, vec], out_shape=[tok_shape, tok_shape, tok_shape, vec_shape, vec_shape],
        scratch_shapes=[pltpu.VMEM((N_HEADS, C_DK, C_DK), F32)],
        compiler_params=_params(("arbitrary",)),
    )(q, k, v, g, beta, states, do)


ATT_SCALE = B_QK ** -0.5


SCORE_SCALE_LOG2 = ATT_SCALE * math.log2(math.e)


def _lanes(x, n):
    return x if n == 1 else jnp.concatenate([x] * n, axis=1)


def _scores(a_ref, b_ref):
    return lax.dot_general(a_ref[...], b_ref[...], (((1,), (1,)), ((), ())), preferred_element_type=F32) * SCORE_SCALE_LOG2


def _flash_fwd(q, k, v, tb):
    s = q.shape[0]
    tb = min(tb, s)
    nb = s // tb
    nrep = tb // 128

    def body(q_ref, k_ref, v_ref, o_ref, lse_ref, m_s, l_s, acc):
        i, j = pl.program_id(1), pl.program_id(2)

        @pl.when(j == 0)
        def _():
            m_s[...] = jnp.full(m_s.shape, -jnp.inf, F32)
            l_s[...] = jnp.zeros(l_s.shape, F32)
            acc[...] = jnp.zeros(acc.shape, F32)

        def step(masked):
            sc = _scores(q_ref, k_ref)
            if masked:
                sc = jnp.where(_iota((tb, tb), 1) <= _iota((tb, tb), 0), sc, -jnp.inf)
            m_prev = m_s[...]
            m_new = jnp.maximum(m_prev, jnp.max(sc, axis=1, keepdims=True))
            alpha = jnp.exp2(m_prev - m_new)
            p = jnp.exp2(sc - _lanes(m_new, nrep))
            l_s[...] = alpha * l_s[...] + jnp.sum(p, axis=1, keepdims=True)
            acc[...] = alpha * acc[...] + jnp.dot(p.astype(BF16), v_ref[...], preferred_element_type=F32)
            m_s[...] = m_new

        @pl.when(j < i)
        def _():
            step(False)

        @pl.when(j == i)
        def _():
            step(True)
            o_ref[...] = acc[...] / l_s[...]
            lse_ref[...] = m_s[...] + jnp.log2(l_s[...])

    qs = pl.BlockSpec((tb, HEAD_PAD), lambda h, i, j: (i, h))
    ks = pl.BlockSpec((tb, HEAD_PAD), lambda h, i, j: (jnp.minimum(j, i), h))
    shape = jax.ShapeDtypeStruct((s, N_HEADS * HEAD_PAD), F32)
    return pl.pallas_call(
        body, name="flash_fwd", grid=(N_HEADS, nb, nb), in_specs=[qs, ks, ks],
        out_specs=[qs, qs], out_shape=[shape, shape],
        scratch_shapes=[pltpu.VMEM((tb, 128), F32), pltpu.VMEM((tb, 128), F32), pltpu.VMEM((tb, HEAD_PAD), F32)],
        compiler_params=_params(("parallel", "parallel", "arbitrary")),
    )(q, k, v)


def _f_attn_delta(o, do):
    return (_head_sums(o * do, HEAD_PAD),)


def _row_stats(rep):
    return rep[:, ::HEAD_PAD].T.reshape(N_HEADS, 1, -1)


def _flash_bwd(q, k, v, do, lse_row, delta_row, tb):
    s = q.shape[0]
    tb = min(tb, s)
    nb = s // tb

    def body(q_ref, k_ref, v_ref, do_ref, lse_ref, dl_ref, dq_ref, dk_ref, dv_ref, dk_acc, dv_acc):
        j, i = pl.program_id(1), pl.program_id(2)

        @pl.when((i == 0) & (j == 0))
        def _():
            dq_ref[...] = jnp.zeros(dq_ref.shape, F32)

        @pl.when(i == 0)
        def _():
            dk_acc[...] = jnp.zeros(dk_acc.shape, F32)
            dv_acc[...] = jnp.zeros(dv_acc.shape, F32)

        def step(masked):
            st = _scores(k_ref, q_ref)
            if masked:
                st = jnp.where(_iota((tb, tb), 0) <= _iota((tb, tb), 1), st, -jnp.inf)
            do = do_ref[...].astype(BF16)
            pt = jnp.exp2(st - lse_ref[0])
            dpt = lax.dot_general(v_ref[...], do, (((1,), (1,)), ((), ())), preferred_element_type=F32)
            dst = (pt * (dpt - dl_ref[0])).astype(BF16)
            dv_acc[...] += jnp.dot(pt.astype(BF16), do, preferred_element_type=F32)
            dk_acc[...] += jnp.dot(dst, q_ref[...], preferred_element_type=F32)
            rows = pl.ds(pl.multiple_of(i * tb, tb), tb)
            dq_ref[rows, :] += lax.dot_general(dst, k_ref[...], (((0,), (0,)), ((), ())),
                                               preferred_element_type=F32) * ATT_SCALE

        @pl.when(i == j)
        def _():
            step(True)

        @pl.when(i > j)
        def _():
            step(False)

        @pl.when(i == nb - 1)
        def _():
            dk_ref[...] = dk_acc[...] * ATT_SCALE
            dv_ref[...] = dv_acc[...]

    qs = pl.BlockSpec((tb, HEAD_PAD), lambda h, j, i: (jnp.maximum(i, j), h))
    ks = pl.BlockSpec((tb, HEAD_PAD), lambda h, j, i: (j, h))
    rs = pl.BlockSpec((1, 1, tb), lambda h, j, i: (h, 0, jnp.maximum(i, j)))
    shape = jax.ShapeDtypeStruct((s, N_HEADS * HEAD_PAD), F32)
    return pl.pallas_call(
        body, name="flash_bwd", grid=(N_HEADS, nb, nb), in_specs=[qs, ks, ks, qs, rs, rs],
        out_specs=[pl.BlockSpec((s, HEAD_PAD), lambda h, j, i: (0, h)), ks, ks], out_shape=[shape, shape, shape],
        scratch_shapes=[pltpu.VMEM((tb, HEAD_PAD), F32), pltpu.VMEM((tb, HEAD_PAD), F32)],
        compiler_params=_params(("parallel", "arbitrary", "arbitrary")),
    )(q, k, v, do, lse_row, delta_row)


def _loss_head(y, target, t):
    s, d = y.shape
    t = min(t, s)

    def body(y_ref, t_ref, sum_ref, dy_ref):
        @pl.when(pl.program_id(0) == 0)
        def _():
            sum_ref[...] = jnp.zeros(sum_ref.shape, F32)

        err = y_ref[...] - t_ref[...]
        dy_ref[...] = err * (1.0 / d)
        sum_ref[...] += jnp.broadcast_to(jnp.sum(err * err), sum_ref.shape)

    return pl.pallas_call(
        body, name="loss_head", grid=(s // t,),
        in_specs=[pl.BlockSpec((t, d), lambda i: (i, 0))] * 2,
        out_specs=[pl.BlockSpec((1, 128), lambda i: (0, 0)), pl.BlockSpec((t, d), lambda i: (i, 0))],
        out_shape=[jax.ShapeDtypeStruct((1, 128), F32), jax.ShapeDtypeStruct((s, d), F32)],
        compiler_params=_params(("arbitrary",)),
    )(y, target)


def _pad_to(a, axis, size):
    pad = [(0, 0)] * a.ndim
    pad[axis] = (0, size - a.shape[axis])
    return jnp.pad(a, pad)


W_IN_GROUPS = {
    "a": [(0, O_QLAT)],
    "b": [(O_KVLAT, O_KROPE), B_NOPE, (O_KROPE, O_CQKV), HEAD_PAD - B_QK, (O_QLAT, O_KVLAT)],
    "c": [(O_CQKV, O_GATES), W_C - (O_GATES - O_CQKV)],
    "g": [(O_GATES, D_IN)],
}
W_IN_SHARD = D_IN // N_DEV
W_IN_ROWS = 128


def _group_width(key):
    return sum(e if isinstance(e, int) else e[1] - e[0] for e in W_IN_GROUPS[key])


def _assemble_w_in(shards):
    depth = shards.shape[1]
    keys = list(W_IN_GROUPS)

    def body(s_ref, *o_refs):
        vals = [s_ref[k, 0] for k in range(N_DEV)]
        for key, o_ref in zip(keys, o_refs):
            parts = []
            for e in W_IN_GROUPS[key]:
                if isinstance(e, int):
                    parts.append(jnp.zeros((W_IN_ROWS, e), shards.dtype))
                    continue
                lo, hi = e
                while lo < hi:
                    k = lo // W_IN_SHARD
                    end = min(hi, (k + 1) * W_IN_SHARD)
                    parts.append(vals[k][:, lo - k * W_IN_SHARD:end - k * W_IN_SHARD])
                    lo = end
            o_ref[0] = parts[0] if len(parts) == 1 else jnp.concatenate(parts, axis=1)

    return pl.pallas_call(
        body, name="assemble_w_in", grid=(depth, D_MODEL // W_IN_ROWS),
        in_specs=[pl.BlockSpec((N_DEV, 1, W_IN_ROWS, W_IN_SHARD), lambda l, i: (0, l, i, 0))],
        out_specs=[pl.BlockSpec((1, W_IN_ROWS, _group_width(key)), lambda l, i: (l, i, 0)) for key in keys],
        out_shape=[jax.ShapeDtypeStruct((depth, D_MODEL, _group_width(key)), shards.dtype) for key in keys],
        compiler_params=_params(("parallel", "parallel")),
    )(shards)


def _split_dw_in(groups):
    keys = list(W_IN_GROUPS)
    depth = groups[0].shape[0]
    runs = []
    for gi, key in enumerate(keys):
        col = 0
        for e in W_IN_GROUPS[key]:
            if not isinstance(e, int):
                runs.append((e[0], e[1], gi, col))
            col += e if isinstance(e, int) else e[1] - e[0]
    runs.sort()

    def body(*refs):
        vals = [r[0] for r in refs[:len(keys)]]
        o_ref = refs[len(keys)]
        for k in range(N_DEV):
            lo, hi = k * W_IN_SHARD, (k + 1) * W_IN_SHARD
            parts = []
            for a, b, gi, col in runs:
                s, e = max(a, lo), min(b, hi)
                if s < e:
                    parts.append(vals[gi][:, col + s - a:col + e - a])
            o_ref[k, 0] = jnp.concatenate(parts, axis=1)

    return pl.pallas_call(
        body, name="split_dw_in", grid=(depth, D_MODEL // W_IN_ROWS),
        in_specs=[pl.BlockSpec((1, W_IN_ROWS, _group_width(key)), lambda l, i: (l, i, 0)) for key in keys],
        out_specs=pl.BlockSpec((N_DEV, 1, W_IN_ROWS, W_IN_SHARD), lambda l, i: (0, l, i, 0)),
        out_shape=jax.ShapeDtypeStruct((N_DEV, depth, D_MODEL, W_IN_SHARD), F32),
        compiler_params=_params(("parallel", "parallel")),
    )(*groups)


def _prep_layer(w, l):
    p = {}
    for key in W_IN_GROUPS:
        val = w["w_in_" + key][l].astype(BF16)
        p["w_" + key] = val
        p["wt_" + key] = val.T
    for name in ("norm1_g", "sgu_norm_g", "q_lat_norm_g", "kv_lat_norm_g", "o_norm_g", "norm2_g"):
        p[name] = w[name][l][None, :]
    p["w_spatial"], p["b_spatial"] = w["w_spatial"][l], w["b_spatial"][l]
    p["wq"] = _pad_to(w["w_q_up"][l].astype(F32).reshape(B_Q_LORA, N_HEADS, B_QK), 2, HEAD_PAD).reshape(B_Q_LORA, -1)
    kv = w["w_kv_up"][l].astype(F32).reshape(B_KV_LORA, N_HEADS, B_NOPE + B_VDIM)
    p["wk"] = _pad_to(kv[:, :, :B_NOPE], 2, HEAD_PAD).reshape(B_KV_LORA, -1)
    p["wv"] = _pad_to(kv[:, :, B_NOPE:], 2, HEAD_PAD).reshape(B_KV_LORA, -1)
    p["qn_g"] = _pad_to(w["q_norm_g"][l][None, :], 1, HEAD_PAD)
    p["kn_g"] = _pad_to(w["k_norm_g"][l][None, :], 1, HEAD_PAD)
    p["conv_w"] = _pad_to(w["conv_w"][l], 0, 8)
    row = lambda v: jnp.pad(v[None, :], ((0, 0), (N_HEADS, HEAD_PAD - 2 * N_HEADS)))
    p["a_log"], p["dt_bias"] = row(w["a_log"][l]), row(w["dt_bias"][l])
    wb = w["w_branch"][l]
    wb1 = _pad_to(wb[1].reshape(N_HEADS, B_VDIM, D_MODEL), 1, HEAD_PAD).reshape(-1, D_MODEL)
    for key, val in (("wb0", wb[0]), ("wb1", wb1), ("wb2", wb[2]), ("w_out", w["w_out"][l]),
                     ("w_ff1", w["w_ff1"][l]), ("w_ff2", w["w_ff2"][l])):
        p[key] = val.astype(BF16)
        p[key + "_t"] = val.T.astype(BF16)
    return p


def _unprep_grads(g):
    out = {"w_in_" + key: g["w_" + key] for key in W_IN_GROUPS}
    for name in ("norm1_g", "sgu_norm_g", "q_lat_norm_g", "kv_lat_norm_g", "o_norm_g", "norm2_g"):
        out[name] = g[name][0]
    out["w_spatial"], out["b_spatial"] = g["w_spatial"], g["b_spatial"]
    out["w_q_up"] = g["wq"].reshape(B_Q_LORA, N_HEADS, HEAD_PAD)[:, :, :B_QK].reshape(B_Q_LORA, -1)
    gk = g["wk"].reshape(B_KV_LORA, N_HEADS, HEAD_PAD)[:, :, :B_NOPE]
    gv = g["wv"].reshape(B_KV_LORA, N_HEADS, HEAD_PAD)[:, :, :B_VDIM]
    out["w_kv_up"] = jnp.concatenate([gk, gv], axis=2).reshape(B_KV_LORA, -1)
    out["q_norm_g"], out["k_norm_g"] = g["qn_g"][0, :B_QK], g["kn_g"][0, :B_QK]
    out["conv_w"] = g["conv_w"][:4]
    out["a_log"], out["dt_bias"] = g["a_log"][0, N_HEADS:2 * N_HEADS], g["dt_bias"][0, N_HEADS:2 * N_HEADS]
    gb1 = g["wb1"].reshape(N_HEADS, HEAD_PAD, D_MODEL)[:, :B_VDIM].reshape(-1, D_MODEL)
    out["w_branch"] = jnp.stack([g["wb0"], gb1, g["wb2"]], axis=0)
    out["w_out"], out["w_ff1"], out["w_ff2"] = g["w_out"], g["w_ff1"], g["w_ff2"]
    return out


def _heads_first(a):
    s = a.shape[0]
    return a.reshape(s, N_HEADS, C_DK).transpose(1, 0, 2)


def _heads_last(a):
    return a.transpose(1, 0, 2).reshape(a.shape[1], N_HEADS * C_DK)


def _chunk_vec(a):
    return a.reshape(-1, C_CHUNK, N_HEADS).transpose(0, 2, 1)


def _unchunk_vec(a):
    return a.transpose(0, 2, 1).reshape(-1, N_HEADS)


def _layer_fwd(x, p, tabs, t, tb):
    sv = {"x": x}
    h1, = _local_fwd("norm1", _f_norm, [(x, D_MODEL, 0)], [p["norm1_g"]], [(D_MODEL, BF16)], t)
    sv["h1"] = h1
    p_a, p_b, p_c, p_g = (_matmul("proj_" + key, h1, p["w_" + key]) for key in "abcg")
    sv.update(p_a=p_a, p_b=p_b, p_c=p_c, p_g=p_g)
    y_a, = _local_fwd("sgu", _f_sgu, [(p_a, 2 * A_WIDTH, 0)], [p["sgu_norm_g"], p["w_spatial"], p["b_spatial"]],
                      [(A_WIDTH, BF16)], SGU_CHUNK)
    q, k, v = _local_fwd("mla_prep", _f_mla_prep, [(p_b, W_B, 0)] + [(tb_, HEAD_PAD, 0) for tb_ in tabs],
                         [p["q_lat_norm_g"], p["kv_lat_norm_g"], p["wq"], p["wk"], p["wv"], p["qn_g"], p["kn_g"]],
                         [(N_HEADS * HEAD_PAD, BF16)] * 3, min(t, 256))
    o_b, lse = _flash_fwd(q, k, v, tb)
    sv.update(q=q, k=k, v=v, o_b=o_b, lse=lse)
    conv_pre = _conv_fwd(p_c, p["conv_w"], t)
    cq, ck, cv, gb = _local_fwd("gdn_pre", _f_gdn_pre, [(conv_pre, C_QKV, 0), (p_c, HEAD_PAD, (C_QKV + C_Z) // HEAD_PAD)],
                                [p["a_log"], p["dt_bias"]], [(A_WIDTH, F32)] * 3 + [(HEAD_PAD, F32)], t)
    cq, ck, cv = _heads_first(cq), _heads_first(ck), _heads_first(cv)
    beta, g = _chunk_vec(gb[:, :N_HEADS]), _chunk_vec(gb[:, N_HEADS:2 * N_HEADS])
    o_c, states = _delta_fwd(cq, ck, cv, g, beta)
    o_c = _heads_last(o_c)
    sv.update(conv_pre=conv_pre, cq=cq, ck=ck, cv=cv, beta=beta, g=g, states=states, o_c=o_c)
    y_c, = _local_fwd("gdn_post", _f_gdn_post, [(o_c, A_WIDTH, 0), (p_c, C_Z, C_QKV // C_Z)], [p["o_norm_g"]],
                      [(A_WIDTH, BF16)], t)
    y0 = _matmul("branch0", y_a, p["wb0"])
    y1 = _matmul("branch1", o_b, p["wb1"])
    y2 = _matmul("branch2", y_c, p["wb2"])
    merged, = _local_fwd("merge", _f_merge, [(p_g, 3 * D_MODEL, 0), (y0, D_MODEL, 0), (y1, D_MODEL, 0), (y2, D_MODEL, 0)],
                         [], [(D_MODEL, BF16)], t)
    x1 = _matmul("out_proj", merged, p["w_out"], add=x)
    sv.update(y_a=y_a, y_c=y_c, y0=y0, y1=y1, y2=y2, merged=merged, x1=x1)
    h2, = _local_fwd("norm2", _f_norm, [(x1, D_MODEL, 0)], [p["norm2_g"]], [(D_MODEL, BF16)], t)
    a, r = _matmul("ff1", h2, p["w_ff1"], epilogue=lambda acc: (acc, jnp.square(jnp.maximum(acc, 0.0))),
                   out_dtypes=(BF16, BF16))
    x2 = _matmul("ff2", r, p["w_ff2"], add=x1)
    sv.update(h2=h2, a=a, r=r)
    return x2, sv


def _layer_bwd(dx2, sv, p, tabs, t, tb):
    g = {}
    da = _matmul("d_ff2", dx2, p["w_ff2_t"], extras=(sv["a"],), out_dtypes=(BF16,),
                 epilogue=lambda dr, a: (dr * (2.0 * jnp.maximum(a.astype(F32), 0.0)),))
    g["w_ff2"] = _matmul_tn("dw_ff2", sv["r"], dx2)
    dh2 = _matmul("d_ff1", da, p["w_ff1_t"])
    g["w_ff1"] = _matmul_tn("dw_ff1", sv["h2"], da)
    (dx1,), (g["norm2_g"],) = _local_bwd("norm2_bwd", _f_norm_res, [(sv["x1"], D_MODEL, 0)], [p["norm2_g"]],
                                         [dh2, dx2], t, [True], [True])
    dmerged = _matmul("d_out_proj", dx1, p["w_out_t"])
    g["w_out"] = _matmul_tn("dw_out", sv["merged"], dx1)
    (dp_g, dy0, dy1, dy2), _ = _local_bwd(
        "merge_bwd", _f_merge, [(sv["p_g"], 3 * D_MODEL, 0), (sv["y0"], D_MODEL, 0), (sv["y1"], D_MODEL, 0),
                                (sv["y2"], D_MODEL, 0)], [], [dmerged], min(t, 256), [True] * 4, [])
    dy_a = _matmul("d_branch0", dy0, p["wb0_t"])
    do_b = _matmul("d_branch1", dy1, p["wb1_t"])
    dy_c = _matmul("d_branch2", dy2, p["wb2_t"])
    g["wb0"] = _matmul_tn("dw_branch0", sv["y_a"], dy0)
    g["wb1"] = _matmul_tn("dw_branch1", sv["o_b"], dy1)
    g["wb2"] = _matmul_tn("dw_branch2", sv["y_c"], dy2)
    p_c = sv["p_c"]
    (do_c, dc_z), (g["o_norm_g"],) = _local_bwd(
        "gdn_post_bwd", _f_gdn_post, [(sv["o_c"], A_WIDTH, 0), (p_c, C_Z, C_QKV // C_Z)], [p["o_norm_g"]], [dy_c], t,
        [True, True], [True])
    dcq, dck, dcv, dg, dbeta = _delta_bwd(sv["cq"], sv["ck"], sv["cv"], sv["g"], sv["beta"], sv["states"],
                                          _heads_first(do_c))
    dgb = jnp.pad(jnp.concatenate([_unchunk_vec(dbeta), _unchunk_vec(dg)], axis=1),
                  ((0, 0), (0, HEAD_PAD - 2 * N_HEADS)))
    (dconv, dba), (g["a_log"], g["dt_bias"]) = _local_bwd(
        "gdn_pre_bwd", _f_gdn_pre, [(sv["conv_pre"], C_QKV, 0), (p_c, HEAD_PAD, (C_QKV + C_Z) // HEAD_PAD)],
        [p["a_log"], p["dt_bias"]], [_heads_last(dcq), _heads_last(dck), _heads_last(dcv), dgb], t,
        [True, True], [True, True])
    dc_qkv, g["conv_w"] = _conv_bwd(p_c, dconv, p["conv_w"], t)
    dp_c = jnp.concatenate([dc_qkv, dc_z, dba], axis=1)
    delta, = _local_fwd("attn_delta", _f_attn_delta, [(sv["o_b"], N_HEADS * HEAD_PAD, 0), (do_b, N_HEADS * HEAD_PAD, 0)], [],
                        [(N_HEADS * HEAD_PAD, F32)], t)
    dq, dk, dv = _flash_bwd(sv["q"], sv["k"], sv["v"], do_b, _row_stats(sv["lse"]), _row_stats(delta), tb)
    mla_pars = [p["q_lat_norm_g"], p["kv_lat_norm_g"], p["wq"], p["wk"], p["wv"], p["qn_g"], p["kn_g"]]
    (dp_b,), mla_g = _local_bwd(
        "mla_prep_bwd", _f_mla_prep, [(sv["p_b"], W_B, 0)] + [(tb_, HEAD_PAD, 0) for tb_ in tabs], mla_pars,
        [dq, dk, dv], min(t, 256), [True, False, False, False], [True] * 7)
    for name, val in zip(("q_lat_norm_g", "kv_lat_norm_g", "wq", "wk", "wv", "qn_g", "kn_g"), mla_g):
        g[name] = val
    (dp_a,), (g["sgu_norm_g"], g["w_spatial"], g["b_spatial"]) = _local_bwd(
        "sgu_bwd", _f_sgu, [(sv["p_a"], 2 * A_WIDTH, 0)], [p["sgu_norm_g"], p["w_spatial"], p["b_spatial"]], [dy_a],
        SGU_CHUNK, [True], [True] * 3)
    dh1 = None
    for key, dp in (("a", dp_a), ("b", dp_b), ("c", dp_c), ("g", dp_g)):
        dh1 = _matmul("d_proj_" + key, dp, p["wt_" + key], add=dh1)
        g["w_" + key] = _matmul_tn("dw_proj_" + key, sv["h1"], dp)
    (dx,), (g["norm1_g"],) = _local_bwd("norm1_bwd", _f_norm_res, [(sv["x"], D_MODEL, 0)], [p["norm1_g"]],
                                        [dh1, dx1], t, [True], [True])
    return dx, _unprep_grads(g)


def _local_step(x, positions, w, target, t=512, tb=1024):
    s = x.shape[0]
    t = min(t, s)
    w = dict(w)
    for key, val in zip(W_IN_GROUPS, _assemble_w_in(w["w_in"])):
        w["w_in_" + key] = val
    half = B_ROPE // 2
    inv_freq = 1.0 / (ROPE_BASE ** (jnp.arange(half, dtype=F32) / half))
    inv_row = jnp.concatenate([jnp.zeros((B_NOPE,), F32), inv_freq, inv_freq, jnp.zeros((HEAD_PAD - B_QK,), F32)])[None, :]
    tabs = _local_fwd("rope_tables", _f_rope_tables, [(positions, 1, 0)], [inv_row], [(HEAD_PAD, F32)] * 3, t)
    preps, saved = [], []
    for l in range(DEPTH):
        preps.append(_prep_layer(w, l))
        x, sv = _layer_fwd(x, preps[l], tabs, t, tb)
        saved.append(sv)
    sq, dx = _loss_head(x, target, t)
    grads = [None] * DEPTH
    for l in reversed(range(DEPTH)):
        dx, grads[l] = _layer_bwd(dx, saved[l], preps[l], tabs, t, tb)
    stacked = lambda name: jnp.stack([grads[l][name] for l in range(DEPTH)], axis=0)
    out = {name: stacked(name) for name in WEIGHTS if name != "w_in"}
    out["w_in"] = _split_dw_in([stacked("w_in_" + key) for key in W_IN_GROUPS])
    return sq, dx, out


def _exchange(name, arrays, gather):
    n = len(arrays)

    def body(*refs):
        send, recv = refs[:n], refs[n:2 * n]
        send_sems, recv_sems, local_sems = refs[2 * n:]
        x, y, c = lax.axis_index("x"), lax.axis_index("y"), lax.axis_index("c")
        me = 4 * x + 2 * y + c

        def src(a, idx):
            return send[a] if gather[a] else send[a].at[idx]

        local = [pltpu.make_async_copy(src(a, me), recv[a].at[me], local_sems.at[a]) for a in range(n)]
        for cp in local:
            cp.start()
        copies = []
        for d in range(1, N_DEV):
            px, py, pc = x ^ ((d >> 2) & 1), y ^ ((d >> 1) & 1), c ^ (d & 1)
            peer = 4 * px + 2 * py + pc
            for a in range(n):
                cp = pltpu.make_async_remote_copy(
                    src_ref=src(a, peer), dst_ref=recv[a].at[me], send_sem=send_sems.at[a, d],
                    recv_sem=recv_sems.at[a, d], device_id=(px, py, pc), device_id_type=pl.DeviceIdType.MESH)
                cp.start()
                copies.append((cp, a, peer, d))
        for cp, a, peer, d in copies:
            cp.wait_send()
            pltpu.make_async_remote_copy(
                src_ref=src(a, peer), dst_ref=recv[a].at[peer], send_sem=send_sems.at[a, d],
                recv_sem=recv_sems.at[a, d], device_id=(x, y, c), device_id_type=pl.DeviceIdType.MESH).wait_recv()
        for cp in local:
            cp.wait()

    any_spec = pl.BlockSpec(memory_space=pl.ANY)
    return pl.pallas_call(
        body, name=name, in_specs=[any_spec] * n, out_specs=[any_spec] * n,
        out_shape=[jax.ShapeDtypeStruct(((N_DEV,) + a.shape) if gather[i] else a.shape, a.dtype)
                   for i, a in enumerate(arrays)],
        scratch_shapes=[pltpu.SemaphoreType.DMA((n, N_DEV)), pltpu.SemaphoreType.DMA((n, N_DEV)),
                        pltpu.SemaphoreType.DMA((n,))],
    )(*arrays)


REDUCE_BLOCK_ELEMS = 64 * 1024


def _reduce_adamw(name, recv, w, m, v):
    rows, cols = w.shape
    tr = rows
    while tr % 16 == 0 and tr * (-(-cols // 128) * 128) > REDUCE_BLOCK_ELEMS:
        tr //= 2
    c1, c2 = 1.0 - ADAM_B1 ** ADAM_STEP, 1.0 - ADAM_B2 ** ADAM_STEP

    def body(r_ref, w_ref, m_ref, v_ref, g_ref, d_ref, nm_ref, nv_ref):
        g = r_ref[0]
        for j in range(1, N_DEV):
            g = g + r_ref[j]
        m_new = ADAM_B1 * m_ref[...] + (1.0 - ADAM_B1) * g
        v_new = ADAM_B2 * v_ref[...] + (1.0 - ADAM_B2) * jnp.square(g)
        d_ref[...] = -ADAM_LR * ((m_new / c1) / (jnp.sqrt(v_new / c2) + ADAM_EPS) + ADAM_WD * w_ref[...])
        g_ref[...], nm_ref[...], nv_ref[...] = g, m_new, v_new

    flat = pl.BlockSpec((tr, cols), lambda i: (i, 0))
    return pl.pallas_call(
        body, name=name, grid=(rows // tr,),
        in_specs=[pl.BlockSpec((N_DEV, tr, cols), lambda i: (0, i, 0)), flat, flat, flat], out_specs=[flat] * 4,
        out_shape=[jax.ShapeDtypeStruct((rows, cols), F32)] * 4, compiler_params=_params(("parallel",)),
    )(recv, w, m, v)


PACK_ROWS = 512


def _pack(cols):
    flat = jnp.concatenate(cols, axis=-1)
    tile = PACK_ROWS * 128
    flat = _pad_to(flat, flat.ndim - 1, -(-flat.shape[-1] // tile) * tile)
    return flat.reshape(flat.shape[:-1] + (-1, 128))


def _unpack(packed, shapes, lead=()):
    flat = packed.reshape(lead + (-1,))
    out, off = [], 0
    for shp in shapes:
        n = math.prod(shp)
        out.append(flat[..., off:off + n].reshape(lead + tuple(shp)))
        off += n
    return out


def _to_shards(name, full):
    ax = SHARD_AXIS[name]
    shp = full.shape
    return jnp.moveaxis(full.reshape(shp[:ax] + (N_DEV, shp[ax] // N_DEV) + shp[ax + 1:]), ax, 0)


def _from_shards(name, shards):
    ax = SHARD_AXIS[name]
    a = jnp.moveaxis(shards, 0, ax)
    return a.reshape(a.shape[:ax] + (a.shape[ax] * a.shape[ax + 1],) + a.shape[ax + 2:])


def kernel(x, positions, norm1_g, w_in, sgu_norm_g, w_spatial, b_spatial, q_lat_norm_g, w_q_up, kv_lat_norm_g, w_kv_up, q_norm_g, k_norm_g, conv_w, a_log, dt_bias, o_norm_g, w_branch, w_out, norm2_g, w_ff1, w_ff2, loss_target, m_norm1_g, m_w_in, m_sgu_norm_g, m_w_spatial, m_b_spatial, m_q_lat_norm_g, m_w_q_up, m_kv_lat_norm_g, m_w_kv_up, m_q_norm_g, m_k_norm_g, m_conv_w, m_a_log, m_dt_bias, m_o_norm_g, m_w_branch, m_w_out, m_norm2_g, m_w_ff1, m_w_ff2, v_norm1_g, v_w_in, v_sgu_norm_g, v_w_spatial, v_b_spatial, v_q_lat_norm_g, v_w_q_up, v_kv_lat_norm_g, v_w_kv_up, v_q_norm_g, v_k_norm_g, v_conv_w, v_a_log, v_dt_bias, v_o_norm_g, v_w_branch, v_w_out, v_norm2_g, v_w_ff1, v_w_ff2):
    args = locals()
    local_w = {n: args[n] for n in WEIGHTS}
    state = (local_w, {n: args["m_" + n] for n in WEIGHTS}, {n: args["v_" + n] for n in WEIGHTS})
    wire = [local_w[n] if n in EXACT_GATHER else local_w[n].astype(BF16) for n in SHARDED]
    full = dict(local_w)
    for n, part in zip(SHARDED, _exchange("gather_weights", wire, [True] * len(wire))):
        full[n] = part if n == "w_in" else _from_shards(n, part)
    sq, grad_x, grads = _local_step(x[0], positions.reshape(-1, 1), full, loss_target[0])
    loss = lax.psum(sq[0, 0] * (0.5 / D_MODEL), ("x", "y", "c"))
    send = [grads[n] if n == "w_in" else _to_shards(n, grads[n]) for n in SHARDED]
    rep = _pack([grads[n].reshape(-1) for n in REPLICATED])
    recv = _exchange("exchange_grads", send + [rep], [False] * len(send) + [True])
    results = {}
    for n, r in zip(SHARDED, recv):
        shp = local_w[n].shape
        flat = (math.prod(shp[:-1]), shp[-1])
        outs = _reduce_adamw("adamw_" + n, r.reshape((N_DEV,) + flat), *[src[n].reshape(flat) for src in state])
        results[n] = [o.reshape(shp) for o in outs]
    outs = _reduce_adamw("adamw_replicated", recv[-1],
                         *[_pack([src[n].reshape(-1) for n in REPLICATED]) for src in state])
    rep_shapes = [local_w[n].shape for n in REPLICATED]
    for n, vals in zip(REPLICATED, zip(*[_unpack(o, rep_shapes) for o in outs])):
        results[n] = vals
    return (loss, grad_x[None], *[results[n][k] for k in range(4) for n in WEIGHTS])
```

```python
import functools
import math

import jax
import jax.numpy as jnp
from jax import lax
from jax.experimental import pallas as pl
from jax.experimental.pallas import tpu as pltpu

F32, BF16 = jnp.float32, jnp.bfloat16
HI = lax.Precision.HIGHEST

N_DEV = 8
D_MODEL = 1024
DEPTH = 2
N_HEADS = 8
HEAD_PAD = 128
A_WIDTH = 512
B_NOPE, B_ROPE, B_VDIM = 64, 32, 64
B_QK = B_NOPE + B_ROPE
B_Q_LORA, B_KV_LORA = 384, 256
ROPE_BASE = 10000.0
C_DK = 64
C_CHUNK = 64
DELTA_SUB = 2
DELTA_BLOCK = 16
C_QKV = 1536
C_Z = 512
SGU_CHUNK = 128
D_FF = 4096
EPS = 1e-6
ADAM_LR, ADAM_B1, ADAM_B2, ADAM_EPS, ADAM_WD, ADAM_STEP = 0.001, 0.9, 0.999, 1e-08, 0.01, 10
O_QLAT, O_KVLAT, O_KROPE, O_CQKV, O_GATES, D_IN = 1024, 1408, 1664, 1696, 3760, 6832
W_B, W_C = 768, 2176
VMEM_LIMIT = 56 * 2 ** 20
NEG = -1e30

SHARDED = ("w_in", "w_q_up", "w_kv_up", "conv_w", "w_branch", "w_out", "w_ff1", "w_ff2")
EXACT_GATHER = ("conv_w",)
SHARD_AXIS = {"w_in": 2, "w_q_up": 2, "w_kv_up": 2, "conv_w": 2, "w_branch": 3, "w_out": 1, "w_ff1": 2, "w_ff2": 1}
REPLICATED = ("norm1_g", "sgu_norm_g", "w_spatial", "b_spatial", "q_lat_norm_g", "kv_lat_norm_g", "q_norm_g",
              "k_norm_g", "a_log", "dt_bias", "o_norm_g", "norm2_g")
WEIGHTS = ("norm1_g", "w_in", "sgu_norm_g", "w_spatial", "b_spatial", "q_lat_norm_g", "w_q_up", "kv_lat_norm_g",
           "w_kv_up", "q_norm_g", "k_norm_g", "conv_w", "a_log", "dt_bias", "o_norm_g", "w_branch", "w_out",
           "norm2_g", "w_ff1", "w_ff2")


def _params(sem, vmem=VMEM_LIMIT):
    return pltpu.CompilerParams(dimension_semantics=sem, vmem_limit_bytes=vmem)


_FORMS = {"nn": (1, 0), "nt": (1, 1), "tn": (0, 0)}


def _dot(form, a, b, batch, prec=None):
    ca, cb = _FORMS[form]
    o = 1 if batch else 0
    bd = ((0,), (0,)) if batch else ((), ())
    return lax.dot_general(a, b, (((ca + o,), (cb + o,)), bd), precision=prec, preferred_element_type=F32)


def _mm_raw(form, batch, a, b):
    return _dot(form, a.astype(BF16), b.astype(BF16), batch)


@functools.partial(jax.custom_vjp, nondiff_argnums=(0, 1))
def _mm(form, batch, a, b):
    return _mm_raw(form, batch, a, b)


def _mm_fwd(form, batch, a, b):
    return _mm_raw(form, batch, a, b), (a, b)


def _mm_bwd(form, batch, res, g):
    a, b = res
    if form == "nn":
        da, db = _mm_raw("nt", batch, g, b), _mm_raw("tn", batch, a, g)
    elif form == "nt":
        da, db = _mm_raw("nn", batch, g, b), _mm_raw("tn", batch, g, a)
    else:
        da, db = _mm_raw("nt", batch, b, g), _mm_raw("nn", batch, a, g)
    return da.astype(a.dtype), db.astype(b.dtype)


_mm.defvjp(_mm_fwd, _mm_bwd)


def _mmh(form, a, b, batch=False):
    return _dot(form, a, b, batch, HI)


@functools.partial(jax.custom_vjp, nondiff_argnums=(1, 2))
def _roll(x, shift, axis):
    return pltpu.roll(x, shift % x.shape[axis], axis)


def _roll_fwd(x, shift, axis):
    return _roll(x, shift, axis), None


def _roll_bwd(shift, axis, _, g):
    return (_roll(g, -shift, axis),)


_roll.defvjp(_roll_fwd, _roll_bwd)


@jax.custom_vjp
def _tile_heads(x):
    return jnp.concatenate([x] * N_HEADS, axis=1)


def _tile_heads_fwd(x):
    return _tile_heads(x), None


def _tile_heads_bwd(_, g):
    w = g.shape[1] // N_HEADS
    acc = g[:, :w]
    for h in range(1, N_HEADS):
        acc = acc + g[:, h * w:(h + 1) * w]
    return (acc,)


_tile_heads.defvjp(_tile_heads_fwd, _tile_heads_bwd)


def _iota(shape, dim):
    return lax.broadcasted_iota(jnp.int32, shape, dim)


def _head_indicator_t(width, per_head):
    return (_iota((N_HEADS, width), 1) // per_head == _iota((N_HEADS, width), 0)).astype(F32)


def _rms(x, g):
    return x * lax.rsqrt(jnp.mean(x * x, axis=-1, keepdims=True) + EPS) * g


def _gelu(x):
    return 0.5 * x * (1.0 + lax.erf(x * (2.0 ** -0.5)))


def _head_sums(x, per_head):
    blocks = []
    for b in range(x.shape[1] // 128):
        blk = x[:, b * 128:(b + 1) * 128]
        if per_head == 128:
            blocks.append(jnp.broadcast_to(jnp.sum(blk, axis=1, keepdims=True), blk.shape))
        else:
            low = _iota((1, 128), 1) < per_head
            s_low = jnp.sum(jnp.where(low, blk, 0.0), axis=1, keepdims=True)
            s_high = jnp.sum(jnp.where(low, 0.0, blk), axis=1, keepdims=True)
            blocks.append(jnp.where(low, s_low, s_high))
    return jnp.concatenate(blocks, axis=1)


def _head_rms(x, g_full, per_head, n_real):
    return x * lax.rsqrt(_head_sums(x * x, per_head) * (1.0 / n_real) + EPS) * g_full


def _rope(x, cos_t, sin_hi, sin_lo):
    half = B_ROPE // 2
    return (x * _tile_heads(cos_t) + _roll(x, half, 1) * _tile_heads(sin_hi)
            + _roll(x, -half, 1) * _tile_heads(sin_lo))


def _tok_spec(t, width, col):
    return pl.BlockSpec((t, width), lambda i, c=col: (i, c))


def _par_spec(shape):
    nd = len(shape)
    return pl.BlockSpec(shape, lambda i: (0,) * nd)


def _local_fwd(name, f, toks, pars, outs, t):
    s = toks[0][0].shape[0]
    nt, npar = len(toks), len(pars)

    def body(*refs):
        vals = [r[...] for r in refs[:nt + npar]]
        for r, o in zip(refs[nt + npar:], f(*vals)):
            r[...] = o.astype(r.dtype)

    return pl.pallas_call(
        body, name=name, grid=(s // t,),
        in_specs=[_tok_spec(t, w, c) for _, w, c in toks] + [_par_spec(p.shape) for p in pars],
        out_specs=[_tok_spec(t, w, 0) for w, _ in outs],
        out_shape=[jax.ShapeDtypeStruct((s, w), dt) for w, dt in outs],
        compiler_params=_params(("parallel",)),
    )(*[a for a, _, _ in toks], *pars)


def _local_bwd(name, f, toks, pars, cots, t, tok_diff, par_diff, grad_dtypes=None):
    s = toks[0][0].shape[0]
    nt, npar, nc = len(toks), len(pars), len(cots)
    dt_idx = [k for k in range(nt) if tok_diff[k]]
    dp_idx = [k for k in range(npar) if par_diff[k]]
    grad_dtypes = grad_dtypes or [F32] * len(dt_idx)

    def body(*refs):
        i = pl.program_id(0)
        tv = [r[...] for r in refs[:nt]]
        pv = [r[...] for r in refs[nt:nt + npar]]
        cv = [r[...].astype(F32) for r in refs[nt + npar:nt + npar + nc]]
        out_refs = refs[nt + npar + nc:]

        def g(*d):
            tt, pp = list(tv), list(pv)
            for k, val in zip(dt_idx, d[:len(dt_idx)]):
                tt[k] = val
            for k, val in zip(dp_idx, d[len(dt_idx):]):
                pp[k] = val
            return tuple(o.astype(F32) for o in f(*tt, *pp))

        _, vjp = jax.vjp(g, *[tv[k] for k in dt_idx], *[pv[k] for k in dp_idx])
        grads = vjp(tuple(cv))
        for r, gr in zip(out_refs[:len(dt_idx)], grads[:len(dt_idx)]):
            r[...] = gr.astype(r.dtype)
        par_refs = out_refs[len(dt_idx):]

        @pl.when(i == 0)
        def _():
            for r in par_refs:
                r[...] = jnp.zeros(r.shape, r.dtype)

        for r, gr in zip(par_refs, grads[len(dt_idx):]):
            r[...] += gr.astype(F32)

    res = pl.pallas_call(
        body, name=name, grid=(s // t,),
        in_specs=([_tok_spec(t, w, c) for _, w, c in toks] + [_par_spec(p.shape) for p in pars]
                  + [_tok_spec(t, c.shape[1], 0) for c in cots]),
        out_specs=([_tok_spec(t, toks[k][1], 0) for k in dt_idx] + [_par_spec(pars[k].shape) for k in dp_idx]),
        out_shape=([jax.ShapeDtypeStruct((s, toks[k][1]), dt) for k, dt in zip(dt_idx, grad_dtypes)]
                   + [jax.ShapeDtypeStruct(pars[k].shape, F32) for k in dp_idx]),
        compiler_params=_params(("arbitrary",)),
    )(*[a for a, _, _ in toks], *pars, *cots)
    return res[:len(dt_idx)], res[len(dt_idx):]


def _f_norm(x, g):
    return (_rms(x, g),)


def _f_norm_res(x, g):
    return _rms(x, g), x


def _f_rope_tables(pos, inv_row):
    ang = pos.astype(F32) * inv_row
    lane = _iota(ang.shape, 1)
    sn = jnp.sin(ang)
    half = B_ROPE // 2
    sin_hi = jnp.where((lane >= B_NOPE + half) & (lane < B_QK), sn, 0.0)
    sin_lo = jnp.where((lane >= B_NOPE) & (lane < B_NOPE + half), -sn, 0.0)
    return jnp.cos(ang), sin_hi, sin_lo


def _f_sgu(p_a, g, w_s, b_s):
    u = _gelu(p_a[:, :A_WIDTH])
    v = _rms(_gelu(p_a[:, A_WIDTH:]), g)
    tril = _iota((SGU_CHUNK, SGU_CHUNK), 1) <= _iota((SGU_CHUNK, SGU_CHUNK), 0)
    w_cat = jnp.concatenate([jnp.where(tril, w_s[gi], 0.0) for gi in range(N_HEADS)], axis=1)
    group = _iota((1, A_WIDTH), 1) // (A_WIDTH // N_HEADS)
    v_stack = jnp.concatenate([jnp.where(group == gi, v, 0.0) for gi in range(N_HEADS)], axis=0)
    bias = _mmh("tn", b_s, _head_indicator_t(A_WIDTH, A_WIDTH // N_HEADS))
    return (u * (_mm("nn", False, w_cat, v_stack) + bias),)


def _f_mla_prep(p_b, cos_t, sin_hi, sin_lo, q_lat_g, kv_lat_g, wq, wk, wv, qn_g, kn_g):
    kv_lat, k_rope, q_lat = p_b[:, :B_KV_LORA], p_b[:, B_KV_LORA:B_KV_LORA + HEAD_PAD], p_b[:, B_KV_LORA + HEAD_PAD:]
    q = _mm("nn", False, _rms(q_lat, q_lat_g), wq)
    q = _rope(_head_rms(q, _tile_heads(qn_g), HEAD_PAD, B_QK), cos_t, sin_hi, sin_lo)
    kvn = _rms(kv_lat, kv_lat_g)
    k = _mm("nn", False, kvn, wk) + _tile_heads(k_rope)
    k = _rope(_head_rms(k, _tile_heads(kn_g), HEAD_PAD, B_QK), cos_t, sin_hi, sin_lo)
    return q, k, _mm("nn", False, kvn, wv)


def _f_gdn_pre(conv_pre, ba, a_log_row, dt_row):
    qkv = jax.nn.silu(conv_pre)

    def l2(x):
        return x * lax.rsqrt(_head_sums(x * x, C_DK) + EPS)

    lane = _iota(ba.shape, 1)
    g = -jnp.exp(a_log_row) * jax.nn.softplus(ba + dt_row)
    gb = jnp.where(lane < N_HEADS, jax.nn.sigmoid(ba), jnp.where(lane < 2 * N_HEADS, g, 0.0))
    return l2(qkv[:, :A_WIDTH]), l2(qkv[:, A_WIDTH:2 * A_WIDTH]), qkv[:, 2 * A_WIDTH:], gb


def _f_gdn_post(o, c_z, o_g):
    place = (_iota((C_DK, A_WIDTH), 1) % C_DK == _iota((C_DK, A_WIDTH), 0)).astype(F32)
    return (_head_rms(o, _mmh("nn", o_g, place), C_DK, C_DK) * jax.nn.silu(c_z),)


def _f_merge(p_g, y0, y1, y2):
    d = D_MODEL
    return (jax.nn.sigmoid(p_g[:, :d]) * y0 + jax.nn.sigmoid(p_g[:, d:2 * d]) * y1
            + jax.nn.sigmoid(p_g[:, 2 * d:]) * y2,)


def _pick(n, whole_up_to, candidates):
    if n <= whole_up_to:
        return n
    for c in candidates:
        if n % c == 0:
            return c
    return n


def _matmul(name, a, w, add=None, out_dtype=F32, tm=512, extras=(), epilogue=None, out_dtypes=None):
    m, k = a.shape
    n = w.shape[1]
    tm = min(tm, m)
    tn = _pick(n, 2304, (2048, 1536, 1024, 512))
    if add is not None:
        extras, epilogue = (add,), lambda r, x: (r + x,)
    if epilogue is None:
        epilogue = lambda r: (r,)
    out_dtypes = out_dtypes or (out_dtype,)
    n_ex, n_out = len(extras), len(out_dtypes)

    def body(*refs):
        a_ref, w_ref = refs[0], refs[1]
        ex_refs, o_refs = refs[2:2 + n_ex], refs[2 + n_ex:2 + n_ex + n_out]
        total = jnp.dot(a_ref[...].astype(BF16), w_ref[...].astype(BF16), preferred_element_type=F32)
        for o_ref, r in zip(o_refs, epilogue(total, *[e[...] for e in ex_refs])):
            o_ref[...] = r.astype(o_ref.dtype)

    tile = pl.BlockSpec((tm, tn), lambda j, i: (i, j))
    res = pl.pallas_call(
        body, name=name, grid=(n // tn, m // tm),
        in_specs=[pl.BlockSpec((tm, k), lambda j, i: (i, 0)), pl.BlockSpec((k, tn), lambda j, i: (0, j))] + [tile] * n_ex,
        out_specs=[tile] * n_out, out_shape=[jax.ShapeDtypeStruct((m, n), dt) for dt in out_dtypes],
        compiler_params=_params(("parallel", "parallel")),
    )(a, w, *extras)
    return res[0] if n_out == 1 else res


def _matmul_tn(name, a, b, a_col=None, tm=512):
    m = a.shape[0]
    k, acol = (a.shape[1], 0) if a_col is None else a_col
    n = b.shape[1]
    tm = min(tm, m)
    tk = _pick(k, 1536, (1024, 512))
    tn = _pick(n, 2304, (1024, 512))
    nm = m // tm

    def body(a_ref, b_ref, o_ref):
        mm = pl.program_id(2)
        part = lax.dot_general(a_ref[...].astype(BF16), b_ref[...].astype(BF16), (((0,), (0,)), ((), ())),
                               preferred_element_type=F32)

        @pl.when(mm == 0)
        def _():
            o_ref[...] = part

        @pl.when(mm > 0)
        def _():
            o_ref[...] += part

    kb = k // tk
    return pl.pallas_call(
        body, name=name, grid=(kb, n // tn, nm),
        in_specs=[pl.BlockSpec((tm, tk), lambda i, j, mm: (mm, acol * kb + i)),
                  pl.BlockSpec((tm, tn), lambda i, j, mm: (mm, j))],
        out_specs=pl.BlockSpec((tk, tn), lambda i, j, mm: (i, j)),
        out_shape=jax.ShapeDtypeStruct((k, n), F32),
        compiler_params=_params(("parallel", "parallel", "arbitrary")),
    )(a, b)


def _shift_down(x, prev, s):
    rolled = pltpu.roll(x, s, 0)
    pr = pltpu.roll(prev, s, 0)
    head = jnp.where(_iota((8, 1), 0) < s, pr, rolled[:8])
    return jnp.concatenate([head, rolled[8:]], axis=0)


def _shift_up(x, nxt, s):
    t = x.shape[0]
    rolled = pltpu.roll(x, t - s, 0)
    nr = pltpu.roll(nxt, 8 - s, 0)
    tail = jnp.where(_iota((8, 1), 0) >= 8 - s, nr, rolled[t - 8:])
    return jnp.concatenate([rolled[:t - 8], tail], axis=0)


def _conv_fwd(p_c, w8, t):
    s = p_c.shape[0]
    t = min(t, s)
    r = t // 8

    def body(x_ref, prev_ref, w_ref, o_ref):
        i = pl.program_id(0)
        x = x_ref[...]
        prev = jnp.where(i == 0, 0.0, prev_ref[...])
        acc = w_ref[3:4, :] * x
        for sh in range(1, 4):
            acc = acc + w_ref[3 - sh:4 - sh, :] * _shift_down(x, prev, sh)
        o_ref[...] = acc

    return pl.pallas_call(
        body, name="conv_fwd", grid=(s // t,),
        in_specs=[pl.BlockSpec((t, C_QKV), lambda i: (i, 0)),
                  pl.BlockSpec((8, C_QKV), lambda i: (jnp.maximum(i * r - 1, 0), 0)),
                  pl.BlockSpec((8, C_QKV), lambda i: (0, 0))],
        out_specs=pl.BlockSpec((t, C_QKV), lambda i: (i, 0)),
        out_shape=jax.ShapeDtypeStruct((s, C_QKV), F32),
        compiler_params=_params(("parallel",)),
    )(p_c, p_c, w8)


def _conv_bwd(p_c, dy, w8, t):
    s = p_c.shape[0]
    t = min(t, s)
    r = t // 8
    n = s // t

    def body(x_ref, prev_ref, dy_ref, next_ref, w_ref, dx_ref, dw_ref):
        i = pl.program_id(0)
        x, g = x_ref[...], dy_ref[...]
        prev = jnp.where(i == 0, 0.0, prev_ref[...])
        nxt = jnp.where(i == n - 1, 0.0, next_ref[...])

        @pl.when(i == 0)
        def _():
            dw_ref[...] = jnp.zeros(dw_ref.shape, F32)

        dx = w_ref[3:4, :] * g
        dw_ref[3:4, :] += jnp.sum(g * x, axis=0, keepdims=True)
        for sh in range(1, 4):
            dx = dx + w_ref[3 - sh:4 - sh, :] * _shift_up(g, nxt, sh)
            dw_ref[3 - sh:4 - sh, :] += jnp.sum(g * _shift_down(x, prev, sh), axis=0, keepdims=True)
        dx_ref[...] = dx.astype(dx_ref.dtype)

    return pl.pallas_call(
        body, name="conv_bwd", grid=(n,),
        in_specs=[pl.BlockSpec((t, C_QKV), lambda i: (i, 0)),
                  pl.BlockSpec((8, C_QKV), lambda i: (jnp.maximum(i * r - 1, 0), 0)),
                  pl.BlockSpec((t, C_QKV), lambda i: (i, 0)),
                  pl.BlockSpec((8, C_QKV), lambda i: (jnp.minimum((i + 1) * r, s // 8 - 1), 0)),
                  pl.BlockSpec((8, C_QKV), lambda i: (0, 0))],
        out_specs=[pl.BlockSpec((t, C_QKV), lambda i: (i, 0)), pl.BlockSpec((8, C_QKV), lambda i: (0, 0))],
        out_shape=[jax.ShapeDtypeStruct((s, C_QKV), BF16), jax.ShapeDtypeStruct((8, C_QKV), F32)],
        compiler_params=_params(("arbitrary",)),
    )(p_c, p_c, dy, dy, w8)


def _delta_chunks(state, q, k, v, g, beta):
    pre = [_delta_pre(q[:, i * C_CHUNK:(i + 1) * C_CHUNK], k[:, i * C_CHUNK:(i + 1) * C_CHUNK],
                      v[:, i * C_CHUNK:(i + 1) * C_CHUNK], g[i], beta[i]) for i in range(DELTA_SUB)]
    outs = []
    for qg, intra, u, w, k_dec, dec in pre:
        v_new = u - _mm("nn", True, w, state)
        outs.append(_mm("nn", True, qg, state) + _mm("nn", True, intra, v_new))
        state = state * dec + _mm("tn", True, k_dec, v_new)
    return state, jnp.concatenate(outs, axis=1)


def _delta_pre(q, k, v, g, beta):
    c = C_CHUNK
    row, col = _iota((c, c), 0), _iota((c, c), 1)
    tril, strict = col <= row, col < row
    gc = _mmh("nn", g, (row <= col).astype(F32))
    g_last = jnp.sum(g, axis=1, keepdims=True)
    qs = q * (C_DK ** -0.5)
    decay = jnp.exp(jnp.where(tril, gc[:, :, None] - gc[:, None, :], NEG))
    k_beta, v_beta = k * beta[:, :, None], v * beta[:, :, None]
    x = -jnp.where(strict, _mm("nt", True, k_beta, k) * decay, 0.0)
    xd = jnp.where(row // DELTA_BLOCK == col // DELTA_BLOCK, x, 0.0)
    powers = [xd]
    for _ in range(3):
        powers.append(_mm("nn", True, powers[-1], powers[-1]))
    sol = jnp.concatenate([x - xd, v_beta, k_beta * jnp.exp(gc)[:, :, None]], axis=2)
    for p in reversed(powers):
        sol = sol + _mm("nn", True, p, sol)
    y, sol = sol[:, :, :c], sol[:, :, c:]
    sol = sol + _mm("nn", True, _mm("nn", True, y, y), sol)
    sol = sol + _mm("nn", True, y, sol)
    u, w = sol[:, :, :C_DK], sol[:, :, C_DK:]
    intra = jnp.where(tril, _mm("nt", True, qs, k) * decay, 0.0)
    return (qs * jnp.exp(gc)[:, :, None], intra, u, w, k * jnp.exp(g_last - gc)[:, :, None],
            jnp.exp(g_last)[:, :, None])


def _delta_specs(n, rev):
    def at(i):
        return n - 1 - i if rev else i
    tok = pl.BlockSpec((N_HEADS, DELTA_SUB * C_CHUNK, C_DK), lambda i: (0, at(i), 0))
    vec = pl.BlockSpec((DELTA_SUB, N_HEADS, C_CHUNK), lambda i: (at(i), 0, 0))
    st = pl.BlockSpec((1, N_HEADS, C_DK, C_DK), lambda i: (at(i), 0, 0, 0))
    return tok, vec, st


def _delta_fwd(q, k, v, g, beta):
    s = q.shape[1]
    n = s // (DELTA_SUB * C_CHUNK)
    tok, vec, st = _delta_specs(n, False)

    def body(q_ref, k_ref, v_ref, g_ref, b_ref, o_ref, st_ref, state):
        @pl.when(pl.program_id(0) == 0)
        def _():
            state[...] = jnp.zeros(state.shape, F32)

        cur = state[...]
        st_ref[0] = cur
        new, o = _delta_chunks(cur, q_ref[...], k_ref[...], v_ref[...], g_ref[...], b_ref[...])
        o_ref[...] = o
        state[...] = new

    return pl.pallas_call(
        body, name="delta_fwd", grid=(n,), in_specs=[tok, tok, tok, vec, vec], out_specs=[tok, st],
        out_shape=[jax.ShapeDtypeStruct((N_HEADS, s, C_DK), F32), jax.ShapeDtypeStruct((n, N_HEADS, C_DK, C_DK), F32)],
        scratch_shapes=[pltpu.VMEM((N_HEADS, C_DK, C_DK), F32)],
        compiler_params=_params(("arbitrary",)),
    )(q, k, v, g, beta)


def _delta_bwd(q, k, v, g, beta, states, do):
    s = q.shape[1]
    n = s // (DELTA_SUB * C_CHUNK)
    tok, vec, st = _delta_specs(n, True)

    def body(q_ref, k_ref, v_ref, g_ref, b_ref, st_ref, do_ref, dq_ref, dk_ref, dv_ref, dg_ref, db_ref, dstate):
        @pl.when(pl.program_id(0) == 0)
        def _():
            dstate[...] = jnp.zeros(dstate.shape, F32)

        _, vjp = jax.vjp(_delta_chunks, st_ref[0], q_ref[...], k_ref[...], v_ref[...], g_ref[...], b_ref[...])
        dst, dq, dk, dv, dg, db = vjp((dstate[...], do_ref[...]))
        dq_ref[...], dk_ref[...], dv_ref[...] = dq, dk, dv
        dg_ref[...], db_ref[...] = dg, db
        dstate[...] = dst

    tok_shape = jax.ShapeDtypeStruct((N_HEADS, s, C_DK), F32)
    vec_shape = jax.ShapeDtypeStruct((n * DELTA_SUB, N_HEADS, C_CHUNK), F32)
    return pl.pallas_call(
        body, name="delta_bwd", grid=(n,), in_specs=[tok, tok, tok, vec, vec, st, tok],
        out_specs=[tok, tok, tok, vec, vec], out_shape=[tok_shape, tok_shape, tok_shape, vec_shape, vec_shape],
        scratch_shapes=[pltpu.VMEM((N_HEADS, C_DK, C_DK), F32)],
        compiler_params=_params(("arbitrary",)),
    )(q, k, v, g, beta, states, do)


ATT_SCALE = B_QK ** -0.5


SCORE_SCALE_LOG2 = ATT_SCALE * math.log2(math.e)


def _lanes(x, n):
    return x if n == 1 else jnp.concatenate([x] * n, axis=1)


def _scores(a_ref, b_ref):
    return lax.dot_general(a_ref[...], b_ref[...], (((1,), (1,)), ((), ())), preferred_element_type=F32) * SCORE_SCALE_LOG2


def _flash_fwd(q, k, v, tb):
    s = q.shape[0]
    tb = min(tb, s)
    nb = s // tb
    nrep = tb // 128

    def body(q_ref, k_ref, v_ref, o_ref, lse_ref, m_s, l_s, acc):
        i, j = pl.program_id(1), pl.program_id(2)

        @pl.when(j == 0)
        def _():
            m_s[...] = jnp.full(m_s.shape, -jnp.inf, F32)
            l_s[...] = jnp.zeros(l_s.shape, F32)
            acc[...] = jnp.zeros(acc.shape, F32)

        def step(masked):
            sc = _scores(q_ref, k_ref)
            if masked:
                sc = jnp.where(_iota((tb, tb), 1) <= _iota((tb, tb), 0), sc, -jnp.inf)
            m_prev = m_s[...]
            m_new = jnp.maximum(m_prev, jnp.max(sc, axis=1, keepdims=True))
            alpha = jnp.exp2(m_prev - m_new)
            p = jnp.exp2(sc - _lanes(m_new, nrep))
            l_s[...] = alpha * l_s[...] + jnp.sum(p, axis=1, keepdims=True)
            acc[...] = alpha * acc[...] + jnp.dot(p.astype(BF16), v_ref[...], preferred_element_type=F32)
            m_s[...] = m_new

        @pl.when(j < i)
        def _():
            step(False)

        @pl.when(j == i)
        def _():
            step(True)
            o_ref[...] = acc[...] / l_s[...]
            lse_ref[...] = m_s[...] + jnp.log2(l_s[...])

    qs = pl.BlockSpec((tb, HEAD_PAD), lambda h, i, j: (i, h))
    ks = pl.BlockSpec((tb, HEAD_PAD), lambda h, i, j: (jnp.minimum(j, i), h))
    shape = jax.ShapeDtypeStruct((s, N_HEADS * HEAD_PAD), F32)
    return pl.pallas_call(
        body, name="flash_fwd", grid=(N_HEADS, nb, nb), in_specs=[qs, ks, ks],
        out_specs=[qs, qs], out_shape=[shape, shape],
        scratch_shapes=[pltpu.VMEM((tb, 128), F32), pltpu.VMEM((tb, 128), F32), pltpu.VMEM((tb, HEAD_PAD), F32)],
        compiler_params=_params(("parallel", "parallel", "arbitrary")),
    )(q, k, v)


def _f_attn_delta(o, do):
    return (_head_sums(o * do, HEAD_PAD),)


def _row_stats(rep):
    return rep[:, ::HEAD_PAD].T.reshape(N_HEADS, 1, -1)


def _flash_bwd(q, k, v, do, lse_row, delta_row, tb):
    s = q.shape[0]
    tb = min(tb, s)
    nb = s // tb

    def body(q_ref, k_ref, v_ref, do_ref, lse_ref, dl_ref, dq_ref, dk_ref, dv_ref, dk_acc, dv_acc):
        j, i = pl.program_id(1), pl.program_id(2)

        @pl.when((i == 0) & (j == 0))
        def _():
            dq_ref[...] = jnp.zeros(dq_ref.shape, F32)

        @pl.when(i == 0)
        def _():
            dk_acc[...] = jnp.zeros(dk_acc.shape, F32)
            dv_acc[...] = jnp.zeros(dv_acc.shape, F32)

        def step(masked):
            st = _scores(k_ref, q_ref)
            if masked:
                st = jnp.where(_iota((tb, tb), 0) <= _iota((tb, tb), 1), st, -jnp.inf)
            do = do_ref[...].astype(BF16)
            pt = jnp.exp2(st - lse_ref[0])
            dpt = lax.dot_general(v_ref[...], do, (((1,), (1,)), ((), ())), preferred_element_type=F32)
            dst = (pt * (dpt - dl_ref[0])).astype(BF16)
            dv_acc[...] += jnp.dot(pt.astype(BF16), do, preferred_element_type=F32)
            dk_acc[...] += jnp.dot(dst, q_ref[...], preferred_element_type=F32)
            rows = pl.ds(pl.multiple_of(i * tb, tb), tb)
            dq_ref[rows, :] += lax.dot_general(dst, k_ref[...], (((0,), (0,)), ((), ())),
                                               preferred_element_type=F32) * ATT_SCALE

        @pl.when(i == j)
        def _():
            step(True)

        @pl.when(i > j)
        def _():
            step(False)

        @pl.when(i == nb - 1)
        def _():
            dk_ref[...] = dk_acc[...] * ATT_SCALE
            dv_ref[...] = dv_acc[...]

    qs = pl.BlockSpec((tb, HEAD_PAD), lambda h, j, i: (jnp.maximum(i, j), h))
    ks = pl.BlockSpec((tb, HEAD_PAD), lambda h, j, i: (j, h))
    rs = pl.BlockSpec((1, 1, tb), lambda h, j, i: (h, 0, jnp.maximum(i, j)))
    shape = jax.ShapeDtypeStruct((s, N_HEADS * HEAD_PAD), F32)
    return pl.pallas_call(
        body, name="flash_bwd", grid=(N_HEADS, nb, nb), in_specs=[qs, ks, ks, qs, rs, rs],
        out_specs=[pl.BlockSpec((s, HEAD_PAD), lambda h, j, i: (0, h)), ks, ks], out_shape=[shape, shape, shape],
        scratch_shapes=[pltpu.VMEM((tb, HEAD_PAD), F32), pltpu.VMEM((tb, HEAD_PAD), F32)],
        compiler_params=_params(("parallel", "arbitrary", "arbitrary")),
    )(q, k, v, do, lse_row, delta_row)


def _loss_head(y, target, t):
    s, d = y.shape
    t = min(t, s)

    def body(y_ref, t_ref, sum_ref, dy_ref):
        @pl.when(pl.program_id(0) == 0)
        def _():
            sum_ref[...] = jnp.zeros(sum_ref.shape, F32)

        err = y_ref[...] - t_ref[...]
        dy_ref[...] = err * (1.0 / d)
        sum_ref[...] += jnp.broadcast_to(jnp.sum(err * err), sum_ref.shape)

    return pl.pallas_call(
        body, name="loss_head", grid=(s // t,),
        in_specs=[pl.BlockSpec((t, d), lambda i: (i, 0))] * 2,
        out_specs=[pl.BlockSpec((1, 128), lambda i: (0, 0)), pl.BlockSpec((t, d), lambda i: (i, 0))],
        out_shape=[jax.ShapeDtypeStruct((1, 128), F32), jax.ShapeDtypeStruct((s, d), F32)],
        compiler_params=_params(("arbitrary",)),
    )(y, target)


def _pad_to(a, axis, size):
    pad = [(0, 0)] * a.ndim
    pad[axis] = (0, size - a.shape[axis])
    return jnp.pad(a, pad)


W_IN_GROUPS = {
    "a": [(0, O_QLAT)],
    "b": [(O_KVLAT, O_KROPE), B_NOPE, (O_KROPE, O_CQKV), HEAD_PAD - B_QK, (O_QLAT, O_KVLAT)],
    "c": [(O_CQKV, O_GATES), W_C - (O_GATES - O_CQKV)],
    "g": [(O_GATES, D_IN)],
}
W_IN_SHARD = D_IN // N_DEV
W_IN_ROWS = 128


def _group_width(key):
    return sum(e if isinstance(e, int) else e[1] - e[0] for e in W_IN_GROUPS[key])


def _assemble_w_in(shards):
    depth = shards.shape[1]
    keys = list(W_IN_GROUPS)

    def body(s_ref, *o_refs):
        vals = [s_ref[k, 0] for k in range(N_DEV)]
        for key, o_ref in zip(keys, o_refs):
            parts = []
            for e in W_IN_GROUPS[key]:
                if isinstance(e, int):
                    parts.append(jnp.zeros((W_IN_ROWS, e), shards.dtype))
                    continue
                lo, hi = e
                while lo < hi:
                    k = lo // W_IN_SHARD
                    end = min(hi, (k + 1) * W_IN_SHARD)
                    parts.append(vals[k][:, lo - k * W_IN_SHARD:end - k * W_IN_SHARD])
                    lo = end
            o_ref[0] = parts[0] if len(parts) == 1 else jnp.concatenate(parts, axis=1)

    return pl.pallas_call(
        body, name="assemble_w_in", grid=(depth, D_MODEL // W_IN_ROWS),
        in_specs=[pl.BlockSpec((N_DEV, 1, W_IN_ROWS, W_IN_SHARD), lambda l, i: (0, l, i, 0))],
        out_specs=[pl.BlockSpec((1, W_IN_ROWS, _group_width(key)), lambda l, i: (l, i, 0)) for key in keys],
        out_shape=[jax.ShapeDtypeStruct((depth, D_MODEL, _group_width(key)), shards.dtype) for key in keys],
        compiler_params=_params(("parallel", "parallel")),
    )(shards)


def _split_dw_in(groups):
    keys = list(W_IN_GROUPS)
    depth = groups[0].shape[0]
    runs = []
    for gi, key in enumerate(keys):
        col = 0
        for e in W_IN_GROUPS[key]:
            if not isinstance(e, int):
                runs.append((e[0], e[1], gi, col))
            col += e if isinstance(e, int) else e[1] - e[0]
    runs.sort()

    def body(*refs):
        vals = [r[0] for r in refs[:len(keys)]]
        o_ref = refs[len(keys)]
        for k in range(N_DEV):
            lo, hi = k * W_IN_SHARD, (k + 1) * W_IN_SHARD
            parts = []
            for a, b, gi, col in runs:
                s, e = max(a, lo), min(b, hi)
                if s < e:
                    parts.append(vals[gi][:, col + s - a:col + e - a])
            o_ref[k, 0] = jnp.concatenate(parts, axis=1)

    return pl.pallas_call(
        body, name="split_dw_in", grid=(depth, D_MODEL // W_IN_ROWS),
        in_specs=[pl.BlockSpec((1, W_IN_ROWS, _group_width(key)), lambda l, i: (l, i, 0)) for key in keys],
        out_specs=pl.BlockSpec((N_DEV, 1, W_IN_ROWS, W_IN_SHARD), lambda l, i: (0, l, i, 0)),
        out_shape=jax.ShapeDtypeStruct((N_DEV, depth, D_MODEL, W_IN_SHARD), F32),
        compiler_params=_params(("parallel", "parallel")),
    )(*groups)


def _prep_layer(w, l):
    p = {}
    for key in W_IN_GROUPS:
        val = w["w_in_" + key][l].astype(BF16)
        p["w_" + key] = val
        p["wt_" + key] = val.T
    for name in ("norm1_g", "sgu_norm_g", "q_lat_norm_g", "kv_lat_norm_g", "o_norm_g", "norm2_g"):
        p[name] = w[name][l][None, :]
    p["w_spatial"], p["b_spatial"] = w["w_spatial"][l], w["b_spatial"][l]
    p["wq"] = _pad_to(w["w_q_up"][l].astype(F32).reshape(B_Q_LORA, N_HEADS, B_QK), 2, HEAD_PAD).reshape(B_Q_LORA, -1)
    kv = w["w_kv_up"][l].astype(F32).reshape(B_KV_LORA, N_HEADS, B_NOPE + B_VDIM)
    p["wk"] = _pad_to(kv[:, :, :B_NOPE], 2, HEAD_PAD).reshape(B_KV_LORA, -1)
    p["wv"] = _pad_to(kv[:, :, B_NOPE:], 2, HEAD_PAD).reshape(B_KV_LORA, -1)
    p["qn_g"] = _pad_to(w["q_norm_g"][l][None, :], 1, HEAD_PAD)
    p["kn_g"] = _pad_to(w["k_norm_g"][l][None, :], 1, HEAD_PAD)
    p["conv_w"] = _pad_to(w["conv_w"][l], 0, 8)
    row = lambda v: jnp.pad(v[None, :], ((0, 0), (N_HEADS, HEAD_PAD - 2 * N_HEADS)))
    p["a_log"], p["dt_bias"] = row(w["a_log"][l]), row(w["dt_bias"][l])
    wb = w["w_branch"][l]
    wb1 = _pad_to(wb[1].reshape(N_HEADS, B_VDIM, D_MODEL), 1, HEAD_PAD).reshape(-1, D_MODEL)
    for key, val in (("wb0", wb[0]), ("wb1", wb1), ("wb2", wb[2]), ("w_out", w["w_out"][l]),
                     ("w_ff1", w["w_ff1"][l]), ("w_ff2", w["w_ff2"][l])):
        p[key] = val.astype(BF16)
        p[key + "_t"] = val.T.astype(BF16)
    return p


def _unprep_grads(g):
    out = {"w_in_" + key: g["w_" + key] for key in W_IN_GROUPS}
    for name in ("norm1_g", "sgu_norm_g", "q_lat_norm_g", "kv_lat_norm_g", "o_norm_g", "norm2_g"):
        out[name] = g[name][0]
    out["w_spatial"], out["b_spatial"] = g["w_spatial"], g["b_spatial"]
    out["w_q_up"] = g["wq"].reshape(B_Q_LORA, N_HEADS, HEAD_PAD)[:, :, :B_QK].reshape(B_Q_LORA, -1)
    gk = g["wk"].reshape(B_KV_LORA, N_HEADS, HEAD_PAD)[:, :, :B_NOPE]
    gv = g["wv"].reshape(B_KV_LORA, N_HEADS, HEAD_PAD)[:, :, :B_VDIM]
    out["w_kv_up"] = jnp.concatenate([gk, gv], axis=2).reshape(B_KV_LORA, -1)
    out["q_norm_g"], out["k_norm_g"] = g["qn_g"][0, :B_QK], g["kn_g"][0, :B_QK]
    out["conv_w"] = g["conv_w"][:4]
    out["a_log"], out["dt_bias"] = g["a_log"][0, N_HEADS:2 * N_HEADS], g["dt_bias"][0, N_HEADS:2 * N_HEADS]
    gb1 = g["wb1"].reshape(N_HEADS, HEAD_PAD, D_MODEL)[:, :B_VDIM].reshape(-1, D_MODEL)
    out["w_branch"] = jnp.stack([g["wb0"], gb1, g["wb2"]], axis=0)
    out["w_out"], out["w_ff1"], out["w_ff2"] = g["w_out"], g["w_ff1"], g["w_ff2"]
    return out


def _heads_first(a):
    s = a.shape[0]
    return a.reshape(s, N_HEADS, C_DK).transpose(1, 0, 2)


def _heads_last(a):
    return a.transpose(1, 0, 2).reshape(a.shape[1], N_HEADS * C_DK)


def _chunk_vec(a):
    return a.reshape(-1, C_CHUNK, N_HEADS).transpose(0, 2, 1)


def _unchunk_vec(a):
    return a.transpose(0, 2, 1).reshape(-1, N_HEADS)


def _layer_fwd(x, p, tabs, t, tb):
    sv = {"x": x}
    h1, = _local_fwd("norm1", _f_norm, [(x, D_MODEL, 0)], [p["norm1_g"]], [(D_MODEL, BF16)], t)
    sv["h1"] = h1
    p_a, p_b, p_c, p_g = (_matmul("proj_" + key, h1, p["w_" + key]) for key in "abcg")
    sv.update(p_a=p_a, p_b=p_b, p_c=p_c, p_g=p_g)
    y_a, = _local_fwd("sgu", _f_sgu, [(p_a, 2 * A_WIDTH, 0)], [p["sgu_norm_g"], p["w_spatial"], p["b_spatial"]],
                      [(A_WIDTH, BF16)], SGU_CHUNK)
    q, k, v = _local_fwd("mla_prep", _f_mla_prep, [(p_b, W_B, 0)] + [(tb_, HEAD_PAD, 0) for tb_ in tabs],
                         [p["q_lat_norm_g"], p["kv_lat_norm_g"], p["wq"], p["wk"], p["wv"], p["qn_g"], p["kn_g"]],
                         [(N_HEADS * HEAD_PAD, BF16)] * 3, min(t, 256))
    o_b, lse = _flash_fwd(q, k, v, tb)
    sv.update(q=q, k=k, v=v, o_b=o_b, lse=lse)
    conv_pre = _conv_fwd(p_c, p["conv_w"], t)
    cq, ck, cv, gb = _local_fwd("gdn_pre", _f_gdn_pre, [(conv_pre, C_QKV, 0), (p_c, HEAD_PAD, (C_QKV + C_Z) // HEAD_PAD)],
                                [p["a_log"], p["dt_bias"]], [(A_WIDTH, F32)] * 3 + [(HEAD_PAD, F32)], t)
    cq, ck, cv = _heads_first(cq), _heads_first(ck), _heads_first(cv)
    beta, g = _chunk_vec(gb[:, :N_HEADS]), _chunk_vec(gb[:, N_HEADS:2 * N_HEADS])
    o_c, states = _delta_fwd(cq, ck, cv, g, beta)
    o_c = _heads_last(o_c)
    sv.update(conv_pre=conv_pre, cq=cq, ck=ck, cv=cv, beta=beta, g=g, states=states, o_c=o_c)
    y_c, = _local_fwd("gdn_post", _f_gdn_post, [(o_c, A_WIDTH, 0), (p_c, C_Z, C_QKV // C_Z)], [p["o_norm_g"]],
                      [(A_WIDTH, BF16)], t)
    y0 = _matmul("branch0", y_a, p["wb0"])
    y1 = _matmul("branch1", o_b, p["wb1"])
    y2 = _matmul("branch2", y_c, p["wb2"])
    merged, = _local_fwd("merge", _f_merge, [(p_g, 3 * D_MODEL, 0), (y0, D_MODEL, 0), (y1, D_MODEL, 0), (y2, D_MODEL, 0)],
                         [], [(D_MODEL, BF16)], t)
    x1 = _matmul("out_proj", merged, p["w_out"], add=x)
    sv.update(y_a=y_a, y_c=y_c, y0=y0, y1=y1, y2=y2, merged=merged, x1=x1)
    h2, = _local_fwd("norm2", _f_norm, [(x1, D_MODEL, 0)], [p["norm2_g"]], [(D_MODEL, BF16)], t)
    a, r = _matmul("ff1", h2, p["w_ff1"], epilogue=lambda acc: (acc, jnp.square(jnp.maximum(acc, 0.0))),
                   out_dtypes=(BF16, BF16))
    x2 = _matmul("ff2", r, p["w_ff2"], add=x1)
    sv.update(h2=h2, a=a, r=r)
    return x2, sv


def _layer_bwd(dx2, sv, p, tabs, t, tb):
    g = {}
    da = _matmul("d_ff2", dx2, p["w_ff2_t"], extras=(sv["a"],), out_dtypes=(BF16,),
                 epilogue=lambda dr, a: (dr * (2.0 * jnp.maximum(a.astype(F32), 0.0)),))
    g["w_ff2"] = _matmul_tn("dw_ff2", sv["r"], dx2)
    dh2 = _matmul("d_ff1", da, p["w_ff1_t"])
    g["w_ff1"] = _matmul_tn("dw_ff1", sv["h2"], da)
    (dx1,), (g["norm2_g"],) = _local_bwd("norm2_bwd", _f_norm_res, [(sv["x1"], D_MODEL, 0)], [p["norm2_g"]],
                                         [dh2, dx2], t, [True], [True])
    dmerged = _matmul("d_out_proj", dx1, p["w_out_t"])
    g["w_out"] = _matmul_tn("dw_out", sv["merged"], dx1)
    (dp_g, dy0, dy1, dy2), _ = _local_bwd(
        "merge_bwd", _f_merge, [(sv["p_g"], 3 * D_MODEL, 0), (sv["y0"], D_MODEL, 0), (sv["y1"], D_MODEL, 0),
                                (sv["y2"], D_MODEL, 0)], [], [dmerged], min(t, 256), [True] * 4, [], [BF16] * 4)
    dy_a = _matmul("d_branch0", dy0, p["wb0_t"])
    do_b = _matmul("d_branch1", dy1, p["wb1_t"])
    dy_c = _matmul("d_branch2", dy2, p["wb2_t"])
    g["wb0"] = _matmul_tn("dw_branch0", sv["y_a"], dy0)
    g["wb1"] = _matmul_tn("dw_branch1", sv["o_b"], dy1)
    g["wb2"] = _matmul_tn("dw_branch2", sv["y_c"], dy2)
    p_c = sv["p_c"]
    (do_c, dc_z), (g["o_norm_g"],) = _local_bwd(
        "gdn_post_bwd", _f_gdn_post, [(sv["o_c"], A_WIDTH, 0), (p_c, C_Z, C_QKV // C_Z)], [p["o_norm_g"]], [dy_c], t,
        [True, True], [True], [F32, BF16])
    dcq, dck, dcv, dg, dbeta = _delta_bwd(sv["cq"], sv["ck"], sv["cv"], sv["g"], sv["beta"], sv["states"],
                                          _heads_first(do_c))
    dgb = jnp.pad(jnp.concatenate([_unchunk_vec(dbeta), _unchunk_vec(dg)], axis=1),
                  ((0, 0), (0, HEAD_PAD - 2 * N_HEADS)))
    (dconv, dba), (g["a_log"], g["dt_bias"]) = _local_bwd(
        "gdn_pre_bwd", _f_gdn_pre, [(sv["conv_pre"], C_QKV, 0), (p_c, HEAD_PAD, (C_QKV + C_Z) // HEAD_PAD)],
        [p["a_log"], p["dt_bias"]], [_heads_last(dcq), _heads_last(dck), _heads_last(dcv), dgb], t,
        [True, True], [True, True], [F32, BF16])
    dc_qkv, g["conv_w"] = _conv_bwd(p_c, dconv, p["conv_w"], t)
    dp_c = jnp.concatenate([dc_qkv, dc_z, dba], axis=1)
    delta, = _local_fwd("attn_delta", _f_attn_delta, [(sv["o_b"], N_HEADS * HEAD_PAD, 0), (do_b, N_HEADS * HEAD_PAD, 0)], [],
                        [(N_HEADS * HEAD_PAD, F32)], t)
    dq, dk, dv = _flash_bwd(sv["q"], sv["k"], sv["v"], do_b, _row_stats(sv["lse"]), _row_stats(delta), tb)
    mla_pars = [p["q_lat_norm_g"], p["kv_lat_norm_g"], p["wq"], p["wk"], p["wv"], p["qn_g"], p["kn_g"]]
    (dp_b,), mla_g = _local_bwd(
        "mla_prep_bwd", _f_mla_prep, [(sv["p_b"], W_B, 0)] + [(tb_, HEAD_PAD, 0) for tb_ in tabs], mla_pars,
        [dq, dk, dv], min(t, 256), [True, False, False, False], [True] * 7, [BF16])
    for name, val in zip(("q_lat_norm_g", "kv_lat_norm_g", "wq", "wk", "wv", "qn_g", "kn_g"), mla_g):
        g[name] = val
    (dp_a,), (g["sgu_norm_g"], g["w_spatial"], g["b_spatial"]) = _local_bwd(
        "sgu_bwd", _f_sgu, [(sv["p_a"], 2 * A_WIDTH, 0)], [p["sgu_norm_g"], p["w_spatial"], p["b_spatial"]], [dy_a],
        SGU_CHUNK, [True], [True] * 3, [BF16])
    dh1 = None
    for key, dp in (("a", dp_a), ("b", dp_b), ("c", dp_c), ("g", dp_g)):
        dh1 = _matmul("d_proj_" + key, dp, p["wt_" + key], add=dh1)
        g["w_" + key] = _matmul_tn("dw_proj_" + key, sv["h1"], dp)
    (dx,), (g["norm1_g"],) = _local_bwd("norm1_bwd", _f_norm_res, [(sv["x"], D_MODEL, 0)], [p["norm1_g"]],
                                        [dh1, dx1], t, [True], [True])
    return dx, _unprep_grads(g)


def _local_step(x, positions, w, target, t=512, tb=1024):
    s = x.shape[0]
    t = min(t, s)
    w = dict(w)
    for key, val in zip(W_IN_GROUPS, _assemble_w_in(w["w_in"])):
        w["w_in_" + key] = val
    half = B_ROPE // 2
    inv_freq = 1.0 / (ROPE_BASE ** (jnp.arange(half, dtype=F32) / half))
    inv_row = jnp.concatenate([jnp.zeros((B_NOPE,), F32), inv_freq, inv_freq, jnp.zeros((HEAD_PAD - B_QK,), F32)])[None, :]
    tabs = _local_fwd("rope_tables", _f_rope_tables, [(positions, 1, 0)], [inv_row], [(HEAD_PAD, F32)] * 3, t)
    preps, saved = [], []
    for l in range(DEPTH):
        preps.append(_prep_layer(w, l))
        x, sv = _layer_fwd(x, preps[l], tabs, t, tb)
        saved.append(sv)
    sq, dx = _loss_head(x, target, t)
    grads = [None] * DEPTH
    for l in reversed(range(DEPTH)):
        dx, grads[l] = _layer_bwd(dx, saved[l], preps[l], tabs, t, tb)
    stacked = lambda name: jnp.stack([grads[l][name] for l in range(DEPTH)], axis=0)
    out = {name: stacked(name) for name in WEIGHTS if name != "w_in"}
    out["w_in"] = _split_dw_in([stacked("w_in_" + key) for key in W_IN_GROUPS])
    return sq, dx, out


def _exchange(name, arrays, gather):
    n = len(arrays)

    def body(*refs):
        send, recv = refs[:n], refs[n:2 * n]
        send_sems, recv_sems, local_sems = refs[2 * n:]
        x, y, c = lax.axis_index("x"), lax.axis_index("y"), lax.axis_index("c")
        me = 4 * x + 2 * y + c

        def src(a, idx):
            return send[a] if gather[a] else send[a].at[idx]

        local = [pltpu.make_async_copy(src(a, me), recv[a].at[me], local_sems.at[a]) for a in range(n)]
        for cp in local:
            cp.start()
        copies = []
        for d in range(1, N_DEV):
            px, py, pc = x ^ ((d >> 2) & 1), y ^ ((d >> 1) & 1), c ^ (d & 1)
            peer = 4 * px + 2 * py + pc
            for a in range(n):
                cp = pltpu.make_async_remote_copy(
                    src_ref=src(a, peer), dst_ref=recv[a].at[me], send_sem=send_sems.at[a, d],
                    recv_sem=recv_sems.at[a, d], device_id=(px, py, pc), device_id_type=pl.DeviceIdType.MESH)
                cp.start()
                copies.append((cp, a, peer, d))
        for cp, a, peer, d in copies:
            cp.wait_send()
            pltpu.make_async_remote_copy(
                src_ref=src(a, peer), dst_ref=recv[a].at[peer], send_sem=send_sems.at[a, d],
                recv_sem=recv_sems.at[a, d], device_id=(x, y, c), device_id_type=pl.DeviceIdType.MESH).wait_recv()
        for cp in local:
            cp.wait()

    any_spec = pl.BlockSpec(memory_space=pl.ANY)
    return pl.pallas_call(
        body, name=name, in_specs=[any_spec] * n, out_specs=[any_spec] * n,
        out_shape=[jax.ShapeDtypeStruct(((N_DEV,) + a.shape) if gather[i] else a.shape, a.dtype)
                   for i, a in enumerate(arrays)],
        scratch_shapes=[pltpu.SemaphoreType.DMA((n, N_DEV)), pltpu.SemaphoreType.DMA((n, N_DEV)),
                        pltpu.SemaphoreType.DMA((n,))],
    )(*arrays)


REDUCE_BLOCK_ELEMS = 64 * 1024


def _reduce_adamw(name, recv, w, m, v):
    rows, cols = w.shape
    tr = rows
    while tr % 16 == 0 and tr * (-(-cols // 128) * 128) > REDUCE_BLOCK_ELEMS:
        tr //= 2
    c1, c2 = 1.0 - ADAM_B1 ** ADAM_STEP, 1.0 - ADAM_B2 ** ADAM_STEP

    def body(r_ref, w_ref, m_ref, v_ref, g_ref, d_ref, nm_ref, nv_ref):
        g = r_ref[0]
        for j in range(1, N_DEV):
            g = g + r_ref[j]
        m_new = ADAM_B1 * m_ref[...] + (1.0 - ADAM_B1) * g
        v_new = ADAM_B2 * v_ref[...] + (1.0 - ADAM_B2) * jnp.square(g)
        d_ref[...] = -ADAM_LR * ((m_new / c1) / (jnp.sqrt(v_new / c2) + ADAM_EPS) + ADAM_WD * w_ref[...])
        g_ref[...], nm_ref[...], nv_ref[...] = g, m_new, v_new

    flat = pl.BlockSpec((tr, cols), lambda i: (i, 0))
    return pl.pallas_call(
        body, name=name, grid=(rows // tr,),
        in_specs=[pl.BlockSpec((N_DEV, tr, cols), lambda i: (0, i, 0)), flat, flat, flat], out_specs=[flat] * 4,
        out_shape=[jax.ShapeDtypeStruct((rows, cols), F32)] * 4, compiler_params=_params(("parallel",)),
    )(recv, w, m, v)


PACK_ROWS = 512


def _pack(cols):
    flat = jnp.concatenate(cols, axis=-1)
    tile = PACK_ROWS * 128
    flat = _pad_to(flat, flat.ndim - 1, -(-flat.shape[-1] // tile) * tile)
    return flat.reshape(flat.shape[:-1] + (-1, 128))


def _unpack(packed, shapes, lead=()):
    flat = packed.reshape(lead + (-1,))
    out, off = [], 0
    for shp in shapes:
        n = math.prod(shp)
        out.append(flat[..., off:off + n].reshape(lead + tuple(shp)))
        off += n
    return out


def _to_shards(name, full):
    ax = SHARD_AXIS[name]
    shp = full.shape
    return jnp.moveaxis(full.reshape(shp[:ax] + (N_DEV, shp[ax] // N_DEV) + shp[ax + 1:]), ax, 0)


def _from_shards(name, shards):
    ax = SHARD_AXIS[name]
    a = jnp.moveaxis(shards, 0, ax)
    return a.reshape(a.shape[:ax] + (a.shape[ax] * a.shape[ax + 1],) + a.shape[ax + 2:])


def kernel(x, positions, norm1_g, w_in, sgu_norm_g, w_spatial, b_spatial, q_lat_norm_g, w_q_up, kv_lat_norm_g, w_kv_up, q_norm_g, k_norm_g, conv_w, a_log, dt_bias, o_norm_g, w_branch, w_out, norm2_g, w_ff1, w_ff2, loss_target, m_norm1_g, m_w_in, m_sgu_norm_g, m_w_spatial, m_b_spatial, m_q_lat_norm_g, m_w_q_up, m_kv_lat_norm_g, m_w_kv_up, m_q_norm_g, m_k_norm_g, m_conv_w, m_a_log, m_dt_bias, m_o_norm_g, m_w_branch, m_w_out, m_norm2_g, m_w_ff1, m_w_ff2, v_norm1_g, v_w_in, v_sgu_norm_g, v_w_spatial, v_b_spatial, v_q_lat_norm_g, v_w_q_up, v_kv_lat_norm_g, v_w_kv_up, v_q_norm_g, v_k_norm_g, v_conv_w, v_a_log, v_dt_bias, v_o_norm_g, v_w_branch, v_w_out, v_norm2_g, v_w_ff1, v_w_ff2):
    args = locals()
    local_w = {n: args[n] for n in WEIGHTS}
    state = (local_w, {n: args["m_" + n] for n in WEIGHTS}, {n: args["v_" + n] for n in WEIGHTS})
    wire = [local_w[n] if n in EXACT_GATHER else local_w[n].astype(BF16) for n in SHARDED]
    full = dict(local_w)
    for n, part in zip(SHARDED, _exchange("gather_weights", wire, [True] * len(wire))):
        full[n] = part if n == "w_in" else _from_shards(n, part)
    sq, grad_x, grads = _local_step(x[0], positions.reshape(-1, 1), full, loss_target[0])
    loss = lax.psum(sq[0, 0] * (0.5 / D_MODEL), ("x", "y", "c"))
    send = [grads[n] if n == "w_in" else _to_shards(n, grads[n]) for n in SHARDED]
    rep = _pack([grads[n].reshape(-1) for n in REPLICATED])
    recv = _exchange("exchange_grads", send + [rep], [False] * len(send) + [True])
    results = {}
    for n, r in zip(SHARDED, recv):
        shp = local_w[n].shape
        flat = (math.prod(shp[:-1]), shp[-1])
        outs = _reduce_adamw("adamw_" + n, r.reshape((N_DEV,) + flat), *[src[n].reshape(flat) for src in state])
        results[n] = [o.reshape(shp) for o in outs]
    outs = _reduce_adamw("adamw_replicated", recv[-1],
                         *[_pack([src[n].reshape(-1) for n in REPLICATED]) for src in state])
    rep_shapes = [local_w[n].shape for n in REPLICATED]
    for n, vals in zip(REPLICATED, zip(*[_unpack(o, rep_shapes) for o in outs])):
        results[n] = vals
    return (loss, grad_x[None], *[results[n][k] for k in range(4) for n in WEIGHTS])
```

```python
import functools
import math

import jax
import jax.numpy as jnp
from jax import lax
from jax.experimental import pallas as pl
from jax.experimental.pallas import tpu as pltpu

F32, BF16 = jnp.float32, jnp.bfloat16
HI = lax.Precision.HIGHEST

N_DEV = 8
D_MODEL = 1024
DEPTH = 2
N_HEADS = 8
HEAD_PAD = 128
A_WIDTH = 512
B_NOPE, B_ROPE, B_VDIM = 64, 32, 64
B_QK = B_NOPE + B_ROPE
B_Q_LORA, B_KV_LORA = 384, 256
ROPE_BASE = 10000.0
C_DK = 64
C_CHUNK = 64
DELTA_SUB = 2
DELTA_BLOCK = 16
C_QKV = 1536
C_Z = 512
SGU_CHUNK = 128
D_FF = 4096
EPS = 1e-6
ADAM_LR, ADAM_B1, ADAM_B2, ADAM_EPS, ADAM_WD, ADAM_STEP = 0.001, 0.9, 0.999, 1e-08, 0.01, 10
O_QLAT, O_KVLAT, O_KROPE, O_CQKV, O_GATES, D_IN = 1024, 1408, 1664, 1696, 3760, 6832
W_B, W_C = 768, 2176
VMEM_LIMIT = 56 * 2 ** 20
NEG = -1e30

SHARDED = ("w_in", "w_q_up", "w_kv_up", "conv_w", "w_branch", "w_out", "w_ff1", "w_ff2")
EXACT_GATHER = ("conv_w",)
GRAD_WIRE = BF16
SHARD_AXIS = {"w_in": 2, "w_q_up": 2, "w_kv_up": 2, "conv_w": 2, "w_branch": 3, "w_out": 1, "w_ff1": 2, "w_ff2": 1}
REPLICATED = ("norm1_g", "sgu_norm_g", "w_spatial", "b_spatial", "q_lat_norm_g", "kv_lat_norm_g", "q_norm_g",
              "k_norm_g", "a_log", "dt_bias", "o_norm_g", "norm2_g")
WEIGHTS = ("norm1_g", "w_in", "sgu_norm_g", "w_spatial", "b_spatial", "q_lat_norm_g", "w_q_up", "kv_lat_norm_g",
           "w_kv_up", "q_norm_g", "k_norm_g", "conv_w", "a_log", "dt_bias", "o_norm_g", "w_branch", "w_out",
           "norm2_g", "w_ff1", "w_ff2")


def _params(sem, vmem=VMEM_LIMIT):
    return pltpu.CompilerParams(dimension_semantics=sem, vmem_limit_bytes=vmem)


_FORMS = {"nn": (1, 0), "nt": (1, 1), "tn": (0, 0)}


def _dot(form, a, b, batch, prec=None):
    ca, cb = _FORMS[form]
    o = 1 if batch else 0
    bd = ((0,), (0,)) if batch else ((), ())
    return lax.dot_general(a, b, (((ca + o,), (cb + o,)), bd), precision=prec, preferred_element_type=F32)


def _mm_raw(form, batch, a, b):
    return _dot(form, a.astype(BF16), b.astype(BF16), batch)


@functools.partial(jax.custom_vjp, nondiff_argnums=(0, 1))
def _mm(form, batch, a, b):
    return _mm_raw(form, batch, a, b)


def _mm_fwd(form, batch, a, b):
    return _mm_raw(form, batch, a, b), (a, b)


def _mm_bwd(form, batch, res, g):
    a, b = res
    if form == "nn":
        da, db = _mm_raw("nt", batch, g, b), _mm_raw("tn", batch, a, g)
    elif form == "nt":
        da, db = _mm_raw("nn", batch, g, b), _mm_raw("tn", batch, g, a)
    else:
        da, db = _mm_raw("nt", batch, b, g), _mm_raw("nn", batch, a, g)
    return da.astype(a.dtype), db.astype(b.dtype)


_mm.defvjp(_mm_fwd, _mm_bwd)


def _mmh(form, a, b, batch=False):
    return _dot(form, a, b, batch, HI)


@functools.partial(jax.custom_vjp, nondiff_argnums=(1, 2))
def _roll(x, shift, axis):
    return pltpu.roll(x, shift % x.shape[axis], axis)


def _roll_fwd(x, shift, axis):
    return _roll(x, shift, axis), None


def _roll_bwd(shift, axis, _, g):
    return (_roll(g, -shift, axis),)


_roll.defvjp(_roll_fwd, _roll_bwd)


@jax.custom_vjp
def _tile_heads(x):
    return jnp.concatenate([x] * N_HEADS, axis=1)


def _tile_heads_fwd(x):
    return _tile_heads(x), None


def _tile_heads_bwd(_, g):
    w = g.shape[1] // N_HEADS
    acc = g[:, :w]
    for h in range(1, N_HEADS):
        acc = acc + g[:, h * w:(h + 1) * w]
    return (acc,)


_tile_heads.defvjp(_tile_heads_fwd, _tile_heads_bwd)


def _iota(shape, dim):
    return lax.broadcasted_iota(jnp.int32, shape, dim)


def _head_indicator_t(width, per_head):
    return (_iota((N_HEADS, width), 1) // per_head == _iota((N_HEADS, width), 0)).astype(F32)


def _rms(x, g):
    return x * lax.rsqrt(jnp.mean(x * x, axis=-1, keepdims=True) + EPS) * g


def _gelu(x):
    return 0.5 * x * (1.0 + lax.erf(x * (2.0 ** -0.5)))


def _head_sums(x, per_head):
    blocks = []
    for b in range(x.shape[1] // 128):
        blk = x[:, b * 128:(b + 1) * 128]
        if per_head == 128:
            blocks.append(jnp.broadcast_to(jnp.sum(blk, axis=1, keepdims=True), blk.shape))
        else:
            low = _iota((1, 128), 1) < per_head
            s_low = jnp.sum(jnp.where(low, blk, 0.0), axis=1, keepdims=True)
            s_high = jnp.sum(jnp.where(low, 0.0, blk), axis=1, keepdims=True)
            blocks.append(jnp.where(low, s_low, s_high))
    return jnp.concatenate(blocks, axis=1)


def _head_rms(x, g_full, per_head, n_real):
    return x * lax.rsqrt(_head_sums(x * x, per_head) * (1.0 / n_real) + EPS) * g_full


def _rope(x, cos_t, sin_hi, sin_lo):
    half = B_ROPE // 2
    return (x * _tile_heads(cos_t) + _roll(x, half, 1) * _tile_heads(sin_hi)
            + _roll(x, -half, 1) * _tile_heads(sin_lo))


def _tok_spec(t, width, col):
    return pl.BlockSpec((t, width), lambda i, c=col: (i, c))


def _par_spec(shape):
    nd = len(shape)
    return pl.BlockSpec(shape, lambda i: (0,) * nd)


def _local_fwd(name, f, toks, pars, outs, t):
    s = toks[0][0].shape[0]
    nt, npar = len(toks), len(pars)

    def body(*refs):
        vals = [r[...] for r in refs[:nt + npar]]
        for r, o in zip(refs[nt + npar:], f(*vals)):
            r[...] = o.astype(r.dtype)

    return pl.pallas_call(
        body, name=name, grid=(s // t,),
        in_specs=[_tok_spec(t, w, c) for _, w, c in toks] + [_par_spec(p.shape) for p in pars],
        out_specs=[_tok_spec(t, w, 0) for w, _ in outs],
        out_shape=[jax.ShapeDtypeStruct((s, w), dt) for w, dt in outs],
        compiler_params=_params(("parallel",)),
    )(*[a for a, _, _ in toks], *pars)


def _local_bwd(name, f, toks, pars, cots, t, tok_diff, par_diff, grad_dtypes=None):
    s = toks[0][0].shape[0]
    nt, npar, nc = len(toks), len(pars), len(cots)
    dt_idx = [k for k in range(nt) if tok_diff[k]]
    dp_idx = [k for k in range(npar) if par_diff[k]]
    grad_dtypes = grad_dtypes or [F32] * len(dt_idx)

    def body(*refs):
        i = pl.program_id(0)
        tv = [r[...] for r in refs[:nt]]
        pv = [r[...] for r in refs[nt:nt + npar]]
        cv = [r[...].astype(F32) for r in refs[nt + npar:nt + npar + nc]]
        out_refs = refs[nt + npar + nc:]

        def g(*d):
            tt, pp = list(tv), list(pv)
            for k, val in zip(dt_idx, d[:len(dt_idx)]):
                tt[k] = val
            for k, val in zip(dp_idx, d[len(dt_idx):]):
                pp[k] = val
            return tuple(o.astype(F32) for o in f(*tt, *pp))

        _, vjp = jax.vjp(g, *[tv[k] for k in dt_idx], *[pv[k] for k in dp_idx])
        grads = vjp(tuple(cv))
        for r, gr in zip(out_refs[:len(dt_idx)], grads[:len(dt_idx)]):
            r[...] = gr.astype(r.dtype)
        par_refs = out_refs[len(dt_idx):]

        @pl.when(i == 0)
        def _():
            for r in par_refs:
                r[...] = jnp.zeros(r.shape, r.dtype)

        for r, gr in zip(par_refs, grads[len(dt_idx):]):
            r[...] += gr.astype(F32)

    res = pl.pallas_call(
        body, name=name, grid=(s // t,),
        in_specs=([_tok_spec(t, w, c) for _, w, c in toks] + [_par_spec(p.shape) for p in pars]
                  + [_tok_spec(t, c.shape[1], 0) for c in cots]),
        out_specs=([_tok_spec(t, toks[k][1], 0) for k in dt_idx] + [_par_spec(pars[k].shape) for k in dp_idx]),
        out_shape=([jax.ShapeDtypeStruct((s, toks[k][1]), dt) for k, dt in zip(dt_idx, grad_dtypes)]
                   + [jax.ShapeDtypeStruct(pars[k].shape, F32) for k in dp_idx]),
        compiler_params=_params(("arbitrary",)),
    )(*[a for a, _, _ in toks], *pars, *cots)
    return res[:len(dt_idx)], res[len(dt_idx):]


def _f_norm(x, g):
    return (_rms(x, g),)


def _f_norm_res(x, g):
    return _rms(x, g), x


def _f_rope_tables(pos, inv_row):
    ang = pos.astype(F32) * inv_row
    lane = _iota(ang.shape, 1)
    sn = jnp.sin(ang)
    half = B_ROPE // 2
    sin_hi = jnp.where((lane >= B_NOPE + half) & (lane < B_QK), sn, 0.0)
    sin_lo = jnp.where((lane >= B_NOPE) & (lane < B_NOPE + half), -sn, 0.0)
    return jnp.cos(ang), sin_hi, sin_lo


def _f_sgu(p_a, g, w_s, b_s):
    u = _gelu(p_a[:, :A_WIDTH])
    v = _rms(_gelu(p_a[:, A_WIDTH:]), g)
    tril = _iota((SGU_CHUNK, SGU_CHUNK), 1) <= _iota((SGU_CHUNK, SGU_CHUNK), 0)
    w_cat = jnp.concatenate([jnp.where(tril, w_s[gi], 0.0) for gi in range(N_HEADS)], axis=1)
    group = _iota((1, A_WIDTH), 1) // (A_WIDTH // N_HEADS)
    v_stack = jnp.concatenate([jnp.where(group == gi, v, 0.0) for gi in range(N_HEADS)], axis=0)
    bias = _mmh("tn", b_s, _head_indicator_t(A_WIDTH, A_WIDTH // N_HEADS))
    return (u * (_mm("nn", False, w_cat, v_stack) + bias),)


def _f_mla_prep(p_b, cos_t, sin_hi, sin_lo, q_lat_g, kv_lat_g, wq, wk, wv, qn_g, kn_g):
    kv_lat, k_rope, q_lat = p_b[:, :B_KV_LORA], p_b[:, B_KV_LORA:B_KV_LORA + HEAD_PAD], p_b[:, B_KV_LORA + HEAD_PAD:]
    q = _mm("nn", False, _rms(q_lat, q_lat_g), wq)
    q = _rope(_head_rms(q, _tile_heads(qn_g), HEAD_PAD, B_QK), cos_t, sin_hi, sin_lo)
    kvn = _rms(kv_lat, kv_lat_g)
    k = _mm("nn", False, kvn, wk) + _tile_heads(k_rope)
    k = _rope(_head_rms(k, _tile_heads(kn_g), HEAD_PAD, B_QK), cos_t, sin_hi, sin_lo)
    return q, k, _mm("nn", False, kvn, wv)


def _f_gdn_pre(conv_pre, ba, a_log_row, dt_row):
    qkv = jax.nn.silu(conv_pre)

    def l2(x):
        return x * lax.rsqrt(_head_sums(x * x, C_DK) + EPS)

    lane = _iota(ba.shape, 1)
    g = -jnp.exp(a_log_row) * jax.nn.softplus(ba + dt_row)
    gb = jnp.where(lane < N_HEADS, jax.nn.sigmoid(ba), jnp.where(lane < 2 * N_HEADS, g, 0.0))
    return l2(qkv[:, :A_WIDTH]), l2(qkv[:, A_WIDTH:2 * A_WIDTH]), qkv[:, 2 * A_WIDTH:], gb


def _f_gdn_post(o, c_z, o_g):
    place = (_iota((C_DK, A_WIDTH), 1) % C_DK == _iota((C_DK, A_WIDTH), 0)).astype(F32)
    return (_head_rms(o, _mmh("nn", o_g, place), C_DK, C_DK) * jax.nn.silu(c_z),)


def _f_merge(p_g, y0, y1, y2):
    d = D_MODEL
    return (jax.nn.sigmoid(p_g[:, :d]) * y0 + jax.nn.sigmoid(p_g[:, d:2 * d]) * y1
            + jax.nn.sigmoid(p_g[:, 2 * d:]) * y2,)


def _pick(n, whole_up_to, candidates):
    if n <= whole_up_to:
        return n
    for c in candidates:
        if n % c == 0:
            return c
    return n


def _matmul(name, a, w, add=None, out_dtype=F32, tm=512, extras=(), epilogue=None, out_dtypes=None):
    m, k = a.shape
    n = w.shape[1]
    tm = min(tm, m)
    tn = _pick(n, 2304, (2048, 1536, 1024, 512))
    if add is not None:
        extras, epilogue = (add,), lambda r, x: (r + x,)
    if epilogue is None:
        epilogue = lambda r: (r,)
    out_dtypes = out_dtypes or (out_dtype,)
    n_ex, n_out = len(extras), len(out_dtypes)

    def body(*refs):
        a_ref, w_ref = refs[0], refs[1]
        ex_refs, o_refs = refs[2:2 + n_ex], refs[2 + n_ex:2 + n_ex + n_out]
        total = jnp.dot(a_ref[...].astype(BF16), w_ref[...].astype(BF16), preferred_element_type=F32)
        for o_ref, r in zip(o_refs, epilogue(total, *[e[...] for e in ex_refs])):
            o_ref[...] = r.astype(o_ref.dtype)

    tile = pl.BlockSpec((tm, tn), lambda j, i: (i, j))
    res = pl.pallas_call(
        body, name=name, grid=(n // tn, m // tm),
        in_specs=[pl.BlockSpec((tm, k), lambda j, i: (i, 0)), pl.BlockSpec((k, tn), lambda j, i: (0, j))] + [tile] * n_ex,
        out_specs=[tile] * n_out, out_shape=[jax.ShapeDtypeStruct((m, n), dt) for dt in out_dtypes],
        compiler_params=_params(("parallel", "parallel")),
    )(a, w, *extras)
    return res[0] if n_out == 1 else res


def _matmul_tn(name, a, b, a_col=None, tm=1024):
    m = a.shape[0]
    k, acol = (a.shape[1], 0) if a_col is None else a_col
    n = b.shape[1]
    tm = min(tm, m)
    tk = _pick(k, 1536, (1024, 512))
    tn = _pick(n, 2304, (1024, 512))
    nm = m // tm

    def body(a_ref, b_ref, o_ref):
        mm = pl.program_id(2)
        part = lax.dot_general(a_ref[...].astype(BF16), b_ref[...].astype(BF16), (((0,), (0,)), ((), ())),
                               preferred_element_type=F32)

        @pl.when(mm == 0)
        def _():
            o_ref[...] = part

        @pl.when(mm > 0)
        def _():
            o_ref[...] += part

    kb = k // tk
    return pl.pallas_call(
        body, name=name, grid=(kb, n // tn, nm),
        in_specs=[pl.BlockSpec((tm, tk), lambda i, j, mm: (mm, acol * kb + i)),
                  pl.BlockSpec((tm, tn), lambda i, j, mm: (mm, j))],
        out_specs=pl.BlockSpec((tk, tn), lambda i, j, mm: (i, j)),
        out_shape=jax.ShapeDtypeStruct((k, n), F32),
        compiler_params=_params(("parallel", "parallel", "arbitrary")),
    )(a, b)


def _shift_down(x, prev, s):
    rolled = pltpu.roll(x, s, 0)
    pr = pltpu.roll(prev, s, 0)
    head = jnp.where(_iota((8, 1), 0) < s, pr, rolled[:8])
    return jnp.concatenate([head, rolled[8:]], axis=0)


def _shift_up(x, nxt, s):
    t = x.shape[0]
    rolled = pltpu.roll(x, t - s, 0)
    nr = pltpu.roll(nxt, 8 - s, 0)
    tail = jnp.where(_iota((8, 1), 0) >= 8 - s, nr, rolled[t - 8:])
    return jnp.concatenate([rolled[:t - 8], tail], axis=0)


def _conv_fwd(p_c, w8, t):
    s = p_c.shape[0]
    t = min(t, s)
    r = t // 8

    def body(x_ref, prev_ref, w_ref, o_ref):
        i = pl.program_id(0)
        x = x_ref[...]
        prev = jnp.where(i == 0, 0.0, prev_ref[...])
        acc = w_ref[3:4, :] * x
        for sh in range(1, 4):
            acc = acc + w_ref[3 - sh:4 - sh, :] * _shift_down(x, prev, sh)
        o_ref[...] = acc

    return pl.pallas_call(
        body, name="conv_fwd", grid=(s // t,),
        in_specs=[pl.BlockSpec((t, C_QKV), lambda i: (i, 0)),
                  pl.BlockSpec((8, C_QKV), lambda i: (jnp.maximum(i * r - 1, 0), 0)),
                  pl.BlockSpec((8, C_QKV), lambda i: (0, 0))],
        out_specs=pl.BlockSpec((t, C_QKV), lambda i: (i, 0)),
        out_shape=jax.ShapeDtypeStruct((s, C_QKV), F32),
        compiler_params=_params(("parallel",)),
    )(p_c, p_c, w8)


def _conv_bwd(p_c, dy, w8, t):
    s = p_c.shape[0]
    t = min(t, s)
    r = t // 8
    n = s // t

    def body(x_ref, prev_ref, dy_ref, next_ref, w_ref, dx_ref, dw_ref):
        i = pl.program_id(0)
        x, g = x_ref[...], dy_ref[...]
        prev = jnp.where(i == 0, 0.0, prev_ref[...])
        nxt = jnp.where(i == n - 1, 0.0, next_ref[...])

        @pl.when(i == 0)
        def _():
            dw_ref[...] = jnp.zeros(dw_ref.shape, F32)

        dx = w_ref[3:4, :] * g
        dw_ref[3:4, :] += jnp.sum(g * x, axis=0, keepdims=True)
        for sh in range(1, 4):
            dx = dx + w_ref[3 - sh:4 - sh, :] * _shift_up(g, nxt, sh)
            dw_ref[3 - sh:4 - sh, :] += jnp.sum(g * _shift_down(x, prev, sh), axis=0, keepdims=True)
        dx_ref[...] = dx.astype(dx_ref.dtype)

    return pl.pallas_call(
        body, name="conv_bwd", grid=(n,),
        in_specs=[pl.BlockSpec((t, C_QKV), lambda i: (i, 0)),
                  pl.BlockSpec((8, C_QKV), lambda i: (jnp.maximum(i * r - 1, 0), 0)),
                  pl.BlockSpec((t, C_QKV), lambda i: (i, 0)),
                  pl.BlockSpec((8, C_QKV), lambda i: (jnp.minimum((i + 1) * r, s // 8 - 1), 0)),
                  pl.BlockSpec((8, C_QKV), lambda i: (0, 0))],
        out_specs=[pl.BlockSpec((t, C_QKV), lambda i: (i, 0)), pl.BlockSpec((8, C_QKV), lambda i: (0, 0))],
        out_shape=[jax.ShapeDtypeStruct((s, C_QKV), BF16), jax.ShapeDtypeStruct((8, C_QKV), F32)],
        compiler_params=_params(("arbitrary",)),
    )(p_c, p_c, dy, dy, w8)


def _delta_chunks(state, q, k, v, g, beta):
    pre = [_delta_pre(q[:, i * C_CHUNK:(i + 1) * C_CHUNK], k[:, i * C_CHUNK:(i + 1) * C_CHUNK],
                      v[:, i * C_CHUNK:(i + 1) * C_CHUNK], g[i], beta[i]) for i in range(DELTA_SUB)]
    outs = []
    for qg, intra, u, w, k_dec, dec in pre:
        v_new = u - _mm("nn", True, w, state)
        outs.append(_mm("nn", True, qg, state) + _mm("nn", True, intra, v_new))
        state = state * dec + _mm("tn", True, k_dec, v_new)
    return state, jnp.concatenate(outs, axis=1)


def _delta_pre(q, k, v, g, beta):
    c = C_CHUNK
    row, col = _iota((c, c), 0), _iota((c, c), 1)
    tril, strict = col <= row, col < row
    gc = _mmh("nn", g, (row <= col).astype(F32))
    g_last = jnp.sum(g, axis=1, keepdims=True)
    qs = q * (C_DK ** -0.5)
    decay = jnp.exp(jnp.where(tril, gc[:, :, None] - gc[:, None, :], NEG))
    k_beta, v_beta = k * beta[:, :, None], v * beta[:, :, None]
    x = -jnp.where(strict, _mm("nt", True, k_beta, k) * decay, 0.0)
    xd = jnp.where(row // DELTA_BLOCK == col // DELTA_BLOCK, x, 0.0)
    powers = [xd]
    for _ in range(3):
        powers.append(_mm("nn", True, powers[-1], powers[-1]))
    sol = jnp.concatenate([x - xd, v_beta, k_beta * jnp.exp(gc)[:, :, None]], axis=2)
    for p in reversed(powers):
        sol = sol + _mm("nn", True, p, sol)
    y, sol = sol[:, :, :c], sol[:, :, c:]
    sol = sol + _mm("nn", True, _mm("nn", True, y, y), sol)
    sol = sol + _mm("nn", True, y, sol)
    u, w = sol[:, :, :C_DK], sol[:, :, C_DK:]
    intra = jnp.where(tril, _mm("nt", True, qs, k) * decay, 0.0)
    return (qs * jnp.exp(gc)[:, :, None], intra, u, w, k * jnp.exp(g_last - gc)[:, :, None],
            jnp.exp(g_last)[:, :, None])


def _delta_specs(n, rev):
    def at(i):
        return n - 1 - i if rev else i
    tok = pl.BlockSpec((N_HEADS, DELTA_SUB * C_CHUNK, C_DK), lambda i: (0, at(i), 0))
    vec = pl.BlockSpec((DELTA_SUB, N_HEADS, C_CHUNK), lambda i: (at(i), 0, 0))
    st = pl.BlockSpec((1, N_HEADS, C_DK, C_DK), lambda i: (at(i), 0, 0, 0))
    return tok, vec, st


def _delta_fwd(q, k, v, g, beta):
    s = q.shape[1]
    n = s // (DELTA_SUB * C_CHUNK)
    tok, vec, st = _delta_specs(n, False)

    def body(q_ref, k_ref, v_ref, g_ref, b_ref, o_ref, st_ref, state):
        @pl.when(pl.program_id(0) == 0)
        def _():
            state[...] = jnp.zeros(state.shape, F32)

        cur = state[...]
        st_ref[0] = cur
        new, o = _delta_chunks(cur, q_ref[...], k_ref[...], v_ref[...], g_ref[...], b_ref[...])
        o_ref[...] = o
        state[...] = new

    return pl.pallas_call(
        body, name="delta_fwd", grid=(n,), in_specs=[tok, tok, tok, vec, vec], out_specs=[tok, st],
        out_shape=[jax.ShapeDtypeStruct((N_HEADS, s, C_DK), F32), jax.ShapeDtypeStruct((n, N_HEADS, C_DK, C_DK), F32)],
        scratch_shapes=[pltpu.VMEM((N_HEADS, C_DK, C_DK), F32)],
        compiler_params=_params(("arbitrary",)),
    )(q, k, v, g, beta)


def _delta_bwd(q, k, v, g, beta, states, do):
    s = q.shape[1]
    n = s // (DELTA_SUB * C_CHUNK)
    tok, vec, st = _delta_specs(n, True)

    def body(q_ref, k_ref, v_ref, g_ref, b_ref, st_ref, do_ref, dq_ref, dk_ref, dv_ref, dg_ref, db_ref, dstate):
        @pl.when(pl.program_id(0) == 0)
        def _():
            dstate[...] = jnp.zeros(dstate.shape, F32)

        _, vjp = jax.vjp(_delta_chunks, st_ref[0], q_ref[...], k_ref[...], v_ref[...], g_ref[...], b_ref[...])
        dst, dq, dk, dv, dg, db = vjp((dstate[...], do_ref[...]))
        dq_ref[...], dk_ref[...], dv_ref[...] = dq, dk, dv
        dg_ref[...], db_ref[...] = dg, db
        dstate[...] = dst

    tok_shape = jax.ShapeDtypeStruct((N_HEADS, s, C_DK), F32)
    vec_shape = jax.ShapeDtypeStruct((n * DELTA_SUB, N_HEADS, C_CHUNK), F32)
    return pl.pallas_call(
        body, name="delta_bwd", grid=(n,), in_specs=[tok, tok, tok, vec, vec, st, tok],
        out_specs=[tok, tok, tok, vec, vec], out_shape=[tok_shape, tok_shape, tok_shape, vec_shape, vec_shape],
        scratch_shapes=[pltpu.VMEM((N_HEADS, C_DK, C_DK), F32)],
        compiler_params=_params(("arbitrary",)),
    )(q, k, v, g, beta, states, do)


ATT_SCALE = B_QK ** -0.5


SCORE_SCALE_LOG2 = ATT_SCALE * math.log2(math.e)


def _lanes(x, n):
    return x if n == 1 else jnp.concatenate([x] * n, axis=1)


def _scores(a_ref, b_ref):
    return lax.dot_general(a_ref[...], b_ref[...], (((1,), (1,)), ((), ())), preferred_element_type=F32) * SCORE_SCALE_LOG2


def _flash_fwd(q, k, v, tb):
    s = q.shape[0]
    tb = min(tb, s)
    nb = s // tb
    nrep = tb // 128

    def body(q_ref, k_ref, v_ref, o_ref, lse_ref, m_s, l_s, acc):
        i, j = pl.program_id(1), pl.program_id(2)

        @pl.when(j == 0)
        def _():
            m_s[...] = jnp.full(m_s.shape, -jnp.inf, F32)
            l_s[...] = jnp.zeros(l_s.shape, F32)
            acc[...] = jnp.zeros(acc.shape, F32)

        def step(masked):
            sc = _scores(q_ref, k_ref)
            if masked:
                sc = jnp.where(_iota((tb, tb), 1) <= _iota((tb, tb), 0), sc, -jnp.inf)
            m_prev = m_s[...]
            m_new = jnp.maximum(m_prev, jnp.max(sc, axis=1, keepdims=True))
            alpha = jnp.exp2(m_prev - m_new)
            p = jnp.exp2(sc - _lanes(m_new, nrep))
            l_s[...] = alpha * l_s[...] + jnp.sum(p, axis=1, keepdims=True)
            acc[...] = alpha * acc[...] + jnp.dot(p.astype(BF16), v_ref[...], preferred_element_type=F32)
            m_s[...] = m_new

        @pl.when(j < i)
        def _():
            step(False)

        @pl.when(j == i)
        def _():
            step(True)
            o_ref[...] = acc[...] / l_s[...]
            lse_ref[...] = m_s[...] + jnp.log2(l_s[...])

    qs = pl.BlockSpec((tb, HEAD_PAD), lambda h, i, j: (i, h))
    ks = pl.BlockSpec((tb, HEAD_PAD), lambda h, i, j: (jnp.minimum(j, i), h))
    shape = jax.ShapeDtypeStruct((s, N_HEADS * HEAD_PAD), F32)
    return pl.pallas_call(
        body, name="flash_fwd", grid=(N_HEADS, nb, nb), in_specs=[qs, ks, ks],
        out_specs=[qs, qs], out_shape=[shape, shape],
        scratch_shapes=[pltpu.VMEM((tb, 128), F32), pltpu.VMEM((tb, 128), F32), pltpu.VMEM((tb, HEAD_PAD), F32)],
        compiler_params=_params(("parallel", "parallel", "arbitrary")),
    )(q, k, v)


def _f_attn_delta(o, do):
    return (_head_sums(o * do, HEAD_PAD),)


def _row_stats(rep):
    return rep[:, ::HEAD_PAD].T.reshape(N_HEADS, 1, -1)


def _flash_bwd(q, k, v, do, lse_row, delta_row, tb):
    s = q.shape[0]
    tb = min(tb, s)
    nb = s // tb

    def body(q_ref, k_ref, v_ref, do_ref, lse_ref, dl_ref, dq_ref, dk_ref, dv_ref, dk_acc, dv_acc):
        j, i = pl.program_id(1), pl.program_id(2)

        @pl.when((i == 0) & (j == 0))
        def _():
            dq_ref[...] = jnp.zeros(dq_ref.shape, F32)

        @pl.when(i == 0)
        def _():
            dk_acc[...] = jnp.zeros(dk_acc.shape, F32)
            dv_acc[...] = jnp.zeros(dv_acc.shape, F32)

        def step(masked):
            st = _scores(k_ref, q_ref)
            if masked:
                st = jnp.where(_iota((tb, tb), 0) <= _iota((tb, tb), 1), st, -jnp.inf)
            do = do_ref[...].astype(BF16)
            pt = jnp.exp2(st - lse_ref[0])
            dpt = lax.dot_general(v_ref[...], do, (((1,), (1,)), ((), ())), preferred_element_type=F32)
            dst = (pt * (dpt - dl_ref[0])).astype(BF16)
            dv_acc[...] += jnp.dot(pt.astype(BF16), do, preferred_element_type=F32)
            dk_acc[...] += jnp.dot(dst, q_ref[...], preferred_element_type=F32)
            rows = pl.ds(pl.multiple_of(i * tb, tb), tb)
            dq_ref[rows, :] += lax.dot_general(dst, k_ref[...], (((0,), (0,)), ((), ())),
                                               preferred_element_type=F32) * ATT_SCALE

        @pl.when(i == j)
        def _():
            step(True)

        @pl.when(i > j)
        def _():
            step(False)

        @pl.when(i == nb - 1)
        def _():
            dk_ref[...] = dk_acc[...] * ATT_SCALE
            dv_ref[...] = dv_acc[...]

    qs = pl.BlockSpec((tb, HEAD_PAD), lambda h, j, i: (jnp.maximum(i, j), h))
    ks = pl.BlockSpec((tb, HEAD_PAD), lambda h, j, i: (j, h))
    rs = pl.BlockSpec((1, 1, tb), lambda h, j, i: (h, 0, jnp.maximum(i, j)))
    shape = jax.ShapeDtypeStruct((s, N_HEADS * HEAD_PAD), F32)
    return pl.pallas_call(
        body, name="flash_bwd", grid=(N_HEADS, nb, nb), in_specs=[qs, ks, ks, qs, rs, rs],
        out_specs=[pl.BlockSpec((s, HEAD_PAD), lambda h, j, i: (0, h)), ks, ks], out_shape=[shape, shape, shape],
        scratch_shapes=[pltpu.VMEM((tb, HEAD_PAD), F32), pltpu.VMEM((tb, HEAD_PAD), F32)],
        compiler_params=_params(("parallel", "arbitrary", "arbitrary")),
    )(q, k, v, do, lse_row, delta_row)


def _loss_head(y, target, t):
    s, d = y.shape
    t = min(t, s)

    def body(y_ref, t_ref, sum_ref, dy_ref):
        @pl.when(pl.program_id(0) == 0)
        def _():
            sum_ref[...] = jnp.zeros(sum_ref.shape, F32)

        err = y_ref[...] - t_ref[...]
        dy_ref[...] = err * (1.0 / d)
        sum_ref[...] += jnp.broadcast_to(jnp.sum(err * err), sum_ref.shape)

    return pl.pallas_call(
        body, name="loss_head", grid=(s // t,),
        in_specs=[pl.BlockSpec((t, d), lambda i: (i, 0))] * 2,
        out_specs=[pl.BlockSpec((1, 128), lambda i: (0, 0)), pl.BlockSpec((t, d), lambda i: (i, 0))],
        out_shape=[jax.ShapeDtypeStruct((1, 128), F32), jax.ShapeDtypeStruct((s, d), F32)],
        compiler_params=_params(("arbitrary",)),
    )(y, target)


def _pad_to(a, axis, size):
    pad = [(0, 0)] * a.ndim
    pad[axis] = (0, size - a.shape[axis])
    return jnp.pad(a, pad)


W_IN_GROUPS = {
    "a": [(0, O_QLAT)],
    "b": [(O_KVLAT, O_KROPE), B_NOPE, (O_KROPE, O_CQKV), HEAD_PAD - B_QK, (O_QLAT, O_KVLAT)],
    "c": [(O_CQKV, O_GATES), W_C - (O_GATES - O_CQKV)],
    "g": [(O_GATES, D_IN)],
}
W_IN_SHARD = D_IN // N_DEV
W_IN_ROWS = 128


def _group_width(key):
    return sum(e if isinstance(e, int) else e[1] - e[0] for e in W_IN_GROUPS[key])


def _assemble_w_in(shards):
    depth = shards.shape[1]
    keys = list(W_IN_GROUPS)

    def body(s_ref, *o_refs):
        vals = [s_ref[k, 0] for k in range(N_DEV)]
        for key, o_ref in zip(keys, o_refs):
            parts = []
            for e in W_IN_GROUPS[key]:
                if isinstance(e, int):
                    parts.append(jnp.zeros((W_IN_ROWS, e), shards.dtype))
                    continue
                lo, hi = e
                while lo < hi:
                    k = lo // W_IN_SHARD
                    end = min(hi, (k + 1) * W_IN_SHARD)
                    parts.append(vals[k][:, lo - k * W_IN_SHARD:end - k * W_IN_SHARD])
                    lo = end
            o_ref[0] = parts[0] if len(parts) == 1 else jnp.concatenate(parts, axis=1)

    return pl.pallas_call(
        body, name="assemble_w_in", grid=(depth, D_MODEL // W_IN_ROWS),
        in_specs=[pl.BlockSpec((N_DEV, 1, W_IN_ROWS, W_IN_SHARD), lambda l, i: (0, l, i, 0))],
        out_specs=[pl.BlockSpec((1, W_IN_ROWS, _group_width(key)), lambda l, i: (l, i, 0)) for key in keys],
        out_shape=[jax.ShapeDtypeStruct((depth, D_MODEL, _group_width(key)), shards.dtype) for key in keys],
        compiler_params=_params(("parallel", "parallel")),
    )(shards)


def _split_dw_in(groups):
    keys = list(W_IN_GROUPS)
    depth = groups[0].shape[0]
    runs = []
    for gi, key in enumerate(keys):
        col = 0
        for e in W_IN_GROUPS[key]:
            if not isinstance(e, int):
                runs.append((e[0], e[1], gi, col))
            col += e if isinstance(e, int) else e[1] - e[0]
    runs.sort()

    def body(*refs):
        vals = [r[0] for r in refs[:len(keys)]]
        o_ref = refs[len(keys)]
        for k in range(N_DEV):
            lo, hi = k * W_IN_SHARD, (k + 1) * W_IN_SHARD
            parts = []
            for a, b, gi, col in runs:
                s, e = max(a, lo), min(b, hi)
                if s < e:
                    parts.append(vals[gi][:, col + s - a:col + e - a])
            o_ref[k, 0] = jnp.concatenate(parts, axis=1).astype(GRAD_WIRE)

    return pl.pallas_call(
        body, name="split_dw_in", grid=(depth, D_MODEL // W_IN_ROWS),
        in_specs=[pl.BlockSpec((1, W_IN_ROWS, _group_width(key)), lambda l, i: (l, i, 0)) for key in keys],
        out_specs=pl.BlockSpec((N_DEV, 1, W_IN_ROWS, W_IN_SHARD), lambda l, i: (0, l, i, 0)),
        out_shape=jax.ShapeDtypeStruct((N_DEV, depth, D_MODEL, W_IN_SHARD), GRAD_WIRE),
        compiler_params=_params(("parallel", "parallel")),
    )(*groups)


def _prep_layer(w, l):
    p = {}
    for key in W_IN_GROUPS:
        val = w["w_in_" + key][l].astype(BF16)
        p["w_" + key] = val
        p["wt_" + key] = val.T
    for name in ("norm1_g", "sgu_norm_g", "q_lat_norm_g", "kv_lat_norm_g", "o_norm_g", "norm2_g"):
        p[name] = w[name][l][None, :]
    p["w_spatial"], p["b_spatial"] = w["w_spatial"][l], w["b_spatial"][l]
    p["wq"] = _pad_to(w["w_q_up"][l].astype(F32).reshape(B_Q_LORA, N_HEADS, B_QK), 2, HEAD_PAD).reshape(B_Q_LORA, -1)
    kv = w["w_kv_up"][l].astype(F32).reshape(B_KV_LORA, N_HEADS, B_NOPE + B_VDIM)
    p["wk"] = _pad_to(kv[:, :, :B_NOPE], 2, HEAD_PAD).reshape(B_KV_LORA, -1)
    p["wv"] = _pad_to(kv[:, :, B_NOPE:], 2, HEAD_PAD).reshape(B_KV_LORA, -1)
    p["qn_g"] = _pad_to(w["q_norm_g"][l][None, :], 1, HEAD_PAD)
    p["kn_g"] = _pad_to(w["k_norm_g"][l][None, :], 1, HEAD_PAD)
    p["conv_w"] = _pad_to(w["conv_w"][l], 0, 8)
    row = lambda v: jnp.pad(v[None, :], ((0, 0), (N_HEADS, HEAD_PAD - 2 * N_HEADS)))
    p["a_log"], p["dt_bias"] = row(w["a_log"][l]), row(w["dt_bias"][l])
    wb = w["w_branch"][l]
    wb1 = _pad_to(wb[1].reshape(N_HEADS, B_VDIM, D_MODEL), 1, HEAD_PAD).reshape(-1, D_MODEL)
    for key, val in (("wb0", wb[0]), ("wb1", wb1), ("wb2", wb[2]), ("w_out", w["w_out"][l]),
                     ("w_ff1", w["w_ff1"][l]), ("w_ff2", w["w_ff2"][l])):
        p[key] = val.astype(BF16)
        p[key + "_t"] = val.T.astype(BF16)
    return p


def _unprep_grads(g):
    out = {"w_in_" + key: g["w_" + key] for key in W_IN_GROUPS}
    for name in ("norm1_g", "sgu_norm_g", "q_lat_norm_g", "kv_lat_norm_g", "o_norm_g", "norm2_g"):
        out[name] = g[name][0]
    out["w_spatial"], out["b_spatial"] = g["w_spatial"], g["b_spatial"]
    out["w_q_up"] = g["wq"].reshape(B_Q_LORA, N_HEADS, HEAD_PAD)[:, :, :B_QK].reshape(B_Q_LORA, -1)
    gk = g["wk"].reshape(B_KV_LORA, N_HEADS, HEAD_PAD)[:, :, :B_NOPE]
    gv = g["wv"].reshape(B_KV_LORA, N_HEADS, HEAD_PAD)[:, :, :B_VDIM]
    out["w_kv_up"] = jnp.concatenate([gk, gv], axis=2).reshape(B_KV_LORA, -1)
    out["q_norm_g"], out["k_norm_g"] = g["qn_g"][0, :B_QK], g["kn_g"][0, :B_QK]
    out["conv_w"] = g["conv_w"][:4]
    out["a_log"], out["dt_bias"] = g["a_log"][0, N_HEADS:2 * N_HEADS], g["dt_bias"][0, N_HEADS:2 * N_HEADS]
    gb1 = g["wb1"].reshape(N_HEADS, HEAD_PAD, D_MODEL)[:, :B_VDIM].reshape(-1, D_MODEL)
    out["w_branch"] = jnp.stack([g["wb0"], gb1, g["wb2"]], axis=0)
    out["w_out"], out["w_ff1"], out["w_ff2"] = g["w_out"], g["w_ff1"], g["w_ff2"]
    return out


def _heads_first(a):
    s = a.shape[0]
    return a.reshape(s, N_HEADS, C_DK).transpose(1, 0, 2)


def _heads_last(a):
    return a.transpose(1, 0, 2).reshape(a.shape[1], N_HEADS * C_DK)


def _chunk_vec(a):
    return a.reshape(-1, C_CHUNK, N_HEADS).transpose(0, 2, 1)


def _unchunk_vec(a):
    return a.transpose(0, 2, 1).reshape(-1, N_HEADS)


def _layer_fwd(x, p, tabs, t, tb):
    sv = {"x": x}
    h1, = _local_fwd("norm1", _f_norm, [(x, D_MODEL, 0)], [p["norm1_g"]], [(D_MODEL, BF16)], t)
    sv["h1"] = h1
    p_a, p_b, p_c, p_g = (_matmul("proj_" + key, h1, p["w_" + key]) for key in "abcg")
    sv.update(p_a=p_a, p_b=p_b, p_c=p_c, p_g=p_g)
    y_a, = _local_fwd("sgu", _f_sgu, [(p_a, 2 * A_WIDTH, 0)], [p["sgu_norm_g"], p["w_spatial"], p["b_spatial"]],
                      [(A_WIDTH, BF16)], SGU_CHUNK)
    q, k, v = _local_fwd("mla_prep", _f_mla_prep, [(p_b, W_B, 0)] + [(tb_, HEAD_PAD, 0) for tb_ in tabs],
                         [p["q_lat_norm_g"], p["kv_lat_norm_g"], p["wq"], p["wk"], p["wv"], p["qn_g"], p["kn_g"]],
                         [(N_HEADS * HEAD_PAD, BF16)] * 3, min(t, 256))
    o_b, lse = _flash_fwd(q, k, v, tb)
    sv.update(q=q, k=k, v=v, o_b=o_b, lse=lse)
    conv_pre = _conv_fwd(p_c, p["conv_w"], t)
    cq, ck, cv, gb = _local_fwd("gdn_pre", _f_gdn_pre, [(conv_pre, C_QKV, 0), (p_c, HEAD_PAD, (C_QKV + C_Z) // HEAD_PAD)],
                                [p["a_log"], p["dt_bias"]], [(A_WIDTH, F32)] * 3 + [(HEAD_PAD, F32)], t)
    cq, ck, cv = _heads_first(cq), _heads_first(ck), _heads_first(cv)
    beta, g = _chunk_vec(gb[:, :N_HEADS]), _chunk_vec(gb[:, N_HEADS:2 * N_HEADS])
    o_c, states = _delta_fwd(cq, ck, cv, g, beta)
    o_c = _heads_last(o_c)
    sv.update(conv_pre=conv_pre, cq=cq, ck=ck, cv=cv, beta=beta, g=g, states=states, o_c=o_c)
    y_c, = _local_fwd("gdn_post", _f_gdn_post, [(o_c, A_WIDTH, 0), (p_c, C_Z, C_QKV // C_Z)], [p["o_norm_g"]],
                      [(A_WIDTH, BF16)], t)
    y0 = _matmul("branch0", y_a, p["wb0"])
    y1 = _matmul("branch1", o_b, p["wb1"])
    y2 = _matmul("branch2", y_c, p["wb2"])
    merged, = _local_fwd("merge", _f_merge, [(p_g, 3 * D_MODEL, 0), (y0, D_MODEL, 0), (y1, D_MODEL, 0), (y2, D_MODEL, 0)],
                         [], [(D_MODEL, BF16)], t)
    x1 = _matmul("out_proj", merged, p["w_out"], add=x)
    sv.update(y_a=y_a, y_c=y_c, y0=y0, y1=y1, y2=y2, merged=merged, x1=x1)
    h2, = _local_fwd("norm2", _f_norm, [(x1, D_MODEL, 0)], [p["norm2_g"]], [(D_MODEL, BF16)], t)
    a, r = _matmul("ff1", h2, p["w_ff1"], epilogue=lambda acc: (acc, jnp.square(jnp.maximum(acc, 0.0))),
                   out_dtypes=(BF16, BF16))
    x2 = _matmul("ff2", r, p["w_ff2"], add=x1)
    sv.update(h2=h2, a=a, r=r)
    return x2, sv


def _layer_bwd(dx2, sv, p, tabs, t, tb):
    g = {}
    da = _matmul("d_ff2", dx2, p["w_ff2_t"], extras=(sv["a"],), out_dtypes=(BF16,),
                 epilogue=lambda dr, a: (dr * (2.0 * jnp.maximum(a.astype(F32), 0.0)),))
    g["w_ff2"] = _matmul_tn("dw_ff2", sv["r"], dx2)
    dh2 = _matmul("d_ff1", da, p["w_ff1_t"])
    g["w_ff1"] = _matmul_tn("dw_ff1", sv["h2"], da)
    (dx1,), (g["norm2_g"],) = _local_bwd("norm2_bwd", _f_norm_res, [(sv["x1"], D_MODEL, 0)], [p["norm2_g"]],
                                         [dh2, dx2], t, [True], [True])
    dmerged = _matmul("d_out_proj", dx1, p["w_out_t"])
    g["w_out"] = _matmul_tn("dw_out", sv["merged"], dx1)
    (dp_g, dy0, dy1, dy2), _ = _local_bwd(
        "merge_bwd", _f_merge, [(sv["p_g"], 3 * D_MODEL, 0), (sv["y0"], D_MODEL, 0), (sv["y1"], D_MODEL, 0),
                                (sv["y2"], D_MODEL, 0)], [], [dmerged], min(t, 256), [True] * 4, [], [BF16] * 4)
    dy_a = _matmul("d_branch0", dy0, p["wb0_t"])
    do_b = _matmul("d_branch1", dy1, p["wb1_t"])
    dy_c = _matmul("d_branch2", dy2, p["wb2_t"])
    g["wb0"] = _matmul_tn("dw_branch0", sv["y_a"], dy0)
    g["wb1"] = _matmul_tn("dw_branch1", sv["o_b"], dy1)
    g["wb2"] = _matmul_tn("dw_branch2", sv["y_c"], dy2)
    p_c = sv["p_c"]
    (do_c, dc_z), (g["o_norm_g"],) = _local_bwd(
        "gdn_post_bwd", _f_gdn_post, [(sv["o_c"], A_WIDTH, 0), (p_c, C_Z, C_QKV // C_Z)], [p["o_norm_g"]], [dy_c], t,
        [True, True], [True], [F32, BF16])
    dcq, dck, dcv, dg, dbeta = _delta_bwd(sv["cq"], sv["ck"], sv["cv"], sv["g"], sv["beta"], sv["states"],
                                          _heads_first(do_c))
    dgb = jnp.pad(jnp.concatenate([_unchunk_vec(dbeta), _unchunk_vec(dg)], axis=1),
                  ((0, 0), (0, HEAD_PAD - 2 * N_HEADS)))
    (dconv, dba), (g["a_log"], g["dt_bias"]) = _local_bwd(
        "gdn_pre_bwd", _f_gdn_pre, [(sv["conv_pre"], C_QKV, 0), (p_c, HEAD_PAD, (C_QKV + C_Z) // HEAD_PAD)],
        [p["a_log"], p["dt_bias"]], [_heads_last(dcq), _heads_last(dck), _heads_last(dcv), dgb], t,
        [True, True], [True, True], [F32, BF16])
    dc_qkv, g["conv_w"] = _conv_bwd(p_c, dconv, p["conv_w"], t)
    dp_c = jnp.concatenate([dc_qkv, dc_z, dba], axis=1)
    delta, = _local_fwd("attn_delta", _f_attn_delta, [(sv["o_b"], N_HEADS * HEAD_PAD, 0), (do_b, N_HEADS * HEAD_PAD, 0)], [],
                        [(N_HEADS * HEAD_PAD, F32)], t)
    dq, dk, dv = _flash_bwd(sv["q"], sv["k"], sv["v"], do_b, _row_stats(sv["lse"]), _row_stats(delta), tb)
    mla_pars = [p["q_lat_norm_g"], p["kv_lat_norm_g"], p["wq"], p["wk"], p["wv"], p["qn_g"], p["kn_g"]]
    (dp_b,), mla_g = _local_bwd(
        "mla_prep_bwd", _f_mla_prep, [(sv["p_b"], W_B, 0)] + [(tb_, HEAD_PAD, 0) for tb_ in tabs], mla_pars,
        [dq, dk, dv], min(t, 256), [True, False, False, False], [True] * 7, [BF16])
    for name, val in zip(("q_lat_norm_g", "kv_lat_norm_g", "wq", "wk", "wv", "qn_g", "kn_g"), mla_g):
        g[name] = val
    (dp_a,), (g["sgu_norm_g"], g["w_spatial"], g["b_spatial"]) = _local_bwd(
        "sgu_bwd", _f_sgu, [(sv["p_a"], 2 * A_WIDTH, 0)], [p["sgu_norm_g"], p["w_spatial"], p["b_spatial"]], [dy_a],
        SGU_CHUNK, [True], [True] * 3, [BF16])
    dh1 = None
    for key, dp in (("a", dp_a), ("b", dp_b), ("c", dp_c), ("g", dp_g)):
        dh1 = _matmul("d_proj_" + key, dp, p["wt_" + key], add=dh1)
        g["w_" + key] = _matmul_tn("dw_proj_" + key, sv["h1"], dp)
    (dx,), (g["norm1_g"],) = _local_bwd("norm1_bwd", _f_norm_res, [(sv["x"], D_MODEL, 0)], [p["norm1_g"]],
                                        [dh1, dx1], t, [True], [True])
    return dx, _unprep_grads(g)


def _local_step(x, positions, w, target, t=512, tb=1024):
    s = x.shape[0]
    t = min(t, s)
    w = dict(w)
    for key, val in zip(W_IN_GROUPS, _assemble_w_in(w["w_in"])):
        w["w_in_" + key] = val
    half = B_ROPE // 2
    inv_freq = 1.0 / (ROPE_BASE ** (jnp.arange(half, dtype=F32) / half))
    inv_row = jnp.concatenate([jnp.zeros((B_NOPE,), F32), inv_freq, inv_freq, jnp.zeros((HEAD_PAD - B_QK,), F32)])[None, :]
    tabs = _local_fwd("rope_tables", _f_rope_tables, [(positions, 1, 0)], [inv_row], [(HEAD_PAD, F32)] * 3, t)
    preps, saved = [], []
    for l in range(DEPTH):
        preps.append(_prep_layer(w, l))
        x, sv = _layer_fwd(x, preps[l], tabs, t, tb)
        saved.append(sv)
    sq, dx = _loss_head(x, target, t)
    grads = [None] * DEPTH
    for l in reversed(range(DEPTH)):
        dx, grads[l] = _layer_bwd(dx, saved[l], preps[l], tabs, t, tb)
    stacked = lambda name: jnp.stack([grads[l][name] for l in range(DEPTH)], axis=0)
    out = {name: stacked(name) for name in WEIGHTS if name != "w_in"}
    out["w_in"] = _split_dw_in([stacked("w_in_" + key) for key in W_IN_GROUPS])
    return sq, dx, out


def _exchange(name, arrays, gather):
    n = len(arrays)

    def body(*refs):
        send, recv = refs[:n], refs[n:2 * n]
        send_sems, recv_sems, local_sems = refs[2 * n:]
        x, y, c = lax.axis_index("x"), lax.axis_index("y"), lax.axis_index("c")
        me = 4 * x + 2 * y + c

        def src(a, idx):
            return send[a] if gather[a] else send[a].at[idx]

        local = [pltpu.make_async_copy(src(a, me), recv[a].at[me], local_sems.at[a]) for a in range(n)]
        for cp in local:
            cp.start()
        copies = []
        for d in range(1, N_DEV):
            px, py, pc = x ^ ((d >> 2) & 1), y ^ ((d >> 1) & 1), c ^ (d & 1)
            peer = 4 * px + 2 * py + pc
            for a in range(n):
                cp = pltpu.make_async_remote_copy(
                    src_ref=src(a, peer), dst_ref=recv[a].at[me], send_sem=send_sems.at[a, d],
                    recv_sem=recv_sems.at[a, d], device_id=(px, py, pc), device_id_type=pl.DeviceIdType.MESH)
                cp.start()
                copies.append((cp, a, peer, d))
        for cp, a, peer, d in copies:
            cp.wait_send()
            pltpu.make_async_remote_copy(
                src_ref=src(a, peer), dst_ref=recv[a].at[peer], send_sem=send_sems.at[a, d],
                recv_sem=recv_sems.at[a, d], device_id=(x, y, c), device_id_type=pl.DeviceIdType.MESH).wait_recv()
        for cp in local:
            cp.wait()

    any_spec = pl.BlockSpec(memory_space=pl.ANY)
    return pl.pallas_call(
        body, name=name, in_specs=[any_spec] * n, out_specs=[any_spec] * n,
        out_shape=[jax.ShapeDtypeStruct(((N_DEV,) + a.shape) if gather[i] else a.shape, a.dtype)
                   for i, a in enumerate(arrays)],
        scratch_shapes=[pltpu.SemaphoreType.DMA((n, N_DEV)), pltpu.SemaphoreType.DMA((n, N_DEV)),
                        pltpu.SemaphoreType.DMA((n,))],
    )(*arrays)


REDUCE_BLOCK_ELEMS = 64 * 1024


def _reduce_adamw(name, recv, w, m, v):
    rows, cols = w.shape
    tr = rows
    while tr % 16 == 0 and tr * (-(-cols // 128) * 128) > REDUCE_BLOCK_ELEMS:
        tr //= 2
    c1, c2 = 1.0 - ADAM_B1 ** ADAM_STEP, 1.0 - ADAM_B2 ** ADAM_STEP

    def body(r_ref, w_ref, m_ref, v_ref, g_ref, d_ref, nm_ref, nv_ref):
        g = r_ref[0].astype(F32)
        for j in range(1, N_DEV):
            g = g + r_ref[j].astype(F32)
        m_new = ADAM_B1 * m_ref[...] + (1.0 - ADAM_B1) * g
        v_new = ADAM_B2 * v_ref[...] + (1.0 - ADAM_B2) * jnp.square(g)
        d_ref[...] = -ADAM_LR * ((m_new / c1) / (jnp.sqrt(v_new / c2) + ADAM_EPS) + ADAM_WD * w_ref[...])
        g_ref[...], nm_ref[...], nv_ref[...] = g, m_new, v_new

    flat = pl.BlockSpec((tr, cols), lambda i: (i, 0))
    return pl.pallas_call(
        body, name=name, grid=(rows // tr,),
        in_specs=[pl.BlockSpec((N_DEV, tr, cols), lambda i: (0, i, 0)), flat, flat, flat], out_specs=[flat] * 4,
        out_shape=[jax.ShapeDtypeStruct((rows, cols), F32)] * 4, compiler_params=_params(("parallel",)),
    )(recv, w, m, v)


PACK_ROWS = 512


def _pack(cols):
    flat = jnp.concatenate(cols, axis=-1)
    tile = PACK_ROWS * 128
    flat = _pad_to(flat, flat.ndim - 1, -(-flat.shape[-1] // tile) * tile)
    return flat.reshape(flat.shape[:-1] + (-1, 128))


def _unpack(packed, shapes, lead=()):
    flat = packed.reshape(lead + (-1,))
    out, off = [], 0
    for shp in shapes:
        n = math.prod(shp)
        out.append(flat[..., off:off + n].reshape(lead + tuple(shp)))
        off += n
    return out


def _to_shards(name, full):
    ax = SHARD_AXIS[name]
    shp = full.shape
    return jnp.moveaxis(full.reshape(shp[:ax] + (N_DEV, shp[ax] // N_DEV) + shp[ax + 1:]), ax, 0)


def _from_shards(name, shards):
    ax = SHARD_AXIS[name]
    a = jnp.moveaxis(shards, 0, ax)
    return a.reshape(a.shape[:ax] + (a.shape[ax] * a.shape[ax + 1],) + a.shape[ax + 2:])


def kernel(x, positions, norm1_g, w_in, sgu_norm_g, w_spatial, b_spatial, q_lat_norm_g, w_q_up, kv_lat_norm_g, w_kv_up, q_norm_g, k_norm_g, conv_w, a_log, dt_bias, o_norm_g, w_branch, w_out, norm2_g, w_ff1, w_ff2, loss_target, m_norm1_g, m_w_in, m_sgu_norm_g, m_w_spatial, m_b_spatial, m_q_lat_norm_g, m_w_q_up, m_kv_lat_norm_g, m_w_kv_up, m_q_norm_g, m_k_norm_g, m_conv_w, m_a_log, m_dt_bias, m_o_norm_g, m_w_branch, m_w_out, m_norm2_g, m_w_ff1, m_w_ff2, v_norm1_g, v_w_in, v_sgu_norm_g, v_w_spatial, v_b_spatial, v_q_lat_norm_g, v_w_q_up, v_kv_lat_norm_g, v_w_kv_up, v_q_norm_g, v_k_norm_g, v_conv_w, v_a_log, v_dt_bias, v_o_norm_g, v_w_branch, v_w_out, v_norm2_g, v_w_ff1, v_w_ff2):
    args = locals()
    local_w = {n: args[n] for n in WEIGHTS}
    state = (local_w, {n: args["m_" + n] for n in WEIGHTS}, {n: args["v_" + n] for n in WEIGHTS})
    wire = [local_w[n] if n in EXACT_GATHER else local_w[n].astype(BF16) for n in SHARDED]
    full = dict(local_w)
    for n, part in zip(SHARDED, _exchange("gather_weights", wire, [True] * len(wire))):
        full[n] = part if n == "w_in" else _from_shards(n, part)
    sq, grad_x, grads = _local_step(x[0], positions.reshape(-1, 1), full, loss_target[0])
    loss = lax.psum(sq[0, 0] * (0.5 / D_MODEL), ("x", "y", "c"))
    send = [grads[n] if n == "w_in" else _to_shards(n, grads[n]).astype(GRAD_WIRE) for n in SHARDED]
    rep = _pack([grads[n].reshape(-1) for n in REPLICATED])
    recv = _exchange("exchange_grads", send + [rep], [False] * len(send) + [True])
    results = {}
    for n, r in zip(SHARDED, recv):
        shp = local_w[n].shape
        flat = (math.prod(shp[:-1]), shp[-1])
        outs = _reduce_adamw("adamw_" + n, r.reshape((N_DEV,) + flat), *[src[n].reshape(flat) for src in state])
        results[n] = [o.reshape(shp) for o in outs]
    outs = _reduce_adamw("adamw_replicated", recv[-1],
                         *[_pack([src[n].reshape(-1) for n in REPLICATED]) for src in state])
    rep_shapes = [local_w[n].shape for n in REPLICATED]
    for n, vals in zip(REPLICATED, zip(*[_unpack(o, rep_shapes) for o in outs])):
        results[n] = vals
    return (loss, grad_x[None], *[results[n][k] for k in range(4) for n in WEIGHTS])
```

```python
import functools
import math

import jax
import jax.numpy as jnp
from jax import lax
from jax.experimental import pallas as pl
from jax.experimental.pallas import tpu as pltpu

F32, BF16 = jnp.float32, jnp.bfloat16
HI = lax.Precision.HIGHEST

N_DEV = 8
D_MODEL = 1024
DEPTH = 2
N_HEADS = 8
HEAD_PAD = 128
A_WIDTH = 512
B_NOPE, B_ROPE, B_VDIM = 64, 32, 64
B_QK = B_NOPE + B_ROPE
B_Q_LORA, B_KV_LORA = 384, 256
ROPE_BASE = 10000.0
C_DK = 64
C_CHUNK = 64
DELTA_SUB = 2
DELTA_BLOCK = 16
C_QKV = 1536
C_Z = 512
SGU_CHUNK = 128
D_FF = 4096
EPS = 1e-6
ADAM_LR, ADAM_B1, ADAM_B2, ADAM_EPS, ADAM_WD, ADAM_STEP = 0.001, 0.9, 0.999, 1e-08, 0.01, 10
O_QLAT, O_KVLAT, O_KROPE, O_CQKV, O_GATES, D_IN = 1024, 1408, 1664, 1696, 3760, 6832
W_B, W_C = 768, 2176
VMEM_LIMIT = 56 * 2 ** 20
NEG = -1e30

SHARDED = ("w_in", "w_q_up", "w_kv_up", "conv_w", "w_branch", "w_out", "w_ff1", "w_ff2")
EXACT_GATHER = ("conv_w",)
GRAD_WIRE = BF16
SHARD_AXIS = {"w_in": 2, "w_q_up": 2, "w_kv_up": 2, "conv_w": 2, "w_branch": 3, "w_out": 1, "w_ff1": 2, "w_ff2": 1}
REPLICATED = ("norm1_g", "sgu_norm_g", "w_spatial", "b_spatial", "q_lat_norm_g", "kv_lat_norm_g", "q_norm_g",
              "k_norm_g", "a_log", "dt_bias", "o_norm_g", "norm2_g")
WEIGHTS = ("norm1_g", "w_in", "sgu_norm_g", "w_spatial", "b_spatial", "q_lat_norm_g", "w_q_up", "kv_lat_norm_g",
           "w_kv_up", "q_norm_g", "k_norm_g", "conv_w", "a_log", "dt_bias", "o_norm_g", "w_branch", "w_out",
           "norm2_g", "w_ff1", "w_ff2")


def _params(sem, vmem=VMEM_LIMIT):
    return pltpu.CompilerParams(dimension_semantics=sem, vmem_limit_bytes=vmem)


_FORMS = {"nn": (1, 0), "nt": (1, 1), "tn": (0, 0)}


def _dot(form, a, b, batch, prec=None):
    ca, cb = _FORMS[form]
    o = 1 if batch else 0
    bd = ((0,), (0,)) if batch else ((), ())
    return lax.dot_general(a, b, (((ca + o,), (cb + o,)), bd), precision=prec, preferred_element_type=F32)


def _mm_raw(form, batch, a, b):
    return _dot(form, a.astype(BF16), b.astype(BF16), batch)


@functools.partial(jax.custom_vjp, nondiff_argnums=(0, 1))
def _mm(form, batch, a, b):
    return _mm_raw(form, batch, a, b)


def _mm_fwd(form, batch, a, b):
    return _mm_raw(form, batch, a, b), (a, b)


def _mm_bwd(form, batch, res, g):
    a, b = res
    if form == "nn":
        da, db = _mm_raw("nt", batch, g, b), _mm_raw("tn", batch, a, g)
    elif form == "nt":
        da, db = _mm_raw("nn", batch, g, b), _mm_raw("tn", batch, g, a)
    else:
        da, db = _mm_raw("nt", batch, b, g), _mm_raw("nn", batch, a, g)
    return da.astype(a.dtype), db.astype(b.dtype)


_mm.defvjp(_mm_fwd, _mm_bwd)


def _mmh(form, a, b, batch=False):
    return _dot(form, a, b, batch, HI)


@functools.partial(jax.custom_vjp, nondiff_argnums=(1, 2))
def _roll(x, shift, axis):
    return pltpu.roll(x, shift % x.shape[axis], axis)


def _roll_fwd(x, shift, axis):
    return _roll(x, shift, axis), None


def _roll_bwd(shift, axis, _, g):
    return (_roll(g, -shift, axis),)


_roll.defvjp(_roll_fwd, _roll_bwd)


@jax.custom_vjp
def _tile_heads(x):
    return jnp.concatenate([x] * N_HEADS, axis=1)


def _tile_heads_fwd(x):
    return _tile_heads(x), None


def _tile_heads_bwd(_, g):
    w = g.shape[1] // N_HEADS
    acc = g[:, :w]
    for h in range(1, N_HEADS):
        acc = acc + g[:, h * w:(h + 1) * w]
    return (acc,)


_tile_heads.defvjp(_tile_heads_fwd, _tile_heads_bwd)


def _iota(shape, dim):
    return lax.broadcasted_iota(jnp.int32, shape, dim)


def _head_indicator_t(width, per_head):
    return (_iota((N_HEADS, width), 1) // per_head == _iota((N_HEADS, width), 0)).astype(F32)


def _rms(x, g):
    return x * lax.rsqrt(jnp.mean(x * x, axis=-1, keepdims=True) + EPS) * g


def _gelu(x):
    return 0.5 * x * (1.0 + lax.erf(x * (2.0 ** -0.5)))


def _head_sums(x, per_head):
    blocks = []
    for b in range(x.shape[1] // 128):
        blk = x[:, b * 128:(b + 1) * 128]
        if per_head == 128:
            blocks.append(jnp.broadcast_to(jnp.sum(blk, axis=1, keepdims=True), blk.shape))
        else:
            low = _iota((1, 128), 1) < per_head
            s_low = jnp.sum(jnp.where(low, blk, 0.0), axis=1, keepdims=True)
            s_high = jnp.sum(jnp.where(low, 0.0, blk), axis=1, keepdims=True)
            blocks.append(jnp.where(low, s_low, s_high))
    return jnp.concatenate(blocks, axis=1)


def _head_rms(x, g_full, per_head, n_real):
    return x * lax.rsqrt(_head_sums(x * x, per_head) * (1.0 / n_real) + EPS) * g_full


def _rope(x, cos_t, sin_hi, sin_lo):
    half = B_ROPE // 2
    return (x * _tile_heads(cos_t) + _roll(x, half, 1) * _tile_heads(sin_hi)
            + _roll(x, -half, 1) * _tile_heads(sin_lo))


def _tok_spec(t, width, col):
    return pl.BlockSpec((t, width), lambda i, c=col: (i, c))


def _par_spec(shape):
    nd = len(shape)
    return pl.BlockSpec(shape, lambda i: (0,) * nd)


def _local_fwd(name, f, toks, pars, outs, t):
    s = toks[0][0].shape[0]
    nt, npar = len(toks), len(pars)

    def body(*refs):
        vals = [r[...] for r in refs[:nt + npar]]
        for r, o in zip(refs[nt + npar:], f(*vals)):
            r[...] = o.astype(r.dtype)

    return pl.pallas_call(
        body, name=name, grid=(s // t,),
        in_specs=[_tok_spec(t, w, c) for _, w, c in toks] + [_par_spec(p.shape) for p in pars],
        out_specs=[_tok_spec(t, w, 0) for w, _ in outs],
        out_shape=[jax.ShapeDtypeStruct((s, w), dt) for w, dt in outs],
        compiler_params=_params(("parallel",)),
    )(*[a for a, _, _ in toks], *pars)


def _local_bwd(name, f, toks, pars, cots, t, tok_diff, par_diff, grad_dtypes=None):
    s = toks[0][0].shape[0]
    nt, npar, nc = len(toks), len(pars), len(cots)
    dt_idx = [k for k in range(nt) if tok_diff[k]]
    dp_idx = [k for k in range(npar) if par_diff[k]]
    grad_dtypes = grad_dtypes or [F32] * len(dt_idx)

    def body(*refs):
        i = pl.program_id(0)
        tv = [r[...] for r in refs[:nt]]
        pv = [r[...] for r in refs[nt:nt + npar]]
        cv = [r[...].astype(F32) for r in refs[nt + npar:nt + npar + nc]]
        out_refs = refs[nt + npar + nc:]

        def g(*d):
            tt, pp = list(tv), list(pv)
            for k, val in zip(dt_idx, d[:len(dt_idx)]):
                tt[k] = val
            for k, val in zip(dp_idx, d[len(dt_idx):]):
                pp[k] = val
            return tuple(o.astype(F32) for o in f(*tt, *pp))

        _, vjp = jax.vjp(g, *[tv[k] for k in dt_idx], *[pv[k] for k in dp_idx])
        grads = vjp(tuple(cv))
        for r, gr in zip(out_refs[:len(dt_idx)], grads[:len(dt_idx)]):
            r[...] = gr.astype(r.dtype)
        par_refs = out_refs[len(dt_idx):]

        @pl.when(i == 0)
        def _():
            for r in par_refs:
                r[...] = jnp.zeros(r.shape, r.dtype)

        for r, gr in zip(par_refs, grads[len(dt_idx):]):
            r[...] += gr.astype(F32)

    res = pl.pallas_call(
        body, name=name, grid=(s // t,),
        in_specs=([_tok_spec(t, w, c) for _, w, c in toks] + [_par_spec(p.shape) for p in pars]
                  + [_tok_spec(t, c.shape[1], 0) for c in cots]),
        out_specs=([_tok_spec(t, toks[k][1], 0) for k in dt_idx] + [_par_spec(pars[k].shape) for k in dp_idx]),
        out_shape=([jax.ShapeDtypeStruct((s, toks[k][1]), dt) for k, dt in zip(dt_idx, grad_dtypes)]
                   + [jax.ShapeDtypeStruct(pars[k].shape, F32) for k in dp_idx]),
        compiler_params=_params(("arbitrary",)),
    )(*[a for a, _, _ in toks], *pars, *cots)
    return res[:len(dt_idx)], res[len(dt_idx):]


def _f_norm(x, g):
    return (_rms(x, g),)


def _f_norm_res(x, g):
    return _rms(x, g), x


def _f_rope_tables(pos, inv_row):
    ang = pos.astype(F32) * inv_row
    lane = _iota(ang.shape, 1)
    sn = jnp.sin(ang)
    half = B_ROPE // 2
    sin_hi = jnp.where((lane >= B_NOPE + half) & (lane < B_QK), sn, 0.0)
    sin_lo = jnp.where((lane >= B_NOPE) & (lane < B_NOPE + half), -sn, 0.0)
    return jnp.cos(ang), sin_hi, sin_lo


def _f_sgu(p_a, g, w_s, b_s):
    u = _gelu(p_a[:, :A_WIDTH])
    v = _rms(_gelu(p_a[:, A_WIDTH:]), g)
    tril = _iota((SGU_CHUNK, SGU_CHUNK), 1) <= _iota((SGU_CHUNK, SGU_CHUNK), 0)
    w_cat = jnp.concatenate([jnp.where(tril, w_s[gi], 0.0) for gi in range(N_HEADS)], axis=1)
    group = _iota((1, A_WIDTH), 1) // (A_WIDTH // N_HEADS)
    v_stack = jnp.concatenate([jnp.where(group == gi, v, 0.0) for gi in range(N_HEADS)], axis=0)
    bias = _mmh("tn", b_s, _head_indicator_t(A_WIDTH, A_WIDTH // N_HEADS))
    return (u * (_mm("nn", False, w_cat, v_stack) + bias),)


def _f_mla_prep(p_b, cos_t, sin_hi, sin_lo, q_lat_g, kv_lat_g, wq, wk, wv, qn_g, kn_g):
    kv_lat, k_rope, q_lat = p_b[:, :B_KV_LORA], p_b[:, B_KV_LORA:B_KV_LORA + HEAD_PAD], p_b[:, B_KV_LORA + HEAD_PAD:]
    q = _mm("nn", False, _rms(q_lat, q_lat_g), wq)
    q = _rope(_head_rms(q, _tile_heads(qn_g), HEAD_PAD, B_QK), cos_t, sin_hi, sin_lo)
    kvn = _rms(kv_lat, kv_lat_g)
    k = _mm("nn", False, kvn, wk) + _tile_heads(k_rope)
    k = _rope(_head_rms(k, _tile_heads(kn_g), HEAD_PAD, B_QK), cos_t, sin_hi, sin_lo)
    return q, k, _mm("nn", False, kvn, wv)


def _f_gdn_pre(conv_pre, ba, a_log_row, dt_row):
    qkv = jax.nn.silu(conv_pre)

    def l2(x):
        return x * lax.rsqrt(_head_sums(x * x, C_DK) + EPS)

    lane = _iota(ba.shape, 1)
    g = -jnp.exp(a_log_row) * jax.nn.softplus(ba + dt_row)
    gb = jnp.where(lane < N_HEADS, jax.nn.sigmoid(ba), jnp.where(lane < 2 * N_HEADS, g, 0.0))
    return l2(qkv[:, :A_WIDTH]), l2(qkv[:, A_WIDTH:2 * A_WIDTH]), qkv[:, 2 * A_WIDTH:], gb


def _f_gdn_post(o, c_z, o_g):
    place = (_iota((C_DK, A_WIDTH), 1) % C_DK == _iota((C_DK, A_WIDTH), 0)).astype(F32)
    return (_head_rms(o, _mmh("nn", o_g, place), C_DK, C_DK) * jax.nn.silu(c_z),)


def _f_merge(p_g, y0, y1, y2):
    d = D_MODEL
    p_g, y0, y1, y2 = (a.astype(F32) for a in (p_g, y0, y1, y2))
    return (jax.nn.sigmoid(p_g[:, :d]) * y0 + jax.nn.sigmoid(p_g[:, d:2 * d]) * y1
            + jax.nn.sigmoid(p_g[:, 2 * d:]) * y2,)


def _pick(n, whole_up_to, candidates):
    if n <= whole_up_to:
        return n
    for c in candidates:
        if n % c == 0:
            return c
    return n


def _matmul(name, a, w, add=None, out_dtype=F32, tm=512, extras=(), epilogue=None, out_dtypes=None):
    m, k = a.shape
    n = w.shape[1]
    tm = min(tm, m)
    tn = _pick(n, 2304, (2048, 1536, 1024, 512))
    if add is not None:
        extras, epilogue = (add,), lambda r, x: (r + x,)
    if epilogue is None:
        epilogue = lambda r: (r,)
    out_dtypes = out_dtypes or (out_dtype,)
    n_ex, n_out = len(extras), len(out_dtypes)

    def body(*refs):
        a_ref, w_ref = refs[0], refs[1]
        ex_refs, o_refs = refs[2:2 + n_ex], refs[2 + n_ex:2 + n_ex + n_out]
        total = jnp.dot(a_ref[...].astype(BF16), w_ref[...].astype(BF16), preferred_element_type=F32)
        for o_ref, r in zip(o_refs, epilogue(total, *[e[...] for e in ex_refs])):
            o_ref[...] = r.astype(o_ref.dtype)

    tile = pl.BlockSpec((tm, tn), lambda j, i: (i, j))
    res = pl.pallas_call(
        body, name=name, grid=(n // tn, m // tm),
        in_specs=[pl.BlockSpec((tm, k), lambda j, i: (i, 0)), pl.BlockSpec((k, tn), lambda j, i: (0, j))] + [tile] * n_ex,
        out_specs=[tile] * n_out, out_shape=[jax.ShapeDtypeStruct((m, n), dt) for dt in out_dtypes],
        compiler_params=_params(("parallel", "parallel")),
    )(a, w, *extras)
    return res[0] if n_out == 1 else res


def _matmul_tn(name, a, b, a_col=None, tm=1024):
    m = a.shape[0]
    k, acol = (a.shape[1], 0) if a_col is None else a_col
    n = b.shape[1]
    tm = min(tm, m)
    tk = _pick(k, 1536, (1024, 512))
    tn = _pick(n, 2304, (1024, 512))
    nm = m // tm

    def body(a_ref, b_ref, o_ref):
        mm = pl.program_id(2)
        part = lax.dot_general(a_ref[...].astype(BF16), b_ref[...].astype(BF16), (((0,), (0,)), ((), ())),
                               preferred_element_type=F32)

        @pl.when(mm == 0)
        def _():
            o_ref[...] = part

        @pl.when(mm > 0)
        def _():
            o_ref[...] += part

    kb = k // tk
    return pl.pallas_call(
        body, name=name, grid=(kb, n // tn, nm),
        in_specs=[pl.BlockSpec((tm, tk), lambda i, j, mm: (mm, acol * kb + i)),
                  pl.BlockSpec((tm, tn), lambda i, j, mm: (mm, j))],
        out_specs=pl.BlockSpec((tk, tn), lambda i, j, mm: (i, j)),
        out_shape=jax.ShapeDtypeStruct((k, n), F32),
        compiler_params=_params(("parallel", "parallel", "arbitrary")),
    )(a, b)


def _shift_down(x, prev, s):
    rolled = pltpu.roll(x, s, 0)
    pr = pltpu.roll(prev, s, 0)
    head = jnp.where(_iota((8, 1), 0) < s, pr, rolled[:8])
    return jnp.concatenate([head, rolled[8:]], axis=0)


def _shift_up(x, nxt, s):
    t = x.shape[0]
    rolled = pltpu.roll(x, t - s, 0)
    nr = pltpu.roll(nxt, 8 - s, 0)
    tail = jnp.where(_iota((8, 1), 0) >= 8 - s, nr, rolled[t - 8:])
    return jnp.concatenate([rolled[:t - 8], tail], axis=0)


def _conv_fwd(p_c, w8, t):
    s = p_c.shape[0]
    t = min(t, s)
    r = t // 8

    def body(x_ref, prev_ref, w_ref, o_ref):
        i = pl.program_id(0)
        x = x_ref[...]
        prev = jnp.where(i == 0, 0.0, prev_ref[...])
        acc = w_ref[3:4, :] * x
        for sh in range(1, 4):
            acc = acc + w_ref[3 - sh:4 - sh, :] * _shift_down(x, prev, sh)
        o_ref[...] = acc

    return pl.pallas_call(
        body, name="conv_fwd", grid=(s // t,),
        in_specs=[pl.BlockSpec((t, C_QKV), lambda i: (i, 0)),
                  pl.BlockSpec((8, C_QKV), lambda i: (jnp.maximum(i * r - 1, 0), 0)),
                  pl.BlockSpec((8, C_QKV), lambda i: (0, 0))],
        out_specs=pl.BlockSpec((t, C_QKV), lambda i: (i, 0)),
        out_shape=jax.ShapeDtypeStruct((s, C_QKV), F32),
        compiler_params=_params(("parallel",)),
    )(p_c, p_c, w8)


def _conv_bwd(p_c, dy, w8, t):
    s = p_c.shape[0]
    t = min(t, s)
    r = t // 8
    n = s // t

    def body(x_ref, prev_ref, dy_ref, next_ref, w_ref, dx_ref, dw_ref):
        i = pl.program_id(0)
        x, g = x_ref[...], dy_ref[...]
        prev = jnp.where(i == 0, 0.0, prev_ref[...])
        nxt = jnp.where(i == n - 1, 0.0, next_ref[...])

        @pl.when(i == 0)
        def _():
            dw_ref[...] = jnp.zeros(dw_ref.shape, F32)

        dx = w_ref[3:4, :] * g
        dw_ref[3:4, :] += jnp.sum(g * x, axis=0, keepdims=True)
        for sh in range(1, 4):
            dx = dx + w_ref[3 - sh:4 - sh, :] * _shift_up(g, nxt, sh)
            dw_ref[3 - sh:4 - sh, :] += jnp.sum(g * _shift_down(x, prev, sh), axis=0, keepdims=True)
        dx_ref[...] = dx.astype(dx_ref.dtype)

    return pl.pallas_call(
        body, name="conv_bwd", grid=(n,),
        in_specs=[pl.BlockSpec((t, C_QKV), lambda i: (i, 0)),
                  pl.BlockSpec((8, C_QKV), lambda i: (jnp.maximum(i * r - 1, 0), 0)),
                  pl.BlockSpec((t, C_QKV), lambda i: (i, 0)),
                  pl.BlockSpec((8, C_QKV), lambda i: (jnp.minimum((i + 1) * r, s // 8 - 1), 0)),
                  pl.BlockSpec((8, C_QKV), lambda i: (0, 0))],
        out_specs=[pl.BlockSpec((t, C_QKV), lambda i: (i, 0)), pl.BlockSpec((8, C_QKV), lambda i: (0, 0))],
        out_shape=[jax.ShapeDtypeStruct((s, C_QKV), BF16), jax.ShapeDtypeStruct((8, C_QKV), F32)],
        compiler_params=_params(("arbitrary",)),
    )(p_c, p_c, dy, dy, w8)


def _delta_chunks(state, q, k, v, g, beta):
    pre = [_delta_pre(q[:, i * C_CHUNK:(i + 1) * C_CHUNK], k[:, i * C_CHUNK:(i + 1) * C_CHUNK],
                      v[:, i * C_CHUNK:(i + 1) * C_CHUNK], g[i], beta[i]) for i in range(DELTA_SUB)]
    outs = []
    for qg, intra, u, w, k_dec, dec in pre:
        v_new = u - _mm("nn", True, w, state)
        outs.append(_mm("nn", True, qg, state) + _mm("nn", True, intra, v_new))
        state = state * dec + _mm("tn", True, k_dec, v_new)
    return state, jnp.concatenate(outs, axis=1)


def _delta_pre(q, k, v, g, beta):
    c = C_CHUNK
    row, col = _iota((c, c), 0), _iota((c, c), 1)
    tril, strict = col <= row, col < row
    gc = _mmh("nn", g, (row <= col).astype(F32))
    g_last = jnp.sum(g, axis=1, keepdims=True)
    qs = q * (C_DK ** -0.5)
    decay = jnp.exp(jnp.where(tril, gc[:, :, None] - gc[:, None, :], NEG))
    k_beta, v_beta = k * beta[:, :, None], v * beta[:, :, None]
    x = -jnp.where(strict, _mm("nt", True, k_beta, k) * decay, 0.0)
    xd = jnp.where(row // DELTA_BLOCK == col // DELTA_BLOCK, x, 0.0)
    powers = [xd]
    for _ in range(3):
        powers.append(_mm("nn", True, powers[-1], powers[-1]))
    sol = jnp.concatenate([x - xd, v_beta, k_beta * jnp.exp(gc)[:, :, None]], axis=2)
    for p in reversed(powers):
        sol = sol + _mm("nn", True, p, sol)
    y, sol = sol[:, :, :c], sol[:, :, c:]
    sol = sol + _mm("nn", True, _mm("nn", True, y, y), sol)
    sol = sol + _mm("nn", True, y, sol)
    u, w = sol[:, :, :C_DK], sol[:, :, C_DK:]
    intra = jnp.where(tril, _mm("nt", True, qs, k) * decay, 0.0)
    return (qs * jnp.exp(gc)[:, :, None], intra, u, w, k * jnp.exp(g_last - gc)[:, :, None],
            jnp.exp(g_last)[:, :, None])


def _delta_specs(n, rev):
    def at(i):
        return n - 1 - i if rev else i
    tok = pl.BlockSpec((N_HEADS, DELTA_SUB * C_CHUNK, C_DK), lambda i: (0, at(i), 0))
    vec = pl.BlockSpec((DELTA_SUB, N_HEADS, C_CHUNK), lambda i: (at(i), 0, 0))
    st = pl.BlockSpec((1, N_HEADS, C_DK, C_DK), lambda i: (at(i), 0, 0, 0))
    return tok, vec, st


def _delta_fwd(q, k, v, g, beta):
    s = q.shape[1]
    n = s // (DELTA_SUB * C_CHUNK)
    tok, vec, st = _delta_specs(n, False)

    def body(q_ref, k_ref, v_ref, g_ref, b_ref, o_ref, st_ref, state):
        @pl.when(pl.program_id(0) == 0)
        def _():
            state[...] = jnp.zeros(state.shape, F32)

        cur = state[...]
        st_ref[0] = cur
        new, o = _delta_chunks(cur, q_ref[...], k_ref[...], v_ref[...], g_ref[...], b_ref[...])
        o_ref[...] = o
        state[...] = new

    return pl.pallas_call(
        body, name="delta_fwd", grid=(n,), in_specs=[tok, tok, tok, vec, vec], out_specs=[tok, st],
        out_shape=[jax.ShapeDtypeStruct((N_HEADS, s, C_DK), F32), jax.ShapeDtypeStruct((n, N_HEADS, C_DK, C_DK), F32)],
        scratch_shapes=[pltpu.VMEM((N_HEADS, C_DK, C_DK), F32)],
        compiler_params=_params(("arbitrary",)),
    )(q, k, v, g, beta)


def _delta_bwd(q, k, v, g, beta, states, do):
    s = q.shape[1]
    n = s // (DELTA_SUB * C_CHUNK)
    tok, vec, st = _delta_specs(n, True)

    def body(q_ref, k_ref, v_ref, g_ref, b_ref, st_ref, do_ref, dq_ref, dk_ref, dv_ref, dg_ref, db_ref, dstate):
        @pl.when(pl.program_id(0) == 0)
        def _():
            dstate[...] = jnp.zeros(dstate.shape, F32)

        _, vjp = jax.vjp(_delta_chunks, st_ref[0], q_ref[...], k_ref[...], v_ref[...], g_ref[...], b_ref[...])
        dst, dq, dk, dv, dg, db = vjp((dstate[...], do_ref[...]))
        dq_ref[...], dk_ref[...], dv_ref[...] = dq, dk, dv
        dg_ref[...], db_ref[...] = dg, db
        dstate[...] = dst

    tok_shape = jax.ShapeDtypeStruct((N_HEADS, s, C_DK), F32)
    vec_shape = jax.ShapeDtypeStruct((n * DELTA_SUB, N_HEADS, C_CHUNK), F32)
    return pl.pallas_call(
        body, name="delta_bwd", grid=(n,), in_specs=[tok, tok, tok, vec, vec, st, tok],
        out_specs=[tok, tok, tok, vec, vec], out_shape=[tok_shape, tok_shape, tok_shape, vec_shape, vec_shape],
        scratch_shapes=[pltpu.VMEM((N_HEADS, C_DK, C_DK), F32)],
        compiler_params=_params(("arbitrary",)),
    )(q, k, v, g, beta, states, do)


ATT_SCALE = B_QK ** -0.5


SCORE_SCALE_LOG2 = ATT_SCALE * math.log2(math.e)


def _lanes(x, n):
    return x if n == 1 else jnp.concatenate([x] * n, axis=1)


def _scores(a_ref, b_ref):
    return lax.dot_general(a_ref[...], b_ref[...], (((1,), (1,)), ((), ())), preferred_element_type=F32) * SCORE_SCALE_LOG2


def _flash_fwd(q, k, v, tb):
    s = q.shape[0]
    tb = min(tb, s)
    nb = s // tb
    nrep = tb // 128

    def body(q_ref, k_ref, v_ref, o_ref, lse_ref, m_s, l_s, acc):
        i, j = pl.program_id(1), pl.program_id(2)

        @pl.when(j == 0)
        def _():
            m_s[...] = jnp.full(m_s.shape, -jnp.inf, F32)
            l_s[...] = jnp.zeros(l_s.shape, F32)
            acc[...] = jnp.zeros(acc.shape, F32)

        def step(masked):
            sc = _scores(q_ref, k_ref)
            if masked:
                sc = jnp.where(_iota((tb, tb), 1) <= _iota((tb, tb), 0), sc, -jnp.inf)
            m_prev = m_s[...]
            m_new = jnp.maximum(m_prev, jnp.max(sc, axis=1, keepdims=True))
            alpha = jnp.exp2(m_prev - m_new)
            p = jnp.exp2(sc - _lanes(m_new, nrep))
            l_s[...] = alpha * l_s[...] + jnp.sum(p, axis=1, keepdims=True)
            acc[...] = alpha * acc[...] + jnp.dot(p.astype(BF16), v_ref[...], preferred_element_type=F32)
            m_s[...] = m_new

        @pl.when(j < i)
        def _():
            step(False)

        @pl.when(j == i)
        def _():
            step(True)
            o_ref[...] = acc[...] / l_s[...]
            lse_ref[...] = m_s[...] + jnp.log2(l_s[...])

    qs = pl.BlockSpec((tb, HEAD_PAD), lambda h, i, j: (i, h))
    ks = pl.BlockSpec((tb, HEAD_PAD), lambda h, i, j: (jnp.minimum(j, i), h))
    shape = jax.ShapeDtypeStruct((s, N_HEADS * HEAD_PAD), F32)
    return pl.pallas_call(
        body, name="flash_fwd", grid=(N_HEADS, nb, nb), in_specs=[qs, ks, ks],
        out_specs=[qs, qs], out_shape=[shape, shape],
        scratch_shapes=[pltpu.VMEM((tb, 128), F32), pltpu.VMEM((tb, 128), F32), pltpu.VMEM((tb, HEAD_PAD), F32)],
        compiler_params=_params(("parallel", "parallel", "arbitrary")),
    )(q, k, v)


def _f_attn_delta(o, do):
    return (_head_sums(o * do, HEAD_PAD),)


def _row_stats(rep):
    return rep[:, ::HEAD_PAD].T.reshape(N_HEADS, 1, -1)


def _flash_bwd(q, k, v, do, lse_row, delta_row, tb):
    s = q.shape[0]
    tb = min(tb, s)
    nb = s // tb

    def body(q_ref, k_ref, v_ref, do_ref, lse_ref, dl_ref, dq_ref, dk_ref, dv_ref, dk_acc, dv_acc):
        j, i = pl.program_id(1), pl.program_id(2)

        @pl.when((i == 0) & (j == 0))
        def _():
            dq_ref[...] = jnp.zeros(dq_ref.shape, F32)

        @pl.when(i == 0)
        def _():
            dk_acc[...] = jnp.zeros(dk_acc.shape, F32)
            dv_acc[...] = jnp.zeros(dv_acc.shape, F32)

        def step(masked):
            st = _scores(k_ref, q_ref)
            if masked:
                st = jnp.where(_iota((tb, tb), 0) <= _iota((tb, tb), 1), st, -jnp.inf)
            do = do_ref[...].astype(BF16)
            pt = jnp.exp2(st - lse_ref[0])
            dpt = lax.dot_general(v_ref[...], do, (((1,), (1,)), ((), ())), preferred_element_type=F32)
            dst = (pt * (dpt - dl_ref[0])).astype(BF16)
            dv_acc[...] += jnp.dot(pt.astype(BF16), do, preferred_element_type=F32)
            dk_acc[...] += jnp.dot(dst, q_ref[...], preferred_element_type=F32)
            rows = pl.ds(pl.multiple_of(i * tb, tb), tb)
            dq_ref[rows, :] += lax.dot_general(dst, k_ref[...], (((0,), (0,)), ((), ())),
                                               preferred_element_type=F32) * ATT_SCALE

        @pl.when(i == j)
        def _():
            step(True)

        @pl.when(i > j)
        def _():
            step(False)

        @pl.when(i == nb - 1)
        def _():
            dk_ref[...] = dk_acc[...] * ATT_SCALE
            dv_ref[...] = dv_acc[...]

    qs = pl.BlockSpec((tb, HEAD_PAD), lambda h, j, i: (jnp.maximum(i, j), h))
    ks = pl.BlockSpec((tb, HEAD_PAD), lambda h, j, i: (j, h))
    rs = pl.BlockSpec((1, 1, tb), lambda h, j, i: (h, 0, jnp.maximum(i, j)))
    shape = jax.ShapeDtypeStruct((s, N_HEADS * HEAD_PAD), F32)
    return pl.pallas_call(
        body, name="flash_bwd", grid=(N_HEADS, nb, nb), in_specs=[qs, ks, ks, qs, rs, rs],
        out_specs=[pl.BlockSpec((s, HEAD_PAD), lambda h, j, i: (0, h)), ks, ks], out_shape=[shape, shape, shape],
        scratch_shapes=[pltpu.VMEM((tb, HEAD_PAD), F32), pltpu.VMEM((tb, HEAD_PAD), F32)],
        compiler_params=_params(("parallel", "arbitrary", "arbitrary")),
    )(q, k, v, do, lse_row, delta_row)


def _loss_head(y, target, t):
    s, d = y.shape
    t = min(t, s)

    def body(y_ref, t_ref, sum_ref, dy_ref):
        @pl.when(pl.program_id(0) == 0)
        def _():
            sum_ref[...] = jnp.zeros(sum_ref.shape, F32)

        err = y_ref[...] - t_ref[...]
        dy_ref[...] = err * (1.0 / d)
        sum_ref[...] += jnp.broadcast_to(jnp.sum(err * err), sum_ref.shape)

    return pl.pallas_call(
        body, name="loss_head", grid=(s // t,),
        in_specs=[pl.BlockSpec((t, d), lambda i: (i, 0))] * 2,
        out_specs=[pl.BlockSpec((1, 128), lambda i: (0, 0)), pl.BlockSpec((t, d), lambda i: (i, 0))],
        out_shape=[jax.ShapeDtypeStruct((1, 128), F32), jax.ShapeDtypeStruct((s, d), F32)],
        compiler_params=_params(("arbitrary",)),
    )(y, target)


def _pad_to(a, axis, size):
    pad = [(0, 0)] * a.ndim
    pad[axis] = (0, size - a.shape[axis])
    return jnp.pad(a, pad)


W_IN_GROUPS = {
    "a": [(0, O_QLAT)],
    "b": [(O_KVLAT, O_KROPE), B_NOPE, (O_KROPE, O_CQKV), HEAD_PAD - B_QK, (O_QLAT, O_KVLAT)],
    "c": [(O_CQKV, O_GATES), W_C - (O_GATES - O_CQKV)],
    "g": [(O_GATES, D_IN)],
}
W_IN_SHARD = D_IN // N_DEV
W_IN_ROWS = 128


def _group_width(key):
    return sum(e if isinstance(e, int) else e[1] - e[0] for e in W_IN_GROUPS[key])


def _assemble_w_in(shards):
    depth = shards.shape[1]
    keys = list(W_IN_GROUPS)

    def body(s_ref, *o_refs):
        vals = [s_ref[k, 0] for k in range(N_DEV)]
        for key, o_ref in zip(keys, o_refs):
            parts = []
            for e in W_IN_GROUPS[key]:
                if isinstance(e, int):
                    parts.append(jnp.zeros((W_IN_ROWS, e), shards.dtype))
                    continue
                lo, hi = e
                while lo < hi:
                    k = lo // W_IN_SHARD
                    end = min(hi, (k + 1) * W_IN_SHARD)
                    parts.append(vals[k][:, lo - k * W_IN_SHARD:end - k * W_IN_SHARD])
                    lo = end
            o_ref[0] = parts[0] if len(parts) == 1 else jnp.concatenate(parts, axis=1)

    return pl.pallas_call(
        body, name="assemble_w_in", grid=(depth, D_MODEL // W_IN_ROWS),
        in_specs=[pl.BlockSpec((N_DEV, 1, W_IN_ROWS, W_IN_SHARD), lambda l, i: (0, l, i, 0))],
        out_specs=[pl.BlockSpec((1, W_IN_ROWS, _group_width(key)), lambda l, i: (l, i, 0)) for key in keys],
        out_shape=[jax.ShapeDtypeStruct((depth, D_MODEL, _group_width(key)), shards.dtype) for key in keys],
        compiler_params=_params(("parallel", "parallel")),
    )(shards)


def _split_dw_in(groups):
    keys = list(W_IN_GROUPS)
    depth = groups[0].shape[0]
    runs = []
    for gi, key in enumerate(keys):
        col = 0
        for e in W_IN_GROUPS[key]:
            if not isinstance(e, int):
                runs.append((e[0], e[1], gi, col))
            col += e if isinstance(e, int) else e[1] - e[0]
    runs.sort()

    def body(*refs):
        vals = [r[0] for r in refs[:len(keys)]]
        o_ref = refs[len(keys)]
        for k in range(N_DEV):
            lo, hi = k * W_IN_SHARD, (k + 1) * W_IN_SHARD
            parts = []
            for a, b, gi, col in runs:
                s, e = max(a, lo), min(b, hi)
                if s < e:
                    parts.append(vals[gi][:, col + s - a:col + e - a])
            o_ref[k, 0] = jnp.concatenate(parts, axis=1).astype(GRAD_WIRE)

    return pl.pallas_call(
        body, name="split_dw_in", grid=(depth, D_MODEL // W_IN_ROWS),
        in_specs=[pl.BlockSpec((1, W_IN_ROWS, _group_width(key)), lambda l, i: (l, i, 0)) for key in keys],
        out_specs=pl.BlockSpec((N_DEV, 1, W_IN_ROWS, W_IN_SHARD), lambda l, i: (0, l, i, 0)),
        out_shape=jax.ShapeDtypeStruct((N_DEV, depth, D_MODEL, W_IN_SHARD), GRAD_WIRE),
        compiler_params=_params(("parallel", "parallel")),
    )(*groups)


def _prep_layer(w, l):
    p = {}
    for key in W_IN_GROUPS:
        val = w["w_in_" + key][l].astype(BF16)
        p["w_" + key] = val
        p["wt_" + key] = val.T
    for name in ("norm1_g", "sgu_norm_g", "q_lat_norm_g", "kv_lat_norm_g", "o_norm_g", "norm2_g"):
        p[name] = w[name][l][None, :]
    p["w_spatial"], p["b_spatial"] = w["w_spatial"][l], w["b_spatial"][l]
    p["wq"] = _pad_to(w["w_q_up"][l].astype(F32).reshape(B_Q_LORA, N_HEADS, B_QK), 2, HEAD_PAD).reshape(B_Q_LORA, -1)
    kv = w["w_kv_up"][l].astype(F32).reshape(B_KV_LORA, N_HEADS, B_NOPE + B_VDIM)
    p["wk"] = _pad_to(kv[:, :, :B_NOPE], 2, HEAD_PAD).reshape(B_KV_LORA, -1)
    p["wv"] = _pad_to(kv[:, :, B_NOPE:], 2, HEAD_PAD).reshape(B_KV_LORA, -1)
    p["qn_g"] = _pad_to(w["q_norm_g"][l][None, :], 1, HEAD_PAD)
    p["kn_g"] = _pad_to(w["k_norm_g"][l][None, :], 1, HEAD_PAD)
    p["conv_w"] = _pad_to(w["conv_w"][l], 0, 8)
    row = lambda v: jnp.pad(v[None, :], ((0, 0), (N_HEADS, HEAD_PAD - 2 * N_HEADS)))
    p["a_log"], p["dt_bias"] = row(w["a_log"][l]), row(w["dt_bias"][l])
    wb = w["w_branch"][l]
    wb1 = _pad_to(wb[1].reshape(N_HEADS, B_VDIM, D_MODEL), 1, HEAD_PAD).reshape(-1, D_MODEL)
    for key, val in (("wb0", wb[0]), ("wb1", wb1), ("wb2", wb[2]), ("w_out", w["w_out"][l]),
                     ("w_ff1", w["w_ff1"][l]), ("w_ff2", w["w_ff2"][l])):
        p[key] = val.astype(BF16)
        p[key + "_t"] = val.T.astype(BF16)
    return p


def _unprep_grads(g):
    out = {"w_in_" + key: g["w_" + key] for key in W_IN_GROUPS}
    for name in ("norm1_g", "sgu_norm_g", "q_lat_norm_g", "kv_lat_norm_g", "o_norm_g", "norm2_g"):
        out[name] = g[name][0]
    out["w_spatial"], out["b_spatial"] = g["w_spatial"], g["b_spatial"]
    out["w_q_up"] = g["wq"].reshape(B_Q_LORA, N_HEADS, HEAD_PAD)[:, :, :B_QK].reshape(B_Q_LORA, -1)
    gk = g["wk"].reshape(B_KV_LORA, N_HEADS, HEAD_PAD)[:, :, :B_NOPE]
    gv = g["wv"].reshape(B_KV_LORA, N_HEADS, HEAD_PAD)[:, :, :B_VDIM]
    out["w_kv_up"] = jnp.concatenate([gk, gv], axis=2).reshape(B_KV_LORA, -1)
    out["q_norm_g"], out["k_norm_g"] = g["qn_g"][0, :B_QK], g["kn_g"][0, :B_QK]
    out["conv_w"] = g["conv_w"][:4]
    out["a_log"], out["dt_bias"] = g["a_log"][0, N_HEADS:2 * N_HEADS], g["dt_bias"][0, N_HEADS:2 * N_HEADS]
    gb1 = g["wb1"].reshape(N_HEADS, HEAD_PAD, D_MODEL)[:, :B_VDIM].reshape(-1, D_MODEL)
    out["w_branch"] = jnp.stack([g["wb0"], gb1, g["wb2"]], axis=0)
    out["w_out"], out["w_ff1"], out["w_ff2"] = g["w_out"], g["w_ff1"], g["w_ff2"]
    return out


def _heads_first(a):
    s = a.shape[0]
    return a.reshape(s, N_HEADS, C_DK).transpose(1, 0, 2)


def _heads_last(a):
    return a.transpose(1, 0, 2).reshape(a.shape[1], N_HEADS * C_DK)


def _chunk_vec(a):
    return a.reshape(-1, C_CHUNK, N_HEADS).transpose(0, 2, 1)


def _unchunk_vec(a):
    return a.transpose(0, 2, 1).reshape(-1, N_HEADS)


def _layer_fwd(x, p, tabs, t, tb):
    sv = {"x": x}
    h1, = _local_fwd("norm1", _f_norm, [(x, D_MODEL, 0)], [p["norm1_g"]], [(D_MODEL, BF16)], t)
    sv["h1"] = h1
    p_a, p_b, p_c = (_matmul("proj_" + key, h1, p["w_" + key]) for key in "abc")
    p_g = _matmul("proj_g", h1, p["w_g"], out_dtype=BF16)
    sv.update(p_a=p_a, p_b=p_b, p_c=p_c, p_g=p_g)
    y_a, = _local_fwd("sgu", _f_sgu, [(p_a, 2 * A_WIDTH, 0)], [p["sgu_norm_g"], p["w_spatial"], p["b_spatial"]],
                      [(A_WIDTH, BF16)], SGU_CHUNK)
    q, k, v = _local_fwd("mla_prep", _f_mla_prep, [(p_b, W_B, 0)] + [(tb_, HEAD_PAD, 0) for tb_ in tabs],
                         [p["q_lat_norm_g"], p["kv_lat_norm_g"], p["wq"], p["wk"], p["wv"], p["qn_g"], p["kn_g"]],
                         [(N_HEADS * HEAD_PAD, BF16)] * 3, min(t, 256))
    o_b, lse = _flash_fwd(q, k, v, tb)
    sv.update(q=q, k=k, v=v, o_b=o_b, lse=lse)
    conv_pre = _conv_fwd(p_c, p["conv_w"], t)
    cq, ck, cv, gb = _local_fwd("gdn_pre", _f_gdn_pre, [(conv_pre, C_QKV, 0), (p_c, HEAD_PAD, (C_QKV + C_Z) // HEAD_PAD)],
                                [p["a_log"], p["dt_bias"]], [(A_WIDTH, F32)] * 3 + [(HEAD_PAD, F32)], t)
    cq, ck, cv = _heads_first(cq), _heads_first(ck), _heads_first(cv)
    beta, g = _chunk_vec(gb[:, :N_HEADS]), _chunk_vec(gb[:, N_HEADS:2 * N_HEADS])
    o_c, states = _delta_fwd(cq, ck, cv, g, beta)
    o_c = _heads_last(o_c)
    sv.update(conv_pre=conv_pre, cq=cq, ck=ck, cv=cv, beta=beta, g=g, states=states, o_c=o_c)
    y_c, = _local_fwd("gdn_post", _f_gdn_post, [(o_c, A_WIDTH, 0), (p_c, C_Z, C_QKV // C_Z)], [p["o_norm_g"]],
                      [(A_WIDTH, BF16)], t)
    y0 = _matmul("branch0", y_a, p["wb0"], out_dtype=BF16)
    y1 = _matmul("branch1", o_b, p["wb1"], out_dtype=BF16)
    y2 = _matmul("branch2", y_c, p["wb2"], out_dtype=BF16)
    merged, = _local_fwd("merge", _f_merge, [(p_g, 3 * D_MODEL, 0), (y0, D_MODEL, 0), (y1, D_MODEL, 0), (y2, D_MODEL, 0)],
                         [], [(D_MODEL, BF16)], t)
    x1 = _matmul("out_proj", merged, p["w_out"], add=x)
    sv.update(y_a=y_a, y_c=y_c, y0=y0, y1=y1, y2=y2, merged=merged, x1=x1)
    h2, = _local_fwd("norm2", _f_norm, [(x1, D_MODEL, 0)], [p["norm2_g"]], [(D_MODEL, BF16)], t)
    a, r = _matmul("ff1", h2, p["w_ff1"], epilogue=lambda acc: (acc, jnp.square(jnp.maximum(acc, 0.0))),
                   out_dtypes=(BF16, BF16))
    x2 = _matmul("ff2", r, p["w_ff2"], add=x1)
    sv.update(h2=h2, a=a, r=r)
    return x2, sv


def _layer_bwd(dx2, sv, p, tabs, t, tb):
    g = {}
    da = _matmul("d_ff2", dx2, p["w_ff2_t"], extras=(sv["a"],), out_dtypes=(BF16,),
                 epilogue=lambda dr, a: (dr * (2.0 * jnp.maximum(a.astype(F32), 0.0)),))
    g["w_ff2"] = _matmul_tn("dw_ff2", sv["r"], dx2)
    dh2 = _matmul("d_ff1", da, p["w_ff1_t"])
    g["w_ff1"] = _matmul_tn("dw_ff1", sv["h2"], da)
    (dx1,), (g["norm2_g"],) = _local_bwd("norm2_bwd", _f_norm_res, [(sv["x1"], D_MODEL, 0)], [p["norm2_g"]],
                                         [dh2, dx2], t, [True], [True])
    dmerged = _matmul("d_out_proj", dx1, p["w_out_t"])
    g["w_out"] = _matmul_tn("dw_out", sv["merged"], dx1)
    (dp_g, dy0, dy1, dy2), _ = _local_bwd(
        "merge_bwd", _f_merge, [(sv["p_g"], 3 * D_MODEL, 0), (sv["y0"], D_MODEL, 0), (sv["y1"], D_MODEL, 0),
                                (sv["y2"], D_MODEL, 0)], [], [dmerged], min(t, 256), [True] * 4, [], [BF16] * 4)
    dy_a = _matmul("d_branch0", dy0, p["wb0_t"])
    do_b = _matmul("d_branch1", dy1, p["wb1_t"])
    dy_c = _matmul("d_branch2", dy2, p["wb2_t"])
    g["wb0"] = _matmul_tn("dw_branch0", sv["y_a"], dy0)
    g["wb1"] = _matmul_tn("dw_branch1", sv["o_b"], dy1)
    g["wb2"] = _matmul_tn("dw_branch2", sv["y_c"], dy2)
    p_c = sv["p_c"]
    (do_c, dc_z), (g["o_norm_g"],) = _local_bwd(
        "gdn_post_bwd", _f_gdn_post, [(sv["o_c"], A_WIDTH, 0), (p_c, C_Z, C_QKV // C_Z)], [p["o_norm_g"]], [dy_c], t,
        [True, True], [True], [F32, BF16])
    dcq, dck, dcv, dg, dbeta = _delta_bwd(sv["cq"], sv["ck"], sv["cv"], sv["g"], sv["beta"], sv["states"],
                                          _heads_first(do_c))
    dgb = jnp.pad(jnp.concatenate([_unchunk_vec(dbeta), _unchunk_vec(dg)], axis=1),
                  ((0, 0), (0, HEAD_PAD - 2 * N_HEADS)))
    (dconv, dba), (g["a_log"], g["dt_bias"]) = _local_bwd(
        "gdn_pre_bwd", _f_gdn_pre, [(sv["conv_pre"], C_QKV, 0), (p_c, HEAD_PAD, (C_QKV + C_Z) // HEAD_PAD)],
        [p["a_log"], p["dt_bias"]], [_heads_last(dcq), _heads_last(dck), _heads_last(dcv), dgb], t,
        [True, True], [True, True], [F32, BF16])
    dc_qkv, g["conv_w"] = _conv_bwd(p_c, dconv, p["conv_w"], t)
    dp_c = jnp.concatenate([dc_qkv, dc_z, dba], axis=1)
    delta, = _local_fwd("attn_delta", _f_attn_delta, [(sv["o_b"], N_HEADS * HEAD_PAD, 0), (do_b, N_HEADS * HEAD_PAD, 0)], [],
                        [(N_HEADS * HEAD_PAD, F32)], t)
    dq, dk, dv = _flash_bwd(sv["q"], sv["k"], sv["v"], do_b, _row_stats(sv["lse"]), _row_stats(delta), tb)
    mla_pars = [p["q_lat_norm_g"], p["kv_lat_norm_g"], p["wq"], p["wk"], p["wv"], p["qn_g"], p["kn_g"]]
    (dp_b,), mla_g = _local_bwd(
        "mla_prep_bwd", _f_mla_prep, [(sv["p_b"], W_B, 0)] + [(tb_, HEAD_PAD, 0) for tb_ in tabs], mla_pars,
        [dq, dk, dv], min(t, 256), [True, False, False, False], [True] * 7, [BF16])
    for name, val in zip(("q_lat_norm_g", "kv_lat_norm_g", "wq", "wk", "wv", "qn_g", "kn_g"), mla_g):
        g[name] = val
    (dp_a,), (g["sgu_norm_g"], g["w_spatial"], g["b_spatial"]) = _local_bwd(
        "sgu_bwd", _f_sgu, [(sv["p_a"], 2 * A_WIDTH, 0)], [p["sgu_norm_g"], p["w_spatial"], p["b_spatial"]], [dy_a],
        SGU_CHUNK, [True], [True] * 3, [BF16])
    dh1 = None
    for key, dp in (("a", dp_a), ("b", dp_b), ("c", dp_c), ("g", dp_g)):
        dh1 = _matmul("d_proj_" + key, dp, p["wt_" + key], add=dh1)
        g["w_" + key] = _matmul_tn("dw_proj_" + key, sv["h1"], dp)
    (dx,), (g["norm1_g"],) = _local_bwd("norm1_bwd", _f_norm_res, [(sv["x"], D_MODEL, 0)], [p["norm1_g"]],
                                        [dh1, dx1], t, [True], [True])
    return dx, _unprep_grads(g)


def _local_step(x, positions, w, target, t=512, tb=1024):
    s = x.shape[0]
    t = min(t, s)
    w = dict(w)
    for key, val in zip(W_IN_GROUPS, _assemble_w_in(w["w_in"])):
        w["w_in_" + key] = val
    half = B_ROPE // 2
    inv_freq = 1.0 / (ROPE_BASE ** (jnp.arange(half, dtype=F32) / half))
    inv_row = jnp.concatenate([jnp.zeros((B_NOPE,), F32), inv_freq, inv_freq, jnp.zeros((HEAD_PAD - B_QK,), F32)])[None, :]
    tabs = _local_fwd("rope_tables", _f_rope_tables, [(positions, 1, 0)], [inv_row], [(HEAD_PAD, F32)] * 3, t)
    preps, saved = [], []
    for l in range(DEPTH):
        preps.append(_prep_layer(w, l))
        x, sv = _layer_fwd(x, preps[l], tabs, t, tb)
        saved.append(sv)
    sq, dx = _loss_head(x, target, t)
    grads = [None] * DEPTH
    for l in reversed(range(DEPTH)):
        dx, grads[l] = _layer_bwd(dx, saved[l], preps[l], tabs, t, tb)
    stacked = lambda name: jnp.stack([grads[l][name] for l in range(DEPTH)], axis=0)
    out = {name: stacked(name) for name in WEIGHTS if name != "w_in"}
    out["w_in"] = _split_dw_in([stacked("w_in_" + key) for key in W_IN_GROUPS])
    return sq, dx, out


def _exchange(name, arrays, gather):
    n = len(arrays)

    def body(*refs):
        send, recv = refs[:n], refs[n:2 * n]
        send_sems, recv_sems, local_sems = refs[2 * n:]
        x, y, c = lax.axis_index("x"), lax.axis_index("y"), lax.axis_index("c")
        me = 4 * x + 2 * y + c

        def src(a, idx):
            return send[a] if gather[a] else send[a].at[idx]

        local = [pltpu.make_async_copy(src(a, me), recv[a].at[me], local_sems.at[a]) for a in range(n)]
        for cp in local:
            cp.start()
        copies = []
        for d in range(1, N_DEV):
            px, py, pc = x ^ ((d >> 2) & 1), y ^ ((d >> 1) & 1), c ^ (d & 1)
            peer = 4 * px + 2 * py + pc
            for a in range(n):
                cp = pltpu.make_async_remote_copy(
                    src_ref=src(a, peer), dst_ref=recv[a].at[me], send_sem=send_sems.at[a, d],
                    recv_sem=recv_sems.at[a, d], device_id=(px, py, pc), device_id_type=pl.DeviceIdType.MESH)
                cp.start()
                copies.append((cp, a, peer, d))
        for cp, a, peer, d in copies:
            cp.wait_send()
            pltpu.make_async_remote_copy(
                src_ref=src(a, peer), dst_ref=recv[a].at[peer], send_sem=send_sems.at[a, d],
                recv_sem=recv_sems.at[a, d], device_id=(x, y, c), device_id_type=pl.DeviceIdType.MESH).wait_recv()
        for cp in local:
            cp.wait()

    any_spec = pl.BlockSpec(memory_space=pl.ANY)
    return pl.pallas_call(
        body, name=name, in_specs=[any_spec] * n, out_specs=[any_spec] * n,
        out_shape=[jax.ShapeDtypeStruct(((N_DEV,) + a.shape) if gather[i] else a.shape, a.dtype)
                   for i, a in enumerate(arrays)],
        scratch_shapes=[pltpu.SemaphoreType.DMA((n, N_DEV)), pltpu.SemaphoreType.DMA((n, N_DEV)),
                        pltpu.SemaphoreType.DMA((n,))],
    )(*arrays)


REDUCE_BLOCK_ELEMS = 64 * 1024


def _reduce_adamw(name, recv, w, m, v):
    rows, cols = w.shape
    tr = rows
    while tr % 16 == 0 and tr * (-(-cols // 128) * 128) > REDUCE_BLOCK_ELEMS:
        tr //= 2
    c1, c2 = 1.0 - ADAM_B1 ** ADAM_STEP, 1.0 - ADAM_B2 ** ADAM_STEP

    def body(r_ref, w_ref, m_ref, v_ref, g_ref, d_ref, nm_ref, nv_ref):
        g = r_ref[0].astype(F32)
        for j in range(1, N_DEV):
            g = g + r_ref[j].astype(F32)
        m_new = ADAM_B1 * m_ref[...] + (1.0 - ADAM_B1) * g
        v_new = ADAM_B2 * v_ref[...] + (1.0 - ADAM_B2) * jnp.square(g)
        d_ref[...] = -ADAM_LR * ((m_new / c1) / (jnp.sqrt(v_new / c2) + ADAM_EPS) + ADAM_WD * w_ref[...])
        g_ref[...], nm_ref[...], nv_ref[...] = g, m_new, v_new

    flat = pl.BlockSpec((tr, cols), lambda i: (i, 0))
    return pl.pallas_call(
        body, name=name, grid=(rows // tr,),
        in_specs=[pl.BlockSpec((N_DEV, tr, cols), lambda i: (0, i, 0)), flat, flat, flat], out_specs=[flat] * 4,
        out_shape=[jax.ShapeDtypeStruct((rows, cols), F32)] * 4, compiler_params=_params(("parallel",)),
    )(recv, w, m, v)


PACK_ROWS = 512


def _pack(cols):
    flat = jnp.concatenate(cols, axis=-1)
    tile = PACK_ROWS * 128
    flat = _pad_to(flat, flat.ndim - 1, -(-flat.shape[-1] // tile) * tile)
    return flat.reshape(flat.shape[:-1] + (-1, 128))


def _unpack(packed, shapes, lead=()):
    flat = packed.reshape(lead + (-1,))
    out, off = [], 0
    for shp in shapes:
        n = math.prod(shp)
        out.append(flat[..., off:off + n].reshape(lead + tuple(shp)))
        off += n
    return out


def _to_shards(name, full):
    ax = SHARD_AXIS[name]
    shp = full.shape
    return jnp.moveaxis(full.reshape(shp[:ax] + (N_DEV, shp[ax] // N_DEV) + shp[ax + 1:]), ax, 0)


def _from_shards(name, shards):
    ax = SHARD_AXIS[name]
    a = jnp.moveaxis(shards, 0, ax)
    return a.reshape(a.shape[:ax] + (a.shape[ax] * a.shape[ax + 1],) + a.shape[ax + 2:])


def kernel(x, positions, norm1_g, w_in, sgu_norm_g, w_spatial, b_spatial, q_lat_norm_g, w_q_up, kv_lat_norm_g, w_kv_up, q_norm_g, k_norm_g, conv_w, a_log, dt_bias, o_norm_g, w_branch, w_out, norm2_g, w_ff1, w_ff2, loss_target, m_norm1_g, m_w_in, m_sgu_norm_g, m_w_spatial, m_b_spatial, m_q_lat_norm_g, m_w_q_up, m_kv_lat_norm_g, m_w_kv_up, m_q_norm_g, m_k_norm_g, m_conv_w, m_a_log, m_dt_bias, m_o_norm_g, m_w_branch, m_w_out, m_norm2_g, m_w_ff1, m_w_ff2, v_norm1_g, v_w_in, v_sgu_norm_g, v_w_spatial, v_b_spatial, v_q_lat_norm_g, v_w_q_up, v_kv_lat_norm_g, v_w_kv_up, v_q_norm_g, v_k_norm_g, v_conv_w, v_a_log, v_dt_bias, v_o_norm_g, v_w_branch, v_w_out, v_norm2_g, v_w_ff1, v_w_ff2):
    args = locals()
    local_w = {n: args[n] for n in WEIGHTS}
    state = (local_w, {n: args["m_" + n] for n in WEIGHTS}, {n: args["v_" + n] for n in WEIGHTS})
    wire = [local_w[n] if n in EXACT_GATHER else local_w[n].astype(BF16) for n in SHARDED]
    full = dict(local_w)
    for n, part in zip(SHARDED, _exchange("gather_weights", wire, [True] * len(wire))):
        full[n] = part if n == "w_in" else _from_shards(n, part)
    sq, grad_x, grads = _local_step(x[0], positions.reshape(-1, 1), full, loss_target[0])
    loss = lax.psum(sq[0, 0] * (0.5 / D_MODEL), ("x", "y", "c"))
    send = [grads[n] if n == "w_in" else _to_shards(n, grads[n]).astype(GRAD_WIRE) for n in SHARDED]
    rep = _pack([grads[n].reshape(-1) for n in REPLICATED])
    recv = _exchange("exchange_grads", send + [rep], [False] * len(send) + [True])
    results = {}
    for n, r in zip(SHARDED, recv):
        shp = local_w[n].shape
        flat = (math.prod(shp[:-1]), shp[-1])
        outs = _reduce_adamw("adamw_" + n, r.reshape((N_DEV,) + flat), *[src[n].reshape(flat) for src in state])
        results[n] = [o.reshape(shp) for o in outs]
    outs = _reduce_adamw("adamw_replicated", recv[-1],
                         *[_pack([src[n].reshape(-1) for n in REPLICATED]) for src in state])
    rep_shapes = [local_w[n].shape for n in REPLICATED]
    for n, vals in zip(REPLICATED, zip(*[_unpack(o, rep_shapes) for o in outs])):
        results[n] = vals
    return (loss, grad_x[None], *[results[n][k] for k in range(4) for n in WEIGHTS])
```

```python
import functools
import math

import jax
import jax.numpy as jnp
from jax import lax
from jax.experimental import pallas as pl
from jax.experimental.pallas import tpu as pltpu

F32, BF16 = jnp.float32, jnp.bfloat16
HI = lax.Precision.HIGHEST

N_DEV = 8
D_MODEL = 1024
DEPTH = 2
N_HEADS = 8
HEAD_PAD = 128
A_WIDTH = 512
B_NOPE, B_ROPE, B_VDIM = 64, 32, 64
B_QK = B_NOPE + B_ROPE
B_Q_LORA, B_KV_LORA = 384, 256
ROPE_BASE = 10000.0
C_DK = 64
C_CHUNK = 64
DELTA_SUB = 4
DELTA_BLOCK = 16
C_QKV = 1536
C_Z = 512
SGU_CHUNK = 128
D_FF = 4096
EPS = 1e-6
ADAM_LR, ADAM_B1, ADAM_B2, ADAM_EPS, ADAM_WD, ADAM_STEP = 0.001, 0.9, 0.999, 1e-08, 0.01, 10
O_QLAT, O_KVLAT, O_KROPE, O_CQKV, O_GATES, D_IN = 1024, 1408, 1664, 1696, 3760, 6832
W_B, W_C = 768, 2176
VMEM_LIMIT = 56 * 2 ** 20
NEG = -1e30

SHARDED = ("w_in", "w_q_up", "w_kv_up", "conv_w", "w_branch", "w_out", "w_ff1", "w_ff2")
EXACT_GATHER = ("conv_w",)
GRAD_WIRE = BF16
SHARD_AXIS = {"w_in": 2, "w_q_up": 2, "w_kv_up": 2, "conv_w": 2, "w_branch": 3, "w_out": 1, "w_ff1": 2, "w_ff2": 1}
REPLICATED = ("norm1_g", "sgu_norm_g", "w_spatial", "b_spatial", "q_lat_norm_g", "kv_lat_norm_g", "q_norm_g",
              "k_norm_g", "a_log", "dt_bias", "o_norm_g", "norm2_g")
WEIGHTS = ("norm1_g", "w_in", "sgu_norm_g", "w_spatial", "b_spatial", "q_lat_norm_g", "w_q_up", "kv_lat_norm_g",
           "w_kv_up", "q_norm_g", "k_norm_g", "conv_w", "a_log", "dt_bias", "o_norm_g", "w_branch", "w_out",
           "norm2_g", "w_ff1", "w_ff2")


def _params(sem, vmem=VMEM_LIMIT):
    return pltpu.CompilerParams(dimension_semantics=sem, vmem_limit_bytes=vmem)


_FORMS = {"nn": (1, 0), "nt": (1, 1), "tn": (0, 0)}


def _dot(form, a, b, batch, prec=None):
    ca, cb = _FORMS[form]
    o = 1 if batch else 0
    bd = ((0,), (0,)) if batch else ((), ())
    return lax.dot_general(a, b, (((ca + o,), (cb + o,)), bd), precision=prec, preferred_element_type=F32)


def _mm_raw(form, batch, a, b):
    return _dot(form, a.astype(BF16), b.astype(BF16), batch)


@functools.partial(jax.custom_vjp, nondiff_argnums=(0, 1))
def _mm(form, batch, a, b):
    return _mm_raw(form, batch, a, b)


def _mm_fwd(form, batch, a, b):
    return _mm_raw(form, batch, a, b), (a, b)


def _mm_bwd(form, batch, res, g):
    a, b = res
    if form == "nn":
        da, db = _mm_raw("nt", batch, g, b), _mm_raw("tn", batch, a, g)
    elif form == "nt":
        da, db = _mm_raw("nn", batch, g, b), _mm_raw("tn", batch, g, a)
    else:
        da, db = _mm_raw("nt", batch, b, g), _mm_raw("nn", batch, a, g)
    return da.astype(a.dtype), db.astype(b.dtype)


_mm.defvjp(_mm_fwd, _mm_bwd)


def _mmh(form, a, b, batch=False):
    return _dot(form, a, b, batch, HI)


@functools.partial(jax.custom_vjp, nondiff_argnums=(1, 2))
def _roll(x, shift, axis):
    return pltpu.roll(x, shift % x.shape[axis], axis)


def _roll_fwd(x, shift, axis):
    return _roll(x, shift, axis), None


def _roll_bwd(shift, axis, _, g):
    return (_roll(g, -shift, axis),)


_roll.defvjp(_roll_fwd, _roll_bwd)


@jax.custom_vjp
def _tile_heads(x):
    return jnp.concatenate([x] * N_HEADS, axis=1)


def _tile_heads_fwd(x):
    return _tile_heads(x), None


def _tile_heads_bwd(_, g):
    w = g.shape[1] // N_HEADS
    acc = g[:, :w]
    for h in range(1, N_HEADS):
        acc = acc + g[:, h * w:(h + 1) * w]
    return (acc,)


_tile_heads.defvjp(_tile_heads_fwd, _tile_heads_bwd)


def _iota(shape, dim):
    return lax.broadcasted_iota(jnp.int32, shape, dim)


def _head_indicator_t(width, per_head):
    return (_iota((N_HEADS, width), 1) // per_head == _iota((N_HEADS, width), 0)).astype(F32)


def _rms(x, g):
    return x * lax.rsqrt(jnp.mean(x * x, axis=-1, keepdims=True) + EPS) * g


def _gelu(x):
    return 0.5 * x * (1.0 + lax.erf(x * (2.0 ** -0.5)))


def _head_sums(x, per_head):
    blocks = []
    for b in range(x.shape[1] // 128):
        blk = x[:, b * 128:(b + 1) * 128]
        if per_head == 128:
            blocks.append(jnp.broadcast_to(jnp.sum(blk, axis=1, keepdims=True), blk.shape))
        else:
            low = _iota((1, 128), 1) < per_head
            s_low = jnp.sum(jnp.where(low, blk, 0.0), axis=1, keepdims=True)
            s_high = jnp.sum(jnp.where(low, 0.0, blk), axis=1, keepdims=True)
            blocks.append(jnp.where(low, s_low, s_high))
    return jnp.concatenate(blocks, axis=1)


def _head_rms(x, g_full, per_head, n_real):
    return x * lax.rsqrt(_head_sums(x * x, per_head) * (1.0 / n_real) + EPS) * g_full


def _rope(x, cos_t, sin_hi, sin_lo):
    half = B_ROPE // 2
    return (x * _tile_heads(cos_t) + _roll(x, half, 1) * _tile_heads(sin_hi)
            + _roll(x, -half, 1) * _tile_heads(sin_lo))


def _tok_spec(t, width, col):
    return pl.BlockSpec((t, width), lambda i, c=col: (i, c))


def _par_spec(shape):
    nd = len(shape)
    return pl.BlockSpec(shape, lambda i: (0,) * nd)


def _local_fwd(name, f, toks, pars, outs, t):
    s = toks[0][0].shape[0]
    nt, npar = len(toks), len(pars)

    def body(*refs):
        vals = [r[...] for r in refs[:nt + npar]]
        for r, o in zip(refs[nt + npar:], f(*vals)):
            r[...] = o.astype(r.dtype)

    return pl.pallas_call(
        body, name=name, grid=(s // t,),
        in_specs=[_tok_spec(t, w, c) for _, w, c in toks] + [_par_spec(p.shape) for p in pars],
        out_specs=[_tok_spec(t, w, 0) for w, _ in outs],
        out_shape=[jax.ShapeDtypeStruct((s, w), dt) for w, dt in outs],
        compiler_params=_params(("parallel",)),
    )(*[a for a, _, _ in toks], *pars)


def _local_bwd(name, f, toks, pars, cots, t, tok_diff, par_diff, grad_dtypes=None):
    s = toks[0][0].shape[0]
    nt, npar, nc = len(toks), len(pars), len(cots)
    dt_idx = [k for k in range(nt) if tok_diff[k]]
    dp_idx = [k for k in range(npar) if par_diff[k]]
    grad_dtypes = grad_dtypes or [F32] * len(dt_idx)

    def body(*refs):
        i = pl.program_id(0)
        tv = [r[...] for r in refs[:nt]]
        pv = [r[...] for r in refs[nt:nt + npar]]
        cv = [r[...].astype(F32) for r in refs[nt + npar:nt + npar + nc]]
        out_refs = refs[nt + npar + nc:]

        def g(*d):
            tt, pp = list(tv), list(pv)
            for k, val in zip(dt_idx, d[:len(dt_idx)]):
                tt[k] = val
            for k, val in zip(dp_idx, d[len(dt_idx):]):
                pp[k] = val
            return tuple(o.astype(F32) for o in f(*tt, *pp))

        _, vjp = jax.vjp(g, *[tv[k] for k in dt_idx], *[pv[k] for k in dp_idx])
        grads = vjp(tuple(cv))
        for r, gr in zip(out_refs[:len(dt_idx)], grads[:len(dt_idx)]):
            r[...] = gr.astype(r.dtype)
        par_refs = out_refs[len(dt_idx):]

        @pl.when(i == 0)
        def _():
            for r in par_refs:
                r[...] = jnp.zeros(r.shape, r.dtype)

        for r, gr in zip(par_refs, grads[len(dt_idx):]):
            r[...] += gr.astype(F32)

    res = pl.pallas_call(
        body, name=name, grid=(s // t,),
        in_specs=([_tok_spec(t, w, c) for _, w, c in toks] + [_par_spec(p.shape) for p in pars]
                  + [_tok_spec(t, c.shape[1], 0) for c in cots]),
        out_specs=([_tok_spec(t, toks[k][1], 0) for k in dt_idx] + [_par_spec(pars[k].shape) for k in dp_idx]),
        out_shape=([jax.ShapeDtypeStruct((s, toks[k][1]), dt) for k, dt in zip(dt_idx, grad_dtypes)]
                   + [jax.ShapeDtypeStruct(pars[k].shape, F32) for k in dp_idx]),
        compiler_params=_params(("arbitrary",)),
    )(*[a for a, _, _ in toks], *pars, *cots)
    return res[:len(dt_idx)], res[len(dt_idx):]


def _f_norm(x, g):
    return (_rms(x, g),)


def _f_norm_res(x, g):
    return _rms(x, g), x


def _f_rope_tables(pos, inv_row):
    ang = pos.astype(F32) * inv_row
    lane = _iota(ang.shape, 1)
    sn = jnp.sin(ang)
    half = B_ROPE // 2
    sin_hi = jnp.where((lane >= B_NOPE + half) & (lane < B_QK), sn, 0.0)
    sin_lo = jnp.where((lane >= B_NOPE) & (lane < B_NOPE + half), -sn, 0.0)
    return jnp.cos(ang), sin_hi, sin_lo


def _f_sgu(p_a, g, w_s, b_s):
    u = _gelu(p_a[:, :A_WIDTH])
    v = _rms(_gelu(p_a[:, A_WIDTH:]), g)
    tril = _iota((SGU_CHUNK, SGU_CHUNK), 1) <= _iota((SGU_CHUNK, SGU_CHUNK), 0)
    w_cat = jnp.concatenate([jnp.where(tril, w_s[gi], 0.0) for gi in range(N_HEADS)], axis=1)
    group = _iota((1, A_WIDTH), 1) // (A_WIDTH // N_HEADS)
    v_stack = jnp.concatenate([jnp.where(group == gi, v, 0.0) for gi in range(N_HEADS)], axis=0)
    bias = _mmh("tn", b_s, _head_indicator_t(A_WIDTH, A_WIDTH // N_HEADS))
    return (u * (_mm("nn", False, w_cat, v_stack) + bias),)


def _f_mla_prep(p_b, cos_t, sin_hi, sin_lo, q_lat_g, kv_lat_g, wq, wk, wv, qn_g, kn_g):
    kv_lat, k_rope, q_lat = p_b[:, :B_KV_LORA], p_b[:, B_KV_LORA:B_KV_LORA + HEAD_PAD], p_b[:, B_KV_LORA + HEAD_PAD:]
    q = _mm("nn", False, _rms(q_lat, q_lat_g), wq)
    q = _rope(_head_rms(q, _tile_heads(qn_g), HEAD_PAD, B_QK), cos_t, sin_hi, sin_lo)
    kvn = _rms(kv_lat, kv_lat_g)
    k = _mm("nn", False, kvn, wk) + _tile_heads(k_rope)
    k = _rope(_head_rms(k, _tile_heads(kn_g), HEAD_PAD, B_QK), cos_t, sin_hi, sin_lo)
    return q, k, _mm("nn", False, kvn, wv)


def _f_gdn_pre(conv_pre, ba, a_log_row, dt_row):
    qkv = jax.nn.silu(conv_pre)

    def l2(x):
        return x * lax.rsqrt(_head_sums(x * x, C_DK) + EPS)

    lane = _iota(ba.shape, 1)
    g = -jnp.exp(a_log_row) * jax.nn.softplus(ba + dt_row)
    gb = jnp.where(lane < N_HEADS, jax.nn.sigmoid(ba), jnp.where(lane < 2 * N_HEADS, g, 0.0))
    return l2(qkv[:, :A_WIDTH]), l2(qkv[:, A_WIDTH:2 * A_WIDTH]), qkv[:, 2 * A_WIDTH:], gb


def _f_gdn_post(o, c_z, o_g):
    place = (_iota((C_DK, A_WIDTH), 1) % C_DK == _iota((C_DK, A_WIDTH), 0)).astype(F32)
    return (_head_rms(o, _mmh("nn", o_g, place), C_DK, C_DK) * jax.nn.silu(c_z),)


def _f_merge(p_g, y0, y1, y2):
    d = D_MODEL
    p_g, y0, y1, y2 = (a.astype(F32) for a in (p_g, y0, y1, y2))
    return (jax.nn.sigmoid(p_g[:, :d]) * y0 + jax.nn.sigmoid(p_g[:, d:2 * d]) * y1
            + jax.nn.sigmoid(p_g[:, 2 * d:]) * y2,)


def _pick(n, whole_up_to, candidates):
    if n <= whole_up_to:
        return n
    for c in candidates:
        if n % c == 0:
            return c
    return n


def _matmul(name, a, w, add=None, out_dtype=F32, tm=512, extras=(), epilogue=None, out_dtypes=None):
    m, k = a.shape
    n = w.shape[1]
    tm = min(tm, m)
    tn = _pick(n, 2304, (2048, 1536, 1024, 512))
    if add is not None:
        extras, epilogue = (add,), lambda r, x: (r + x,)
    if epilogue is None:
        epilogue = lambda r: (r,)
    out_dtypes = out_dtypes or (out_dtype,)
    n_ex, n_out = len(extras), len(out_dtypes)

    def body(*refs):
        a_ref, w_ref = refs[0], refs[1]
        ex_refs, o_refs = refs[2:2 + n_ex], refs[2 + n_ex:2 + n_ex + n_out]
        total = jnp.dot(a_ref[...].astype(BF16), w_ref[...].astype(BF16), preferred_element_type=F32)
        for o_ref, r in zip(o_refs, epilogue(total, *[e[...] for e in ex_refs])):
            o_ref[...] = r.astype(o_ref.dtype)

    tile = pl.BlockSpec((tm, tn), lambda j, i: (i, j))
    res = pl.pallas_call(
        body, name=name, grid=(n // tn, m // tm),
        in_specs=[pl.BlockSpec((tm, k), lambda j, i: (i, 0)), pl.BlockSpec((k, tn), lambda j, i: (0, j))] + [tile] * n_ex,
        out_specs=[tile] * n_out, out_shape=[jax.ShapeDtypeStruct((m, n), dt) for dt in out_dtypes],
        compiler_params=_params(("parallel", "parallel")),
    )(a, w, *extras)
    return res[0] if n_out == 1 else res


def _matmul_tn(name, a, b, a_col=None, tm=1024):
    m = a.shape[0]
    k, acol = (a.shape[1], 0) if a_col is None else a_col
    n = b.shape[1]
    tm = min(tm, m)
    tk = _pick(k, 1536, (1024, 512))
    tn = _pick(n, 2304, (1024, 512))
    nm = m // tm

    def body(a_ref, b_ref, o_ref):
        mm = pl.program_id(2)
        part = lax.dot_general(a_ref[...].astype(BF16), b_ref[...].astype(BF16), (((0,), (0,)), ((), ())),
                               preferred_element_type=F32)

        @pl.when(mm == 0)
        def _():
            o_ref[...] = part

        @pl.when(mm > 0)
        def _():
            o_ref[...] += part

    kb = k // tk
    return pl.pallas_call(
        body, name=name, grid=(kb, n // tn, nm),
        in_specs=[pl.BlockSpec((tm, tk), lambda i, j, mm: (mm, acol * kb + i)),
                  pl.BlockSpec((tm, tn), lambda i, j, mm: (mm, j))],
        out_specs=pl.BlockSpec((tk, tn), lambda i, j, mm: (i, j)),
        out_shape=jax.ShapeDtypeStruct((k, n), F32),
        compiler_params=_params(("parallel", "parallel", "arbitrary")),
    )(a, b)


def _shift_down(x, prev, s):
    rolled = pltpu.roll(x, s, 0)
    pr = pltpu.roll(prev, s, 0)
    head = jnp.where(_iota((8, 1), 0) < s, pr, rolled[:8])
    return jnp.concatenate([head, rolled[8:]], axis=0)


def _shift_up(x, nxt, s):
    t = x.shape[0]
    rolled = pltpu.roll(x, t - s, 0)
    nr = pltpu.roll(nxt, 8 - s, 0)
    tail = jnp.where(_iota((8, 1), 0) >= 8 - s, nr, rolled[t - 8:])
    return jnp.concatenate([rolled[:t - 8], tail], axis=0)


def _conv_fwd(p_c, w8, t):
    s = p_c.shape[0]
    t = min(t, s)
    r = t // 8

    def body(x_ref, prev_ref, w_ref, o_ref):
        i = pl.program_id(0)
        x = x_ref[...]
        prev = jnp.where(i == 0, 0.0, prev_ref[...])
        acc = w_ref[3:4, :] * x
        for sh in range(1, 4):
            acc = acc + w_ref[3 - sh:4 - sh, :] * _shift_down(x, prev, sh)
        o_ref[...] = acc

    return pl.pallas_call(
        body, name="conv_fwd", grid=(s // t,),
        in_specs=[pl.BlockSpec((t, C_QKV), lambda i: (i, 0)),
                  pl.BlockSpec((8, C_QKV), lambda i: (jnp.maximum(i * r - 1, 0), 0)),
                  pl.BlockSpec((8, C_QKV), lambda i: (0, 0))],
        out_specs=pl.BlockSpec((t, C_QKV), lambda i: (i, 0)),
        out_shape=jax.ShapeDtypeStruct((s, C_QKV), F32),
        compiler_params=_params(("parallel",)),
    )(p_c, p_c, w8)


def _conv_bwd(p_c, dy, w8, t):
    s = p_c.shape[0]
    t = min(t, s)
    r = t // 8
    n = s // t

    def body(x_ref, prev_ref, dy_ref, next_ref, w_ref, dx_ref, dw_ref):
        i = pl.program_id(0)
        x, g = x_ref[...], dy_ref[...]
        prev = jnp.where(i == 0, 0.0, prev_ref[...])
        nxt = jnp.where(i == n - 1, 0.0, next_ref[...])

        @pl.when(i == 0)
        def _():
            dw_ref[...] = jnp.zeros(dw_ref.shape, F32)

        dx = w_ref[3:4, :] * g
        dw_ref[3:4, :] += jnp.sum(g * x, axis=0, keepdims=True)
        for sh in range(1, 4):
            dx = dx + w_ref[3 - sh:4 - sh, :] * _shift_up(g, nxt, sh)
            dw_ref[3 - sh:4 - sh, :] += jnp.sum(g * _shift_down(x, prev, sh), axis=0, keepdims=True)
        dx_ref[...] = dx.astype(dx_ref.dtype)

    return pl.pallas_call(
        body, name="conv_bwd", grid=(n,),
        in_specs=[pl.BlockSpec((t, C_QKV), lambda i: (i, 0)),
                  pl.BlockSpec((8, C_QKV), lambda i: (jnp.maximum(i * r - 1, 0), 0)),
                  pl.BlockSpec((t, C_QKV), lambda i: (i, 0)),
                  pl.BlockSpec((8, C_QKV), lambda i: (jnp.minimum((i + 1) * r, s // 8 - 1), 0)),
                  pl.BlockSpec((8, C_QKV), lambda i: (0, 0))],
        out_specs=[pl.BlockSpec((t, C_QKV), lambda i: (i, 0)), pl.BlockSpec((8, C_QKV), lambda i: (0, 0))],
        out_shape=[jax.ShapeDtypeStruct((s, C_QKV), BF16), jax.ShapeDtypeStruct((8, C_QKV), F32)],
        compiler_params=_params(("arbitrary",)),
    )(p_c, p_c, dy, dy, w8)


def _delta_chunks(state, q, k, v, g, beta):
    pre = [_delta_pre(q[:, i * C_CHUNK:(i + 1) * C_CHUNK], k[:, i * C_CHUNK:(i + 1) * C_CHUNK],
                      v[:, i * C_CHUNK:(i + 1) * C_CHUNK], g[i], beta[i]) for i in range(DELTA_SUB)]
    outs = []
    for qg, intra, u, w, k_dec, dec in pre:
        v_new = u - _mm("nn", True, w, state)
        outs.append(_mm("nn", True, qg, state) + _mm("nn", True, intra, v_new))
        state = state * dec + _mm("tn", True, k_dec, v_new)
    return state, jnp.concatenate(outs, axis=1)


def _delta_pre(q, k, v, g, beta):
    c = C_CHUNK
    row, col = _iota((c, c), 0), _iota((c, c), 1)
    tril, strict = col <= row, col < row
    gc = _mmh("nn", g, (row <= col).astype(F32))
    g_last = jnp.sum(g, axis=1, keepdims=True)
    qs = q * (C_DK ** -0.5)
    decay = jnp.exp(jnp.where(tril, gc[:, :, None] - gc[:, None, :], NEG))
    k_beta, v_beta = k * beta[:, :, None], v * beta[:, :, None]
    x = -jnp.where(strict, _mm("nt", True, k_beta, k) * decay, 0.0)
    xd = jnp.where(row // DELTA_BLOCK == col // DELTA_BLOCK, x, 0.0)
    powers = [xd]
    for _ in range(3):
        powers.append(_mm("nn", True, powers[-1], powers[-1]))
    sol = jnp.concatenate([x - xd, v_beta, k_beta * jnp.exp(gc)[:, :, None]], axis=2)
    for p in reversed(powers):
        sol = sol + _mm("nn", True, p, sol)
    y, sol = sol[:, :, :c], sol[:, :, c:]
    sol = sol + _mm("nn", True, _mm("nn", True, y, y), sol)
    sol = sol + _mm("nn", True, y, sol)
    u, w = sol[:, :, :C_DK], sol[:, :, C_DK:]
    intra = jnp.where(tril, _mm("nt", True, qs, k) * decay, 0.0)
    return (qs * jnp.exp(gc)[:, :, None], intra, u, w, k * jnp.exp(g_last - gc)[:, :, None],
            jnp.exp(g_last)[:, :, None])


def _delta_specs(n, rev):
    def at(i):
        return n - 1 - i if rev else i
    tok = pl.BlockSpec((N_HEADS, DELTA_SUB * C_CHUNK, C_DK), lambda i: (0, at(i), 0))
    vec = pl.BlockSpec((DELTA_SUB, N_HEADS, C_CHUNK), lambda i: (at(i), 0, 0))
    st = pl.BlockSpec((1, N_HEADS, C_DK, C_DK), lambda i: (at(i), 0, 0, 0))
    return tok, vec, st


def _delta_fwd(q, k, v, g, beta):
    s = q.shape[1]
    n = s // (DELTA_SUB * C_CHUNK)
    tok, vec, st = _delta_specs(n, False)

    def body(q_ref, k_ref, v_ref, g_ref, b_ref, o_ref, st_ref, state):
        @pl.when(pl.program_id(0) == 0)
        def _():
            state[...] = jnp.zeros(state.shape, F32)

        cur = state[...]
        st_ref[0] = cur
        new, o = _delta_chunks(cur, q_ref[...], k_ref[...], v_ref[...], g_ref[...], b_ref[...])
        o_ref[...] = o
        state[...] = new

    return pl.pallas_call(
        body, name="delta_fwd", grid=(n,), in_specs=[tok, tok, tok, vec, vec], out_specs=[tok, st],
        out_shape=[jax.ShapeDtypeStruct((N_HEADS, s, C_DK), F32), jax.ShapeDtypeStruct((n, N_HEADS, C_DK, C_DK), F32)],
        scratch_shapes=[pltpu.VMEM((N_HEADS, C_DK, C_DK), F32)],
        compiler_params=_params(("arbitrary",)),
    )(q, k, v, g, beta)


def _delta_bwd(q, k, v, g, beta, states, do):
    s = q.shape[1]
    n = s // (DELTA_SUB * C_CHUNK)
    tok, vec, st = _delta_specs(n, True)

    def body(q_ref, k_ref, v_ref, g_ref, b_ref, st_ref, do_ref, dq_ref, dk_ref, dv_ref, dg_ref, db_ref, dstate):
        @pl.when(pl.program_id(0) == 0)
        def _():
            dstate[...] = jnp.zeros(dstate.shape, F32)

        _, vjp = jax.vjp(_delta_chunks, st_ref[0], q_ref[...], k_ref[...], v_ref[...], g_ref[...], b_ref[...])
        dst, dq, dk, dv, dg, db = vjp((dstate[...], do_ref[...]))
        dq_ref[...], dk_ref[...], dv_ref[...] = dq, dk, dv
        dg_ref[...], db_ref[...] = dg, db
        dstate[...] = dst

    tok_shape = jax.ShapeDtypeStruct((N_HEADS, s, C_DK), F32)
    vec_shape = jax.ShapeDtypeStruct((n * DELTA_SUB, N_HEADS, C_CHUNK), F32)
    return pl.pallas_call(
        body, name="delta_bwd", grid=(n,), in_specs=[tok, tok, tok, vec, vec, st, tok],
        out_specs=[tok, tok, tok, vec, vec], out_shape=[tok_shape, tok_shape, tok_shape, vec_shape, vec_shape],
        scratch_shapes=[pltpu.VMEM((N_HEADS, C_DK, C_DK), F32)],
        compiler_params=_params(("arbitrary",)),
    )(q, k, v, g, beta, states, do)


ATT_SCALE = B_QK ** -0.5


SCORE_SCALE_LOG2 = ATT_SCALE * math.log2(math.e)


def _lanes(x, n):
    return x if n == 1 else jnp.concatenate([x] * n, axis=1)


def _scores(a_ref, b_ref):
    return lax.dot_general(a_ref[...], b_ref[...], (((1,), (1,)), ((), ())), preferred_element_type=F32) * SCORE_SCALE_LOG2


def _flash_fwd(q, k, v, tb):
    s = q.shape[0]
    tb = min(tb, s)
    nb = s // tb
    nrep = tb // 128

    def body(q_ref, k_ref, v_ref, o_ref, lse_ref, m_s, l_s, acc):
        i, j = pl.program_id(1), pl.program_id(2)

        @pl.when(j == 0)
        def _():
            m_s[...] = jnp.full(m_s.shape, -jnp.inf, F32)
            l_s[...] = jnp.zeros(l_s.shape, F32)
            acc[...] = jnp.zeros(acc.shape, F32)

        def step(masked):
            sc = _scores(q_ref, k_ref)
            if masked:
                sc = jnp.where(_iota((tb, tb), 1) <= _iota((tb, tb), 0), sc, -jnp.inf)
            m_prev = m_s[...]
            m_new = jnp.maximum(m_prev, jnp.max(sc, axis=1, keepdims=True))
            alpha = jnp.exp2(m_prev - m_new)
            p = jnp.exp2(sc - _lanes(m_new, nrep))
            l_s[...] = alpha * l_s[...] + jnp.sum(p, axis=1, keepdims=True)
            acc[...] = alpha * acc[...] + jnp.dot(p.astype(BF16), v_ref[...], preferred_element_type=F32)
            m_s[...] = m_new

        @pl.when(j < i)
        def _():
            step(False)

        @pl.when(j == i)
        def _():
            step(True)
            o_ref[...] = acc[...] / l_s[...]
            lse_ref[...] = m_s[...] + jnp.log2(l_s[...])

    qs = pl.BlockSpec((tb, HEAD_PAD), lambda h, i, j: (i, h))
    ks = pl.BlockSpec((tb, HEAD_PAD), lambda h, i, j: (jnp.minimum(j, i), h))
    shape = jax.ShapeDtypeStruct((s, N_HEADS * HEAD_PAD), F32)
    return pl.pallas_call(
        body, name="flash_fwd", grid=(N_HEADS, nb, nb), in_specs=[qs, ks, ks],
        out_specs=[qs, qs], out_shape=[shape, shape],
        scratch_shapes=[pltpu.VMEM((tb, 128), F32), pltpu.VMEM((tb, 128), F32), pltpu.VMEM((tb, HEAD_PAD), F32)],
        compiler_params=_params(("parallel", "parallel", "arbitrary")),
    )(q, k, v)


def _f_attn_delta(o, do):
    return (_head_sums(o * do, HEAD_PAD),)


def _row_stats(rep):
    return rep[:, ::HEAD_PAD].T.reshape(N_HEADS, 1, -1)


def _flash_bwd(q, k, v, do, lse_row, delta_row, tb):
    s = q.shape[0]
    tb = min(tb, s)
    nb = s // tb

    def body(q_ref, k_ref, v_ref, do_ref, lse_ref, dl_ref, dq_ref, dk_ref, dv_ref, dk_acc, dv_acc):
        j, i = pl.program_id(1), pl.program_id(2)

        @pl.when((i == 0) & (j == 0))
        def _():
            dq_ref[...] = jnp.zeros(dq_ref.shape, F32)

        @pl.when(i == 0)
        def _():
            dk_acc[...] = jnp.zeros(dk_acc.shape, F32)
            dv_acc[...] = jnp.zeros(dv_acc.shape, F32)

        def step(masked):
            st = _scores(k_ref, q_ref)
            if masked:
                st = jnp.where(_iota((tb, tb), 0) <= _iota((tb, tb), 1), st, -jnp.inf)
            do = do_ref[...].astype(BF16)
            pt = jnp.exp2(st - lse_ref[0])
            dpt = lax.dot_general(v_ref[...], do, (((1,), (1,)), ((), ())), preferred_element_type=F32)
            dst = (pt * (dpt - dl_ref[0])).astype(BF16)
            dv_acc[...] += jnp.dot(pt.astype(BF16), do, preferred_element_type=F32)
            dk_acc[...] += jnp.dot(dst, q_ref[...], preferred_element_type=F32)
            rows = pl.ds(pl.multiple_of(i * tb, tb), tb)
            dq_ref[rows, :] += lax.dot_general(dst, k_ref[...], (((0,), (0,)), ((), ())),
                                               preferred_element_type=F32) * ATT_SCALE

        @pl.when(i == j)
        def _():
            step(True)

        @pl.when(i > j)
        def _():
            step(False)

        @pl.when(i == nb - 1)
        def _():
            dk_ref[...] = dk_acc[...] * ATT_SCALE
            dv_ref[...] = dv_acc[...]

    qs = pl.BlockSpec((tb, HEAD_PAD), lambda h, j, i: (jnp.maximum(i, j), h))
    ks = pl.BlockSpec((tb, HEAD_PAD), lambda h, j, i: (j, h))
    rs = pl.BlockSpec((1, 1, tb), lambda h, j, i: (h, 0, jnp.maximum(i, j)))
    shape = jax.ShapeDtypeStruct((s, N_HEADS * HEAD_PAD), F32)
    return pl.pallas_call(
        body, name="flash_bwd", grid=(N_HEADS, nb, nb), in_specs=[qs, ks, ks, qs, rs, rs],
        out_specs=[pl.BlockSpec((s, HEAD_PAD), lambda h, j, i: (0, h)), ks, ks], out_shape=[shape, shape, shape],
        scratch_shapes=[pltpu.VMEM((tb, HEAD_PAD), F32), pltpu.VMEM((tb, HEAD_PAD), F32)],
        compiler_params=_params(("parallel", "arbitrary", "arbitrary")),
    )(q, k, v, do, lse_row, delta_row)


def _loss_head(y, target, t):
    s, d = y.shape
    t = min(t, s)

    def body(y_ref, t_ref, sum_ref, dy_ref):
        @pl.when(pl.program_id(0) == 0)
        def _():
            sum_ref[...] = jnp.zeros(sum_ref.shape, F32)

        err = y_ref[...] - t_ref[...]
        dy_ref[...] = err * (1.0 / d)
        sum_ref[...] += jnp.broadcast_to(jnp.sum(err * err), sum_ref.shape)

    return pl.pallas_call(
        body, name="loss_head", grid=(s // t,),
        in_specs=[pl.BlockSpec((t, d), lambda i: (i, 0))] * 2,
        out_specs=[pl.BlockSpec((1, 128), lambda i: (0, 0)), pl.BlockSpec((t, d), lambda i: (i, 0))],
        out_shape=[jax.ShapeDtypeStruct((1, 128), F32), jax.ShapeDtypeStruct((s, d), F32)],
        compiler_params=_params(("arbitrary",)),
    )(y, target)


def _pad_to(a, axis, size):
    pad = [(0, 0)] * a.ndim
    pad[axis] = (0, size - a.shape[axis])
    return jnp.pad(a, pad)


W_IN_GROUPS = {
    "a": [(0, O_QLAT)],
    "b": [(O_KVLAT, O_KROPE), B_NOPE, (O_KROPE, O_CQKV), HEAD_PAD - B_QK, (O_QLAT, O_KVLAT)],
    "c": [(O_CQKV, O_GATES), W_C - (O_GATES - O_CQKV)],
    "g": [(O_GATES, D_IN)],
}
W_IN_SHARD = D_IN // N_DEV
W_IN_ROWS = 128


def _group_width(key):
    return sum(e if isinstance(e, int) else e[1] - e[0] for e in W_IN_GROUPS[key])


def _assemble_w_in(shards):
    depth = shards.shape[1]
    keys = list(W_IN_GROUPS)

    def body(s_ref, *o_refs):
        vals = [s_ref[k, 0] for k in range(N_DEV)]
        for key, o_ref in zip(keys, o_refs):
            parts = []
            for e in W_IN_GROUPS[key]:
                if isinstance(e, int):
                    parts.append(jnp.zeros((W_IN_ROWS, e), shards.dtype))
                    continue
                lo, hi = e
                while lo < hi:
                    k = lo // W_IN_SHARD
                    end = min(hi, (k + 1) * W_IN_SHARD)
                    parts.append(vals[k][:, lo - k * W_IN_SHARD:end - k * W_IN_SHARD])
                    lo = end
            o_ref[0] = parts[0] if len(parts) == 1 else jnp.concatenate(parts, axis=1)

    return pl.pallas_call(
        body, name="assemble_w_in", grid=(depth, D_MODEL // W_IN_ROWS),
        in_specs=[pl.BlockSpec((N_DEV, 1, W_IN_ROWS, W_IN_SHARD), lambda l, i: (0, l, i, 0))],
        out_specs=[pl.BlockSpec((1, W_IN_ROWS, _group_width(key)), lambda l, i: (l, i, 0)) for key in keys],
        out_shape=[jax.ShapeDtypeStruct((depth, D_MODEL, _group_width(key)), shards.dtype) for key in keys],
        compiler_params=_params(("parallel", "parallel")),
    )(shards)


def _split_dw_in(groups):
    keys = list(W_IN_GROUPS)
    depth = groups[0].shape[0]
    runs = []
    for gi, key in enumerate(keys):
        col = 0
        for e in W_IN_GROUPS[key]:
            if not isinstance(e, int):
                runs.append((e[0], e[1], gi, col))
            col += e if isinstance(e, int) else e[1] - e[0]
    runs.sort()

    def body(*refs):
        vals = [r[0] for r in refs[:len(keys)]]
        o_ref = refs[len(keys)]
        for k in range(N_DEV):
            lo, hi = k * W_IN_SHARD, (k + 1) * W_IN_SHARD
            parts = []
            for a, b, gi, col in runs:
                s, e = max(a, lo), min(b, hi)
                if s < e:
                    parts.append(vals[gi][:, col + s - a:col + e - a])
            o_ref[k, 0] = jnp.concatenate(parts, axis=1).astype(GRAD_WIRE)

    return pl.pallas_call(
        body, name="split_dw_in", grid=(depth, D_MODEL // W_IN_ROWS),
        in_specs=[pl.BlockSpec((1, W_IN_ROWS, _group_width(key)), lambda l, i: (l, i, 0)) for key in keys],
        out_specs=pl.BlockSpec((N_DEV, 1, W_IN_ROWS, W_IN_SHARD), lambda l, i: (0, l, i, 0)),
        out_shape=jax.ShapeDtypeStruct((N_DEV, depth, D_MODEL, W_IN_SHARD), GRAD_WIRE),
        compiler_params=_params(("parallel", "parallel")),
    )(*groups)


def _prep_layer(w, l):
    p = {}
    for key in W_IN_GROUPS:
        val = w["w_in_" + key][l].astype(BF16)
        p["w_" + key] = val
        p["wt_" + key] = val.T
    for name in ("norm1_g", "sgu_norm_g", "q_lat_norm_g", "kv_lat_norm_g", "o_norm_g", "norm2_g"):
        p[name] = w[name][l][None, :]
    p["w_spatial"], p["b_spatial"] = w["w_spatial"][l], w["b_spatial"][l]
    p["wq"] = _pad_to(w["w_q_up"][l].astype(F32).reshape(B_Q_LORA, N_HEADS, B_QK), 2, HEAD_PAD).reshape(B_Q_LORA, -1)
    kv = w["w_kv_up"][l].astype(F32).reshape(B_KV_LORA, N_HEADS, B_NOPE + B_VDIM)
    p["wk"] = _pad_to(kv[:, :, :B_NOPE], 2, HEAD_PAD).reshape(B_KV_LORA, -1)
    p["wv"] = _pad_to(kv[:, :, B_NOPE:], 2, HEAD_PAD).reshape(B_KV_LORA, -1)
    p["qn_g"] = _pad_to(w["q_norm_g"][l][None, :], 1, HEAD_PAD)
    p["kn_g"] = _pad_to(w["k_norm_g"][l][None, :], 1, HEAD_PAD)
    p["conv_w"] = _pad_to(w["conv_w"][l], 0, 8)
    row = lambda v: jnp.pad(v[None, :], ((0, 0), (N_HEADS, HEAD_PAD - 2 * N_HEADS)))
    p["a_log"], p["dt_bias"] = row(w["a_log"][l]), row(w["dt_bias"][l])
    wb = w["w_branch"][l]
    wb1 = _pad_to(wb[1].reshape(N_HEADS, B_VDIM, D_MODEL), 1, HEAD_PAD).reshape(-1, D_MODEL)
    for key, val in (("wb0", wb[0]), ("wb1", wb1), ("wb2", wb[2]), ("w_out", w["w_out"][l]),
                     ("w_ff1", w["w_ff1"][l]), ("w_ff2", w["w_ff2"][l])):
        p[key] = val.astype(BF16)
        p[key + "_t"] = val.T.astype(BF16)
    return p


def _unprep_grads(g):
    out = {"w_in_" + key: g["w_" + key] for key in W_IN_GROUPS}
    for name in ("norm1_g", "sgu_norm_g", "q_lat_norm_g", "kv_lat_norm_g", "o_norm_g", "norm2_g"):
        out[name] = g[name][0]
    out["w_spatial"], out["b_spatial"] = g["w_spatial"], g["b_spatial"]
    out["w_q_up"] = g["wq"].reshape(B_Q_LORA, N_HEADS, HEAD_PAD)[:, :, :B_QK].reshape(B_Q_LORA, -1)
    gk = g["wk"].reshape(B_KV_LORA, N_HEADS, HEAD_PAD)[:, :, :B_NOPE]
    gv = g["wv"].reshape(B_KV_LORA, N_HEADS, HEAD_PAD)[:, :, :B_VDIM]
    out["w_kv_up"] = jnp.concatenate([gk, gv], axis=2).reshape(B_KV_LORA, -1)
    out["q_norm_g"], out["k_norm_g"] = g["qn_g"][0, :B_QK], g["kn_g"][0, :B_QK]
    out["conv_w"] = g["conv_w"][:4]
    out["a_log"], out["dt_bias"] = g["a_log"][0, N_HEADS:2 * N_HEADS], g["dt_bias"][0, N_HEADS:2 * N_HEADS]
    gb1 = g["wb1"].reshape(N_HEADS, HEAD_PAD, D_MODEL)[:, :B_VDIM].reshape(-1, D_MODEL)
    out["w_branch"] = jnp.stack([g["wb0"], gb1, g["wb2"]], axis=0)
    out["w_out"], out["w_ff1"], out["w_ff2"] = g["w_out"], g["w_ff1"], g["w_ff2"]
    return out


def _heads_first(a):
    s = a.shape[0]
    return a.reshape(s, N_HEADS, C_DK).transpose(1, 0, 2)


def _heads_last(a):
    return a.transpose(1, 0, 2).reshape(a.shape[1], N_HEADS * C_DK)


def _chunk_vec(a):
    return a.reshape(-1, C_CHUNK, N_HEADS).transpose(0, 2, 1)


def _unchunk_vec(a):
    return a.transpose(0, 2, 1).reshape(-1, N_HEADS)


def _layer_fwd(x, p, tabs, t, tb):
    sv = {"x": x}
    h1, = _local_fwd("norm1", _f_norm, [(x, D_MODEL, 0)], [p["norm1_g"]], [(D_MODEL, BF16)], t)
    sv["h1"] = h1
    p_a, p_b, p_c = (_matmul("proj_" + key, h1, p["w_" + key]) for key in "abc")
    p_g = _matmul("proj_g", h1, p["w_g"], out_dtype=BF16)
    sv.update(p_a=p_a, p_b=p_b, p_c=p_c, p_g=p_g)
    y_a, = _local_fwd("sgu", _f_sgu, [(p_a, 2 * A_WIDTH, 0)], [p["sgu_norm_g"], p["w_spatial"], p["b_spatial"]],
                      [(A_WIDTH, BF16)], SGU_CHUNK)
    q, k, v = _local_fwd("mla_prep", _f_mla_prep, [(p_b, W_B, 0)] + [(tb_, HEAD_PAD, 0) for tb_ in tabs],
                         [p["q_lat_norm_g"], p["kv_lat_norm_g"], p["wq"], p["wk"], p["wv"], p["qn_g"], p["kn_g"]],
                         [(N_HEADS * HEAD_PAD, BF16)] * 3, min(t, 256))
    o_b, lse = _flash_fwd(q, k, v, tb)
    sv.update(q=q, k=k, v=v, o_b=o_b, lse=lse)
    conv_pre = _conv_fwd(p_c, p["conv_w"], t)
    cq, ck, cv, gb = _local_fwd("gdn_pre", _f_gdn_pre, [(conv_pre, C_QKV, 0), (p_c, HEAD_PAD, (C_QKV + C_Z) // HEAD_PAD)],
                                [p["a_log"], p["dt_bias"]], [(A_WIDTH, F32)] * 3 + [(HEAD_PAD, F32)], t)
    cq, ck, cv = _heads_first(cq), _heads_first(ck), _heads_first(cv)
    beta, g = _chunk_vec(gb[:, :N_HEADS]), _chunk_vec(gb[:, N_HEADS:2 * N_HEADS])
    o_c, states = _delta_fwd(cq, ck, cv, g, beta)
    o_c = _heads_last(o_c)
    sv.update(conv_pre=conv_pre, cq=cq, ck=ck, cv=cv, beta=beta, g=g, states=states, o_c=o_c)
    y_c, = _local_fwd("gdn_post", _f_gdn_post, [(o_c, A_WIDTH, 0), (p_c, C_Z, C_QKV // C_Z)], [p["o_norm_g"]],
                      [(A_WIDTH, BF16)], t)
    y0 = _matmul("branch0", y_a, p["wb0"], out_dtype=BF16)
    y1 = _matmul("branch1", o_b, p["wb1"], out_dtype=BF16)
    y2 = _matmul("branch2", y_c, p["wb2"], out_dtype=BF16)
    merged, = _local_fwd("merge", _f_merge, [(p_g, 3 * D_MODEL, 0), (y0, D_MODEL, 0), (y1, D_MODEL, 0), (y2, D_MODEL, 0)],
                         [], [(D_MODEL, BF16)], t)
    x1 = _matmul("out_proj", merged, p["w_out"], add=x)
    sv.update(y_a=y_a, y_c=y_c, y0=y0, y1=y1, y2=y2, merged=merged, x1=x1)
    h2, = _local_fwd("norm2", _f_norm, [(x1, D_MODEL, 0)], [p["norm2_g"]], [(D_MODEL, BF16)], t)
    a, r = _matmul("ff1", h2, p["w_ff1"], epilogue=lambda acc: (acc, jnp.square(jnp.maximum(acc, 0.0))),
                   out_dtypes=(BF16, BF16))
    x2 = _matmul("ff2", r, p["w_ff2"], add=x1)
    sv.update(h2=h2, a=a, r=r)
    return x2, sv


def _layer_bwd(dx2, sv, p, tabs, t, tb):
    g = {}
    da = _matmul("d_ff2", dx2, p["w_ff2_t"], extras=(sv["a"],), out_dtypes=(BF16,),
                 epilogue=lambda dr, a: (dr * (2.0 * jnp.maximum(a.astype(F32), 0.0)),))
    g["w_ff2"] = _matmul_tn("dw_ff2", sv["r"], dx2)
    dh2 = _matmul("d_ff1", da, p["w_ff1_t"])
    g["w_ff1"] = _matmul_tn("dw_ff1", sv["h2"], da)
    (dx1,), (g["norm2_g"],) = _local_bwd("norm2_bwd", _f_norm_res, [(sv["x1"], D_MODEL, 0)], [p["norm2_g"]],
                                         [dh2, dx2], t, [True], [True])
    dmerged = _matmul("d_out_proj", dx1, p["w_out_t"])
    g["w_out"] = _matmul_tn("dw_out", sv["merged"], dx1)
    (dp_g, dy0, dy1, dy2), _ = _local_bwd(
        "merge_bwd", _f_merge, [(sv["p_g"], 3 * D_MODEL, 0), (sv["y0"], D_MODEL, 0), (sv["y1"], D_MODEL, 0),
                                (sv["y2"], D_MODEL, 0)], [], [dmerged], min(t, 256), [True] * 4, [], [BF16] * 4)
    dy_a = _matmul("d_branch0", dy0, p["wb0_t"])
    do_b = _matmul("d_branch1", dy1, p["wb1_t"])
    dy_c = _matmul("d_branch2", dy2, p["wb2_t"])
    g["wb0"] = _matmul_tn("dw_branch0", sv["y_a"], dy0)
    g["wb1"] = _matmul_tn("dw_branch1", sv["o_b"], dy1)
    g["wb2"] = _matmul_tn("dw_branch2", sv["y_c"], dy2)
    p_c = sv["p_c"]
    (do_c, dc_z), (g["o_norm_g"],) = _local_bwd(
        "gdn_post_bwd", _f_gdn_post, [(sv["o_c"], A_WIDTH, 0), (p_c, C_Z, C_QKV // C_Z)], [p["o_norm_g"]], [dy_c], t,
        [True, True], [True], [F32, BF16])
    dcq, dck, dcv, dg, dbeta = _delta_bwd(sv["cq"], sv["ck"], sv["cv"], sv["g"], sv["beta"], sv["states"],
                                          _heads_first(do_c))
    dgb = jnp.pad(jnp.concatenate([_unchunk_vec(dbeta), _unchunk_vec(dg)], axis=1),
                  ((0, 0), (0, HEAD_PAD - 2 * N_HEADS)))
    (dconv, dba), (g["a_log"], g["dt_bias"]) = _local_bwd(
        "gdn_pre_bwd", _f_gdn_pre, [(sv["conv_pre"], C_QKV, 0), (p_c, HEAD_PAD, (C_QKV + C_Z) // HEAD_PAD)],
        [p["a_log"], p["dt_bias"]], [_heads_last(dcq), _heads_last(dck), _heads_last(dcv), dgb], t,
        [True, True], [True, True], [F32, BF16])
    dc_qkv, g["conv_w"] = _conv_bwd(p_c, dconv, p["conv_w"], t)
    dp_c = jnp.concatenate([dc_qkv, dc_z, dba], axis=1)
    delta, = _local_fwd("attn_delta", _f_attn_delta, [(sv["o_b"], N_HEADS * HEAD_PAD, 0), (do_b, N_HEADS * HEAD_PAD, 0)], [],
                        [(N_HEADS * HEAD_PAD, F32)], t)
    dq, dk, dv = _flash_bwd(sv["q"], sv["k"], sv["v"], do_b, _row_stats(sv["lse"]), _row_stats(delta), tb)
    mla_pars = [p["q_lat_norm_g"], p["kv_lat_norm_g"], p["wq"], p["wk"], p["wv"], p["qn_g"], p["kn_g"]]
    (dp_b,), mla_g = _local_bwd(
        "mla_prep_bwd", _f_mla_prep, [(sv["p_b"], W_B, 0)] + [(tb_, HEAD_PAD, 0) for tb_ in tabs], mla_pars,
        [dq, dk, dv], min(t, 256), [True, False, False, False], [True] * 7, [BF16])
    for name, val in zip(("q_lat_norm_g", "kv_lat_norm_g", "wq", "wk", "wv", "qn_g", "kn_g"), mla_g):
        g[name] = val
    (dp_a,), (g["sgu_norm_g"], g["w_spatial"], g["b_spatial"]) = _local_bwd(
        "sgu_bwd", _f_sgu, [(sv["p_a"], 2 * A_WIDTH, 0)], [p["sgu_norm_g"], p["w_spatial"], p["b_spatial"]], [dy_a],
        SGU_CHUNK, [True], [True] * 3, [BF16])
    dh1 = None
    for key, dp in (("a", dp_a), ("b", dp_b), ("c", dp_c), ("g", dp_g)):
        dh1 = _matmul("d_proj_" + key, dp, p["wt_" + key], add=dh1)
        g["w_" + key] = _matmul_tn("dw_proj_" + key, sv["h1"], dp)
    (dx,), (g["norm1_g"],) = _local_bwd("norm1_bwd", _f_norm_res, [(sv["x"], D_MODEL, 0)], [p["norm1_g"]],
                                        [dh1, dx1], t, [True], [True])
    return dx, _unprep_grads(g)


def _local_step(x, positions, w, target, t=512, tb=1024):
    s = x.shape[0]
    t = min(t, s)
    w = dict(w)
    for key, val in zip(W_IN_GROUPS, _assemble_w_in(w["w_in"])):
        w["w_in_" + key] = val
    half = B_ROPE // 2
    inv_freq = 1.0 / (ROPE_BASE ** (jnp.arange(half, dtype=F32) / half))
    inv_row = jnp.concatenate([jnp.zeros((B_NOPE,), F32), inv_freq, inv_freq, jnp.zeros((HEAD_PAD - B_QK,), F32)])[None, :]
    tabs = _local_fwd("rope_tables", _f_rope_tables, [(positions, 1, 0)], [inv_row], [(HEAD_PAD, F32)] * 3, t)
    preps, saved = [], []
    for l in range(DEPTH):
        preps.append(_prep_layer(w, l))
        x, sv = _layer_fwd(x, preps[l], tabs, t, tb)
        saved.append(sv)
    sq, dx = _loss_head(x, target, t)
    grads = [None] * DEPTH
    for l in reversed(range(DEPTH)):
        dx, grads[l] = _layer_bwd(dx, saved[l], preps[l], tabs, t, tb)
    stacked = lambda name: jnp.stack([grads[l][name] for l in range(DEPTH)], axis=0)
    out = {name: stacked(name) for name in WEIGHTS if name != "w_in"}
    out["w_in"] = _split_dw_in([stacked("w_in_" + key) for key in W_IN_GROUPS])
    return sq, dx, out


def _exchange(name, arrays, gather):
    n = len(arrays)

    def body(*refs):
        send, recv = refs[:n], refs[n:2 * n]
        send_sems, recv_sems, local_sems = refs[2 * n:]
        x, y, c = lax.axis_index("x"), lax.axis_index("y"), lax.axis_index("c")
        me = 4 * x + 2 * y + c

        def src(a, idx):
            return send[a] if gather[a] else send[a].at[idx]

        local = [pltpu.make_async_copy(src(a, me), recv[a].at[me], local_sems.at[a]) for a in range(n)]
        for cp in local:
            cp.start()
        copies = []
        for d in range(1, N_DEV):
            px, py, pc = x ^ ((d >> 2) & 1), y ^ ((d >> 1) & 1), c ^ (d & 1)
            peer = 4 * px + 2 * py + pc
            for a in range(n):
                cp = pltpu.make_async_remote_copy(
                    src_ref=src(a, peer), dst_ref=recv[a].at[me], send_sem=send_sems.at[a, d],
                    recv_sem=recv_sems.at[a, d], device_id=(px, py, pc), device_id_type=pl.DeviceIdType.MESH)
                cp.start()
                copies.append((cp, a, peer, d))
        for cp, a, peer, d in copies:
            cp.wait_send()
            pltpu.make_async_remote_copy(
                src_ref=src(a, peer), dst_ref=recv[a].at[peer], send_sem=send_sems.at[a, d],
                recv_sem=recv_sems.at[a, d], device_id=(x, y, c), device_id_type=pl.DeviceIdType.MESH).wait_recv()
        for cp in local:
            cp.wait()

    any_spec = pl.BlockSpec(memory_space=pl.ANY)
    return pl.pallas_call(
        body, name=name, in_specs=[any_spec] * n, out_specs=[any_spec] * n,
        out_shape=[jax.ShapeDtypeStruct(((N_DEV,) + a.shape) if gather[i] else a.shape, a.dtype)
                   for i, a in enumerate(arrays)],
        scratch_shapes=[pltpu.SemaphoreType.DMA((n, N_DEV)), pltpu.SemaphoreType.DMA((n, N_DEV)),
                        pltpu.SemaphoreType.DMA((n,))],
    )(*arrays)


REDUCE_BLOCK_ELEMS = 64 * 1024


def _reduce_adamw(name, recv, w, m, v):
    rows, cols = w.shape
    tr = rows
    while tr % 16 == 0 and tr * (-(-cols // 128) * 128) > REDUCE_BLOCK_ELEMS:
        tr //= 2
    c1, c2 = 1.0 - ADAM_B1 ** ADAM_STEP, 1.0 - ADAM_B2 ** ADAM_STEP

    def body(r_ref, w_ref, m_ref, v_ref, g_ref, d_ref, nm_ref, nv_ref):
        g = r_ref[0].astype(F32)
        for j in range(1, N_DEV):
            g = g + r_ref[j].astype(F32)
        m_new = ADAM_B1 * m_ref[...] + (1.0 - ADAM_B1) * g
        v_new = ADAM_B2 * v_ref[...] + (1.0 - ADAM_B2) * jnp.square(g)
        d_ref[...] = -ADAM_LR * ((m_new / c1) / (jnp.sqrt(v_new / c2) + ADAM_EPS) + ADAM_WD * w_ref[...])
        g_ref[...], nm_ref[...], nv_ref[...] = g, m_new, v_new

    flat = pl.BlockSpec((tr, cols), lambda i: (i, 0))
    return pl.pallas_call(
        body, name=name, grid=(rows // tr,),
        in_specs=[pl.BlockSpec((N_DEV, tr, cols), lambda i: (0, i, 0)), flat, flat, flat], out_specs=[flat] * 4,
        out_shape=[jax.ShapeDtypeStruct((rows, cols), F32)] * 4, compiler_params=_params(("parallel",)),
    )(recv, w, m, v)


PACK_ROWS = 512


def _pack(cols):
    flat = jnp.concatenate(cols, axis=-1)
    tile = PACK_ROWS * 128
    flat = _pad_to(flat, flat.ndim - 1, -(-flat.shape[-1] // tile) * tile)
    return flat.reshape(flat.shape[:-1] + (-1, 128))


def _unpack(packed, shapes, lead=()):
    flat = packed.reshape(lead + (-1,))
    out, off = [], 0
    for shp in shapes:
        n = math.prod(shp)
        out.append(flat[..., off:off + n].reshape(lead + tuple(shp)))
        off += n
    return out


def _to_shards(name, full):
    ax = SHARD_AXIS[name]
    shp = full.shape
    return jnp.moveaxis(full.reshape(shp[:ax] + (N_DEV, shp[ax] // N_DEV) + shp[ax + 1:]), ax, 0)


def _from_shards(name, shards):
    ax = SHARD_AXIS[name]
    a = jnp.moveaxis(shards, 0, ax)
    return a.reshape(a.shape[:ax] + (a.shape[ax] * a.shape[ax + 1],) + a.shape[ax + 2:])


def kernel(x, positions, norm1_g, w_in, sgu_norm_g, w_spatial, b_spatial, q_lat_norm_g, w_q_up, kv_lat_norm_g, w_kv_up, q_norm_g, k_norm_g, conv_w, a_log, dt_bias, o_norm_g, w_branch, w_out, norm2_g, w_ff1, w_ff2, loss_target, m_norm1_g, m_w_in, m_sgu_norm_g, m_w_spatial, m_b_spatial, m_q_lat_norm_g, m_w_q_up, m_kv_lat_norm_g, m_w_kv_up, m_q_norm_g, m_k_norm_g, m_conv_w, m_a_log, m_dt_bias, m_o_norm_g, m_w_branch, m_w_out, m_norm2_g, m_w_ff1, m_w_ff2, v_norm1_g, v_w_in, v_sgu_norm_g, v_w_spatial, v_b_spatial, v_q_lat_norm_g, v_w_q_up, v_kv_lat_norm_g, v_w_kv_up, v_q_norm_g, v_k_norm_g, v_conv_w, v_a_log, v_dt_bias, v_o_norm_g, v_w_branch, v_w_out, v_norm2_g, v_w_ff1, v_w_ff2):
    args = locals()
    local_w = {n: args[n] for n in WEIGHTS}
    state = (local_w, {n: args["m_" + n] for n in WEIGHTS}, {n: args["v_" + n] for n in WEIGHTS})
    wire = [local_w[n] if n in EXACT_GATHER else local_w[n].astype(BF16) for n in SHARDED]
    full = dict(local_w)
    for n, part in zip(SHARDED, _exchange("gather_weights", wire, [True] * len(wire))):
        full[n] = part if n == "w_in" else _from_shards(n, part)
    sq, grad_x, grads = _local_step(x[0], positions.reshape(-1, 1), full, loss_target[0])
    loss = lax.psum(sq[0, 0] * (0.5 / D_MODEL), ("x", "y", "c"))
    send = [grads[n] if n == "w_in" else _to_shards(n, grads[n]).astype(GRAD_WIRE) for n in SHARDED]
    rep = _pack([grads[n].reshape(-1) for n in REPLICATED])
    recv = _exchange("exchange_grads", send + [rep], [False] * len(send) + [True])
    results = {}
    for n, r in zip(SHARDED, recv):
        shp = local_w[n].shape
        flat = (math.prod(shp[:-1]), shp[-1])
        outs = _reduce_adamw("adamw_" + n, r.reshape((N_DEV,) + flat), *[src[n].reshape(flat) for src in state])
        results[n] = [o.reshape(shp) for o in outs]
    outs = _reduce_adamw("adamw_replicated", recv[-1],
                         *[_pack([src[n].reshape(-1) for n in REPLICATED]) for src in state])
    rep_shapes = [local_w[n].shape for n in REPLICATED]
    for n, vals in zip(REPLICATED, zip(*[_unpack(o, rep_shapes) for o in outs])):
        results[n] = vals
    return (loss, grad_x[None], *[results[n][k] for k in range(4) for n in WEIGHTS])
```

```python
import functools
import math

import jax
import jax.numpy as jnp
from jax import lax
from jax.experimental import pallas as pl
from jax.experimental.pallas import tpu as pltpu

F32, BF16 = jnp.float32, jnp.bfloat16
HI = lax.Precision.HIGHEST

N_DEV = 8
D_MODEL = 1024
DEPTH = 2
N_HEADS = 8
HEAD_PAD = 128
A_WIDTH = 512
B_NOPE, B_ROPE, B_VDIM = 64, 32, 64
B_QK = B_NOPE + B_ROPE
B_Q_LORA, B_KV_LORA = 384, 256
ROPE_BASE = 10000.0
C_DK = 64
C_CHUNK = 64
DELTA_SUB = 2
DELTA_BLOCK = 16
C_QKV = 1536
C_Z = 512
SGU_CHUNK = 128
D_FF = 4096
EPS = 1e-6
ADAM_LR, ADAM_B1, ADAM_B2, ADAM_EPS, ADAM_WD, ADAM_STEP = 0.001, 0.9, 0.999, 1e-08, 0.01, 10
O_QLAT, O_KVLAT, O_KROPE, O_CQKV, O_GATES, D_IN = 1024, 1408, 1664, 1696, 3760, 6832
W_B, W_C = 768, 2176
VMEM_LIMIT = 56 * 2 ** 20
NEG = -1e30

SHARDED = ("w_in", "w_q_up", "w_kv_up", "conv_w", "w_branch", "w_out", "w_ff1", "w_ff2")
EXACT_GATHER = ("conv_w",)
GRAD_WIRE = BF16
SHARD_AXIS = {"w_in": 2, "w_q_up": 2, "w_kv_up": 2, "conv_w": 2, "w_branch": 3, "w_out": 1, "w_ff1": 2, "w_ff2": 1}
REPLICATED = ("norm1_g", "sgu_norm_g", "w_spatial", "b_spatial", "q_lat_norm_g", "kv_lat_norm_g", "q_norm_g",
              "k_norm_g", "a_log", "dt_bias", "o_norm_g", "norm2_g")
WEIGHTS = ("norm1_g", "w_in", "sgu_norm_g", "w_spatial", "b_spatial", "q_lat_norm_g", "w_q_up", "kv_lat_norm_g",
           "w_kv_up", "q_norm_g", "k_norm_g", "conv_w", "a_log", "dt_bias", "o_norm_g", "w_branch", "w_out",
           "norm2_g", "w_ff1", "w_ff2")


def _params(sem, vmem=VMEM_LIMIT):
    return pltpu.CompilerParams(dimension_semantics=sem, vmem_limit_bytes=vmem)


_FORMS = {"nn": (1, 0), "nt": (1, 1), "tn": (0, 0)}


def _dot(form, a, b, batch, prec=None):
    ca, cb = _FORMS[form]
    o = 1 if batch else 0
    bd = ((0,), (0,)) if batch else ((), ())
    return lax.dot_general(a, b, (((ca + o,), (cb + o,)), bd), precision=prec, preferred_element_type=F32)


def _mm_raw(form, batch, a, b):
    return _dot(form, a.astype(BF16), b.astype(BF16), batch)


@functools.partial(jax.custom_vjp, nondiff_argnums=(0, 1))
def _mm(form, batch, a, b):
    return _mm_raw(form, batch, a, b)


def _mm_fwd(form, batch, a, b):
    return _mm_raw(form, batch, a, b), (a, b)


def _mm_bwd(form, batch, res, g):
    a, b = res
    if form == "nn":
        da, db = _mm_raw("nt", batch, g, b), _mm_raw("tn", batch, a, g)
    elif form == "nt":
        da, db = _mm_raw("nn", batch, g, b), _mm_raw("tn", batch, g, a)
    else:
        da, db = _mm_raw("nt", batch, b, g), _mm_raw("nn", batch, a, g)
    return da.astype(a.dtype), db.astype(b.dtype)


_mm.defvjp(_mm_fwd, _mm_bwd)


def _mmh(form, a, b, batch=False):
    return _dot(form, a, b, batch, HI)


@functools.partial(jax.custom_vjp, nondiff_argnums=(1, 2))
def _roll(x, shift, axis):
    return pltpu.roll(x, shift % x.shape[axis], axis)


def _roll_fwd(x, shift, axis):
    return _roll(x, shift, axis), None


def _roll_bwd(shift, axis, _, g):
    return (_roll(g, -shift, axis),)


_roll.defvjp(_roll_fwd, _roll_bwd)


@jax.custom_vjp
def _tile_heads(x):
    return jnp.concatenate([x] * N_HEADS, axis=1)


def _tile_heads_fwd(x):
    return _tile_heads(x), None


def _tile_heads_bwd(_, g):
    w = g.shape[1] // N_HEADS
    acc = g[:, :w]
    for h in range(1, N_HEADS):
        acc = acc + g[:, h * w:(h + 1) * w]
    return (acc,)


_tile_heads.defvjp(_tile_heads_fwd, _tile_heads_bwd)


def _iota(shape, dim):
    return lax.broadcasted_iota(jnp.int32, shape, dim)


def _head_indicator_t(width, per_head):
    return (_iota((N_HEADS, width), 1) // per_head == _iota((N_HEADS, width), 0)).astype(F32)


def _rms(x, g):
    return x * lax.rsqrt(jnp.mean(x * x, axis=-1, keepdims=True) + EPS) * g


def _gelu(x):
    return 0.5 * x * (1.0 + lax.erf(x * (2.0 ** -0.5)))


def _head_sums(x, per_head):
    blocks = []
    for b in range(x.shape[1] // 128):
        blk = x[:, b * 128:(b + 1) * 128]
        if per_head == 128:
            blocks.append(jnp.broadcast_to(jnp.sum(blk, axis=1, keepdims=True), blk.shape))
        else:
            low = _iota((1, 128), 1) < per_head
            s_low = jnp.sum(jnp.where(low, blk, 0.0), axis=1, keepdims=True)
            s_high = jnp.sum(jnp.where(low, 0.0, blk), axis=1, keepdims=True)
            blocks.append(jnp.where(low, s_low, s_high))
    return jnp.concatenate(blocks, axis=1)


def _head_rms(x, g_full, per_head, n_real):
    return x * lax.rsqrt(_head_sums(x * x, per_head) * (1.0 / n_real) + EPS) * g_full


def _rope(x, cos_t, sin_hi, sin_lo):
    half = B_ROPE // 2
    return (x * _tile_heads(cos_t) + _roll(x, half, 1) * _tile_heads(sin_hi)
            + _roll(x, -half, 1) * _tile_heads(sin_lo))


def _tok_spec(t, width, col):
    return pl.BlockSpec((t, width), lambda i, c=col: (i, c))


def _par_spec(shape):
    nd = len(shape)
    return pl.BlockSpec(shape, lambda i: (0,) * nd)


def _local_fwd(name, f, toks, pars, outs, t):
    s = toks[0][0].shape[0]
    nt, npar = len(toks), len(pars)

    def body(*refs):
        vals = [r[...] for r in refs[:nt + npar]]
        for r, o in zip(refs[nt + npar:], f(*vals)):
            r[...] = o.astype(r.dtype)

    return pl.pallas_call(
        body, name=name, grid=(s // t,),
        in_specs=[_tok_spec(t, w, c) for _, w, c in toks] + [_par_spec(p.shape) for p in pars],
        out_specs=[_tok_spec(t, w, 0) for w, _ in outs],
        out_shape=[jax.ShapeDtypeStruct((s, w), dt) for w, dt in outs],
        compiler_params=_params(("parallel",)),
    )(*[a for a, _, _ in toks], *pars)


def _local_bwd(name, f, toks, pars, cots, t, tok_diff, par_diff, grad_dtypes=None):
    s = toks[0][0].shape[0]
    nt, npar, nc = len(toks), len(pars), len(cots)
    dt_idx = [k for k in range(nt) if tok_diff[k]]
    dp_idx = [k for k in range(npar) if par_diff[k]]
    grad_dtypes = grad_dtypes or [F32] * len(dt_idx)

    def body(*refs):
        i = pl.program_id(0)
        tv = [r[...] for r in refs[:nt]]
        pv = [r[...] for r in refs[nt:nt + npar]]
        cv = [r[...].astype(F32) for r in refs[nt + npar:nt + npar + nc]]
        out_refs = refs[nt + npar + nc:]

        def g(*d):
            tt, pp = list(tv), list(pv)
            for k, val in zip(dt_idx, d[:len(dt_idx)]):
                tt[k] = val
            for k, val in zip(dp_idx, d[len(dt_idx):]):
                pp[k] = val
            return tuple(o.astype(F32) for o in f(*tt, *pp))

        _, vjp = jax.vjp(g, *[tv[k] for k in dt_idx], *[pv[k] for k in dp_idx])
        grads = vjp(tuple(cv))
        for r, gr in zip(out_refs[:len(dt_idx)], grads[:len(dt_idx)]):
            r[...] = gr.astype(r.dtype)
        par_refs = out_refs[len(dt_idx):]

        @pl.when(i == 0)
        def _():
            for r in par_refs:
                r[...] = jnp.zeros(r.shape, r.dtype)

        for r, gr in zip(par_refs, grads[len(dt_idx):]):
            r[...] += gr.astype(F32)

    res = pl.pallas_call(
        body, name=name, grid=(s // t,),
        in_specs=([_tok_spec(t, w, c) for _, w, c in toks] + [_par_spec(p.shape) for p in pars]
                  + [_tok_spec(t, c.shape[1], 0) for c in cots]),
        out_specs=([_tok_spec(t, toks[k][1], 0) for k in dt_idx] + [_par_spec(pars[k].shape) for k in dp_idx]),
        out_shape=([jax.ShapeDtypeStruct((s, toks[k][1]), dt) for k, dt in zip(dt_idx, grad_dtypes)]
                   + [jax.ShapeDtypeStruct(pars[k].shape, F32) for k in dp_idx]),
        compiler_params=_params(("arbitrary",)),
    )(*[a for a, _, _ in toks], *pars, *cots)
    return res[:len(dt_idx)], res[len(dt_idx):]


def _f_norm(x, g):
    return (_rms(x, g),)


def _f_norm_res(x, g):
    return _rms(x, g), x


def _f_rope_tables(pos, inv_row):
    ang = pos.astype(F32) * inv_row
    lane = _iota(ang.shape, 1)
    sn = jnp.sin(ang)
    half = B_ROPE // 2
    sin_hi = jnp.where((lane >= B_NOPE + half) & (lane < B_QK), sn, 0.0)
    sin_lo = jnp.where((lane >= B_NOPE) & (lane < B_NOPE + half), -sn, 0.0)
    return jnp.cos(ang), sin_hi, sin_lo


def _f_sgu(p_a, g, w_s, b_s):
    u = _gelu(p_a[:, :A_WIDTH])
    v = _rms(_gelu(p_a[:, A_WIDTH:]), g)
    tril = _iota((SGU_CHUNK, SGU_CHUNK), 1) <= _iota((SGU_CHUNK, SGU_CHUNK), 0)
    w_cat = jnp.concatenate([jnp.where(tril, w_s[gi], 0.0) for gi in range(N_HEADS)], axis=1)
    group = _iota((1, A_WIDTH), 1) // (A_WIDTH // N_HEADS)
    v_stack = jnp.concatenate([jnp.where(group == gi, v, 0.0) for gi in range(N_HEADS)], axis=0)
    bias = _mmh("tn", b_s, _head_indicator_t(A_WIDTH, A_WIDTH // N_HEADS))
    return (u * (_mm("nn", False, w_cat, v_stack) + bias),)


def _f_mla_prep(p_b, cos_t, sin_hi, sin_lo, q_lat_g, kv_lat_g, wq, wk, wv, qn_g, kn_g):
    kv_lat, k_rope, q_lat = p_b[:, :B_KV_LORA], p_b[:, B_KV_LORA:B_KV_LORA + HEAD_PAD], p_b[:, B_KV_LORA + HEAD_PAD:]
    q = _mm("nn", False, _rms(q_lat, q_lat_g), wq)
    q = _rope(_head_rms(q, _tile_heads(qn_g), HEAD_PAD, B_QK), cos_t, sin_hi, sin_lo)
    kvn = _rms(kv_lat, kv_lat_g)
    k = _mm("nn", False, kvn, wk) + _tile_heads(k_rope)
    k = _rope(_head_rms(k, _tile_heads(kn_g), HEAD_PAD, B_QK), cos_t, sin_hi, sin_lo)
    ones_lane = (_iota((1, N_HEADS * HEAD_PAD), 1) % HEAD_PAD == B_VDIM).astype(F32)
    return q, k, _mm("nn", False, kvn, wv) + ones_lane


def _f_gdn_pre(conv_pre, ba, a_log_row, dt_row):
    qkv = jax.nn.silu(conv_pre)

    def l2(x):
        return x * lax.rsqrt(_head_sums(x * x, C_DK) + EPS)

    lane = _iota(ba.shape, 1)
    g = -jnp.exp(a_log_row) * jax.nn.softplus(ba + dt_row)
    gb = jnp.where(lane < N_HEADS, jax.nn.sigmoid(ba), jnp.where(lane < 2 * N_HEADS, g, 0.0))
    return l2(qkv[:, :A_WIDTH]), l2(qkv[:, A_WIDTH:2 * A_WIDTH]), qkv[:, 2 * A_WIDTH:], gb


def _f_gdn_post(o, c_z, o_g):
    place = (_iota((C_DK, A_WIDTH), 1) % C_DK == _iota((C_DK, A_WIDTH), 0)).astype(F32)
    return (_head_rms(o, _mmh("nn", o_g, place), C_DK, C_DK) * jax.nn.silu(c_z),)


def _f_merge(p_g, y0, y1, y2):
    d = D_MODEL
    p_g, y0, y1, y2 = (a.astype(F32) for a in (p_g, y0, y1, y2))
    return (jax.nn.sigmoid(p_g[:, :d]) * y0 + jax.nn.sigmoid(p_g[:, d:2 * d]) * y1
            + jax.nn.sigmoid(p_g[:, 2 * d:]) * y2,)


def _pick(n, whole_up_to, candidates):
    if n <= whole_up_to:
        return n
    for c in candidates:
        if n % c == 0:
            return c
    return n


def _matmul(name, a, w, add=None, out_dtype=F32, tm=512, extras=(), epilogue=None, out_dtypes=None):
    m, k = a.shape
    n = w.shape[1]
    tm = min(tm, m)
    tn = _pick(n, 2304, (2048, 1536, 1024, 512))
    if add is not None:
        extras, epilogue = (add,), lambda r, x: (r + x,)
    if epilogue is None:
        epilogue = lambda r: (r,)
    out_dtypes = out_dtypes or (out_dtype,)
    n_ex, n_out = len(extras), len(out_dtypes)

    def body(*refs):
        a_ref, w_ref = refs[0], refs[1]
        ex_refs, o_refs = refs[2:2 + n_ex], refs[2 + n_ex:2 + n_ex + n_out]
        total = jnp.dot(a_ref[...].astype(BF16), w_ref[...].astype(BF16), preferred_element_type=F32)
        for o_ref, r in zip(o_refs, epilogue(total, *[e[...] for e in ex_refs])):
            o_ref[...] = r.astype(o_ref.dtype)

    tile = pl.BlockSpec((tm, tn), lambda j, i: (i, j))
    res = pl.pallas_call(
        body, name=name, grid=(n // tn, m // tm),
        in_specs=[pl.BlockSpec((tm, k), lambda j, i: (i, 0)), pl.BlockSpec((k, tn), lambda j, i: (0, j))] + [tile] * n_ex,
        out_specs=[tile] * n_out, out_shape=[jax.ShapeDtypeStruct((m, n), dt) for dt in out_dtypes],
        compiler_params=_params(("parallel", "parallel")),
    )(a, w, *extras)
    return res[0] if n_out == 1 else res


def _matmul_tn(name, a, b, a_col=None, tm=1024):
    m = a.shape[0]
    k, acol = (a.shape[1], 0) if a_col is None else a_col
    n = b.shape[1]
    tm = min(tm, m)
    tk = _pick(k, 1536, (1024, 512))
    tn = _pick(n, 2304, (1024, 512))
    nm = m // tm

    def body(a_ref, b_ref, o_ref):
        mm = pl.program_id(2)
        part = lax.dot_general(a_ref[...].astype(BF16), b_ref[...].astype(BF16), (((0,), (0,)), ((), ())),
                               preferred_element_type=F32)

        @pl.when(mm == 0)
        def _():
            o_ref[...] = part

        @pl.when(mm > 0)
        def _():
            o_ref[...] += part

    kb = k // tk
    return pl.pallas_call(
        body, name=name, grid=(kb, n // tn, nm),
        in_specs=[pl.BlockSpec((tm, tk), lambda i, j, mm: (mm, acol * kb + i)),
                  pl.BlockSpec((tm, tn), lambda i, j, mm: (mm, j))],
        out_specs=pl.BlockSpec((tk, tn), lambda i, j, mm: (i, j)),
        out_shape=jax.ShapeDtypeStruct((k, n), F32),
        compiler_params=_params(("parallel", "parallel", "arbitrary")),
    )(a, b)


def _shift_down(x, prev, s):
    rolled = pltpu.roll(x, s, 0)
    pr = pltpu.roll(prev, s, 0)
    head = jnp.where(_iota((8, 1), 0) < s, pr, rolled[:8])
    return jnp.concatenate([head, rolled[8:]], axis=0)


def _shift_up(x, nxt, s):
    t = x.shape[0]
    rolled = pltpu.roll(x, t - s, 0)
    nr = pltpu.roll(nxt, 8 - s, 0)
    tail = jnp.where(_iota((8, 1), 0) >= 8 - s, nr, rolled[t - 8:])
    return jnp.concatenate([rolled[:t - 8], tail], axis=0)


def _conv_fwd(p_c, w8, t):
    s = p_c.shape[0]
    t = min(t, s)
    r = t // 8

    def body(x_ref, prev_ref, w_ref, o_ref):
        i = pl.program_id(0)
        x = x_ref[...]
        prev = jnp.where(i == 0, 0.0, prev_ref[...])
        acc = w_ref[3:4, :] * x
        for sh in range(1, 4):
            acc = acc + w_ref[3 - sh:4 - sh, :] * _shift_down(x, prev, sh)
        o_ref[...] = acc

    return pl.pallas_call(
        body, name="conv_fwd", grid=(s // t,),
        in_specs=[pl.BlockSpec((t, C_QKV), lambda i: (i, 0)),
                  pl.BlockSpec((8, C_QKV), lambda i: (jnp.maximum(i * r - 1, 0), 0)),
                  pl.BlockSpec((8, C_QKV), lambda i: (0, 0))],
        out_specs=pl.BlockSpec((t, C_QKV), lambda i: (i, 0)),
        out_shape=jax.ShapeDtypeStruct((s, C_QKV), F32),
        compiler_params=_params(("parallel",)),
    )(p_c, p_c, w8)


def _conv_bwd(p_c, dy, w8, t):
    s = p_c.shape[0]
    t = min(t, s)
    r = t // 8
    n = s // t

    def body(x_ref, prev_ref, dy_ref, next_ref, w_ref, dx_ref, dw_ref):
        i = pl.program_id(0)
        x, g = x_ref[...], dy_ref[...]
        prev = jnp.where(i == 0, 0.0, prev_ref[...])
        nxt = jnp.where(i == n - 1, 0.0, next_ref[...])

        @pl.when(i == 0)
        def _():
            dw_ref[...] = jnp.zeros(dw_ref.shape, F32)

        dx = w_ref[3:4, :] * g
        dw_ref[3:4, :] += jnp.sum(g * x, axis=0, keepdims=True)
        for sh in range(1, 4):
            dx = dx + w_ref[3 - sh:4 - sh, :] * _shift_up(g, nxt, sh)
            dw_ref[3 - sh:4 - sh, :] += jnp.sum(g * _shift_down(x, prev, sh), axis=0, keepdims=True)
        dx_ref[...] = dx.astype(dx_ref.dtype)

    return pl.pallas_call(
        body, name="conv_bwd", grid=(n,),
        in_specs=[pl.BlockSpec((t, C_QKV), lambda i: (i, 0)),
                  pl.BlockSpec((8, C_QKV), lambda i: (jnp.maximum(i * r - 1, 0), 0)),
                  pl.BlockSpec((t, C_QKV), lambda i: (i, 0)),
                  pl.BlockSpec((8, C_QKV), lambda i: (jnp.minimum((i + 1) * r, s // 8 - 1), 0)),
                  pl.BlockSpec((8, C_QKV), lambda i: (0, 0))],
        out_specs=[pl.BlockSpec((t, C_QKV), lambda i: (i, 0)), pl.BlockSpec((8, C_QKV), lambda i: (0, 0))],
        out_shape=[jax.ShapeDtypeStruct((s, C_QKV), BF16), jax.ShapeDtypeStruct((8, C_QKV), F32)],
        compiler_params=_params(("arbitrary",)),
    )(p_c, p_c, dy, dy, w8)


def _delta_chunks(state, q, k, v, g, beta):
    pre = [_delta_pre(q[:, i * C_CHUNK:(i + 1) * C_CHUNK], k[:, i * C_CHUNK:(i + 1) * C_CHUNK],
                      v[:, i * C_CHUNK:(i + 1) * C_CHUNK], g[i], beta[i]) for i in range(DELTA_SUB)]
    outs = []
    for qg, intra, u, w, k_dec, dec in pre:
        v_new = u - _mm("nn", True, w, state)
        outs.append(_mm("nn", True, qg, state) + _mm("nn", True, intra, v_new))
        state = state * dec + _mm("tn", True, k_dec, v_new)
    return state, jnp.concatenate(outs, axis=1)


def _delta_pre(q, k, v, g, beta):
    c = C_CHUNK
    row, col = _iota((c, c), 0), _iota((c, c), 1)
    tril, strict = col <= row, col < row
    gc = _mmh("nn", g, (row <= col).astype(F32))
    g_last = jnp.sum(g, axis=1, keepdims=True)
    qs = q * (C_DK ** -0.5)
    decay = jnp.exp(jnp.where(tril, gc[:, :, None] - gc[:, None, :], NEG))
    k_beta, v_beta = k * beta[:, :, None], v * beta[:, :, None]
    x = -jnp.where(strict, _mm("nt", True, k_beta, k) * decay, 0.0)
    xd = jnp.where(row // DELTA_BLOCK == col // DELTA_BLOCK, x, 0.0)
    powers = [xd]
    for _ in range(3):
        powers.append(_mm("nn", True, powers[-1], powers[-1]))
    sol = jnp.concatenate([x - xd, v_beta, k_beta * jnp.exp(gc)[:, :, None]], axis=2)
    for p in reversed(powers):
        sol = sol + _mm("nn", True, p, sol)
    y, sol = sol[:, :, :c], sol[:, :, c:]
    sol = sol + _mm("nn", True, _mm("nn", True, y, y), sol)
    sol = sol + _mm("nn", True, y, sol)
    u, w = sol[:, :, :C_DK], sol[:, :, C_DK:]
    intra = jnp.where(tril, _mm("nt", True, qs, k) * decay, 0.0)
    return (qs * jnp.exp(gc)[:, :, None], intra, u, w, k * jnp.exp(g_last - gc)[:, :, None],
            jnp.exp(g_last)[:, :, None])


def _delta_specs(n, rev):
    def at(i):
        return n - 1 - i if rev else i
    tok = pl.BlockSpec((N_HEADS, DELTA_SUB * C_CHUNK, C_DK), lambda i: (0, at(i), 0))
    vec = pl.BlockSpec((DELTA_SUB, N_HEADS, C_CHUNK), lambda i: (at(i), 0, 0))
    st = pl.BlockSpec((1, N_HEADS, C_DK, C_DK), lambda i: (at(i), 0, 0, 0))
    return tok, vec, st


def _delta_fwd(q, k, v, g, beta):
    s = q.shape[1]
    n = s // (DELTA_SUB * C_CHUNK)
    tok, vec, st = _delta_specs(n, False)

    def body(q_ref, k_ref, v_ref, g_ref, b_ref, o_ref, st_ref, state):
        @pl.when(pl.program_id(0) == 0)
        def _():
            state[...] = jnp.zeros(state.shape, F32)

        cur = state[...]
        st_ref[0] = cur
        new, o = _delta_chunks(cur, q_ref[...], k_ref[...], v_ref[...], g_ref[...], b_ref[...])
        o_ref[...] = o
        state[...] = new

    return pl.pallas_call(
        body, name="delta_fwd", grid=(n,), in_specs=[tok, tok, tok, vec, vec], out_specs=[tok, st],
        out_shape=[jax.ShapeDtypeStruct((N_HEADS, s, C_DK), F32), jax.ShapeDtypeStruct((n, N_HEADS, C_DK, C_DK), F32)],
        scratch_shapes=[pltpu.VMEM((N_HEADS, C_DK, C_DK), F32)],
        compiler_params=_params(("arbitrary",)),
    )(q, k, v, g, beta)


def _delta_bwd(q, k, v, g, beta, states, do):
    s = q.shape[1]
    n = s // (DELTA_SUB * C_CHUNK)
    tok, vec, st = _delta_specs(n, True)

    def body(q_ref, k_ref, v_ref, g_ref, b_ref, st_ref, do_ref, dq_ref, dk_ref, dv_ref, dg_ref, db_ref, dstate):
        @pl.when(pl.program_id(0) == 0)
        def _():
            dstate[...] = jnp.zeros(dstate.shape, F32)

        _, vjp = jax.vjp(_delta_chunks, st_ref[0], q_ref[...], k_ref[...], v_ref[...], g_ref[...], b_ref[...])
        dst, dq, dk, dv, dg, db = vjp((dstate[...], do_ref[...]))
        dq_ref[...], dk_ref[...], dv_ref[...] = dq, dk, dv
        dg_ref[...], db_ref[...] = dg, db
        dstate[...] = dst

    tok_shape = jax.ShapeDtypeStruct((N_HEADS, s, C_DK), F32)
    vec_shape = jax.ShapeDtypeStruct((n * DELTA_SUB, N_HEADS, C_CHUNK), F32)
    return pl.pallas_call(
        body, name="delta_bwd", grid=(n,), in_specs=[tok, tok, tok, vec, vec, st, tok],
        out_specs=[tok, tok, tok, vec, vec], out_shape=[tok_shape, tok_shape, tok_shape, vec_shape, vec_shape],
        scratch_shapes=[pltpu.VMEM((N_HEADS, C_DK, C_DK), F32)],
        compiler_params=_params(("arbitrary",)),
    )(q, k, v, g, beta, states, do)


ATT_SCALE = B_QK ** -0.5


SCORE_SCALE_LOG2 = ATT_SCALE * math.log2(math.e)


def _lanes(x, n):
    return x if n == 1 else jnp.concatenate([x] * n, axis=1)


def _scores(a_ref, b_ref):
    return lax.dot_general(a_ref[...], b_ref[...], (((1,), (1,)), ((), ())), preferred_element_type=F32) * SCORE_SCALE_LOG2


def _flash_fwd(q, k, v, tb):
    s = q.shape[0]
    tb = min(tb, s)
    nb = s // tb
    nrep = tb // 128

    def body(q_ref, k_ref, v_ref, o_ref, lse_ref, m_s, l_s, acc):
        i, j = pl.program_id(1), pl.program_id(2)

        @pl.when(j == 0)
        def _():
            m_s[...] = jnp.full(m_s.shape, -jnp.inf, F32)
            l_s[...] = jnp.zeros(l_s.shape, F32)
            acc[...] = jnp.zeros(acc.shape, F32)

        def step(masked):
            sc = _scores(q_ref, k_ref)
            if masked:
                sc = jnp.where(_iota((tb, tb), 1) <= _iota((tb, tb), 0), sc, -jnp.inf)
            m_prev = m_s[...]
            m_new = jnp.maximum(m_prev, jnp.max(sc, axis=1, keepdims=True))
            alpha = jnp.exp2(m_prev - m_new)
            p = jnp.exp2(sc - _lanes(m_new, nrep))
            acc[...] = alpha * acc[...] + jnp.dot(p.astype(BF16), v_ref[...], preferred_element_type=F32)
            m_s[...] = m_new

        @pl.when(j < i)
        def _():
            step(False)

        @pl.when(j == i)
        def _():
            step(True)
            total = acc[...]
            l_s[...] = jnp.broadcast_to(total[:, B_VDIM:B_VDIM + 1], l_s.shape)
            o_ref[...] = total / l_s[...]
            lse_ref[...] = m_s[...] + jnp.log2(l_s[...])

    qs = pl.BlockSpec((tb, HEAD_PAD), lambda h, i, j: (i, h))
    ks = pl.BlockSpec((tb, HEAD_PAD), lambda h, i, j: (jnp.minimum(j, i), h))
    shape = jax.ShapeDtypeStruct((s, N_HEADS * HEAD_PAD), F32)
    return pl.pallas_call(
        body, name="flash_fwd", grid=(N_HEADS, nb, nb), in_specs=[qs, ks, ks],
        out_specs=[qs, qs], out_shape=[shape, shape],
        scratch_shapes=[pltpu.VMEM((tb, 128), F32), pltpu.VMEM((tb, 128), F32), pltpu.VMEM((tb, HEAD_PAD), F32)],
        compiler_params=_params(("parallel", "parallel", "arbitrary")),
    )(q, k, v)


def _f_attn_delta(o, do):
    return (_head_sums(o * do, HEAD_PAD),)


def _row_stats(rep):
    return rep[:, ::HEAD_PAD].T.reshape(N_HEADS, 1, -1)


def _flash_bwd(q, k, v, do, lse_row, delta_row, tb):
    s = q.shape[0]
    tb = min(tb, s)
    nb = s // tb

    def body(q_ref, k_ref, v_ref, do_ref, lse_ref, dl_ref, dq_ref, dk_ref, dv_ref, dk_acc, dv_acc):
        j, i = pl.program_id(1), pl.program_id(2)

        @pl.when((i == 0) & (j == 0))
        def _():
            dq_ref[...] = jnp.zeros(dq_ref.shape, F32)

        @pl.when(i == 0)
        def _():
            dk_acc[...] = jnp.zeros(dk_acc.shape, F32)
            dv_acc[...] = jnp.zeros(dv_acc.shape, F32)

        def step(masked):
            st = _scores(k_ref, q_ref)
            if masked:
                st = jnp.where(_iota((tb, tb), 0) <= _iota((tb, tb), 1), st, -jnp.inf)
            do = do_ref[...].astype(BF16)
            pt = jnp.exp2(st - lse_ref[0])
            dpt = lax.dot_general(v_ref[...], do, (((1,), (1,)), ((), ())), preferred_element_type=F32)
            dst = (pt * (dpt - dl_ref[0])).astype(BF16)
            dv_acc[...] += jnp.dot(pt.astype(BF16), do, preferred_element_type=F32)
            dk_acc[...] += jnp.dot(dst, q_ref[...], preferred_element_type=F32)
            rows = pl.ds(pl.multiple_of(i * tb, tb), tb)
            dq_ref[rows, :] += lax.dot_general(dst, k_ref[...], (((0,), (0,)), ((), ())),
                                               preferred_element_type=F32) * ATT_SCALE

        @pl.when(i == j)
        def _():
            step(True)

        @pl.when(i > j)
        def _():
            step(False)

        @pl.when(i == nb - 1)
        def _():
            dk_ref[...] = dk_acc[...] * ATT_SCALE
            dv_ref[...] = dv_acc[...]

    qs = pl.BlockSpec((tb, HEAD_PAD), lambda h, j, i: (jnp.maximum(i, j), h))
    ks = pl.BlockSpec((tb, HEAD_PAD), lambda h, j, i: (j, h))
    rs = pl.BlockSpec((1, 1, tb), lambda h, j, i: (h, 0, jnp.maximum(i, j)))
    shape = jax.ShapeDtypeStruct((s, N_HEADS * HEAD_PAD), F32)
    return pl.pallas_call(
        body, name="flash_bwd", grid=(N_HEADS, nb, nb), in_specs=[qs, ks, ks, qs, rs, rs],
        out_specs=[pl.BlockSpec((s, HEAD_PAD), lambda h, j, i: (0, h)), ks, ks], out_shape=[shape, shape, shape],
        scratch_shapes=[pltpu.VMEM((tb, HEAD_PAD), F32), pltpu.VMEM((tb, HEAD_PAD), F32)],
        compiler_params=_params(("parallel", "arbitrary", "arbitrary")),
    )(q, k, v, do, lse_row, delta_row)


def _loss_head(y, target, t):
    s, d = y.shape
    t = min(t, s)

    def body(y_ref, t_ref, sum_ref, dy_ref):
        @pl.when(pl.program_id(0) == 0)
        def _():
            sum_ref[...] = jnp.zeros(sum_ref.shape, F32)

        err = y_ref[...] - t_ref[...]
        dy_ref[...] = err * (1.0 / d)
        sum_ref[...] += jnp.broadcast_to(jnp.sum(err * err), sum_ref.shape)

    return pl.pallas_call(
        body, name="loss_head", grid=(s // t,),
        in_specs=[pl.BlockSpec((t, d), lambda i: (i, 0))] * 2,
        out_specs=[pl.BlockSpec((1, 128), lambda i: (0, 0)), pl.BlockSpec((t, d), lambda i: (i, 0))],
        out_shape=[jax.ShapeDtypeStruct((1, 128), F32), jax.ShapeDtypeStruct((s, d), F32)],
        compiler_params=_params(("arbitrary",)),
    )(y, target)


def _pad_to(a, axis, size):
    pad = [(0, 0)] * a.ndim
    pad[axis] = (0, size - a.shape[axis])
    return jnp.pad(a, pad)


W_IN_GROUPS = {
    "a": [(0, O_QLAT)],
    "b": [(O_KVLAT, O_KROPE), B_NOPE, (O_KROPE, O_CQKV), HEAD_PAD - B_QK, (O_QLAT, O_KVLAT)],
    "c": [(O_CQKV, O_GATES), W_C - (O_GATES - O_CQKV)],
    "g": [(O_GATES, D_IN)],
}
W_IN_SHARD = D_IN // N_DEV
W_IN_ROWS = 128


def _group_width(key):
    return sum(e if isinstance(e, int) else e[1] - e[0] for e in W_IN_GROUPS[key])


def _assemble_w_in(shards):
    depth = shards.shape[1]
    keys = list(W_IN_GROUPS)

    def body(s_ref, *o_refs):
        vals = [s_ref[k, 0] for k in range(N_DEV)]
        for key, o_ref in zip(keys, o_refs):
            parts = []
            for e in W_IN_GROUPS[key]:
                if isinstance(e, int):
                    parts.append(jnp.zeros((W_IN_ROWS, e), shards.dtype))
                    continue
                lo, hi = e
                while lo < hi:
                    k = lo // W_IN_SHARD
                    end = min(hi, (k + 1) * W_IN_SHARD)
                    parts.append(vals[k][:, lo - k * W_IN_SHARD:end - k * W_IN_SHARD])
                    lo = end
            o_ref[0] = parts[0] if len(parts) == 1 else jnp.concatenate(parts, axis=1)

    return pl.pallas_call(
        body, name="assemble_w_in", grid=(depth, D_MODEL // W_IN_ROWS),
        in_specs=[pl.BlockSpec((N_DEV, 1, W_IN_ROWS, W_IN_SHARD), lambda l, i: (0, l, i, 0))],
        out_specs=[pl.BlockSpec((1, W_IN_ROWS, _group_width(key)), lambda l, i: (l, i, 0)) for key in keys],
        out_shape=[jax.ShapeDtypeStruct((depth, D_MODEL, _group_width(key)), shards.dtype) for key in keys],
        compiler_params=_params(("parallel", "parallel")),
    )(shards)


def _split_dw_in(groups):
    keys = list(W_IN_GROUPS)
    depth = groups[0].shape[0]
    runs = []
    for gi, key in enumerate(keys):
        col = 0
        for e in W_IN_GROUPS[key]:
            if not isinstance(e, int):
                runs.append((e[0], e[1], gi, col))
            col += e if isinstance(e, int) else e[1] - e[0]
    runs.sort()

    def body(*refs):
        vals = [r[0] for r in refs[:len(keys)]]
        o_ref = refs[len(keys)]
        for k in range(N_DEV):
            lo, hi = k * W_IN_SHARD, (k + 1) * W_IN_SHARD
            parts = []
            for a, b, gi, col in runs:
                s, e = max(a, lo), min(b, hi)
                if s < e:
                    parts.append(vals[gi][:, col + s - a:col + e - a])
            o_ref[k, 0] = jnp.concatenate(parts, axis=1).astype(GRAD_WIRE)

    return pl.pallas_call(
        body, name="split_dw_in", grid=(depth, D_MODEL // W_IN_ROWS),
        in_specs=[pl.BlockSpec((1, W_IN_ROWS, _group_width(key)), lambda l, i: (l, i, 0)) for key in keys],
        out_specs=pl.BlockSpec((N_DEV, 1, W_IN_ROWS, W_IN_SHARD), lambda l, i: (0, l, i, 0)),
        out_shape=jax.ShapeDtypeStruct((N_DEV, depth, D_MODEL, W_IN_SHARD), GRAD_WIRE),
        compiler_params=_params(("parallel", "parallel")),
    )(*groups)


def _prep_layer(w, l):
    p = {}
    for key in W_IN_GROUPS:
        val = w["w_in_" + key][l].astype(BF16)
        p["w_" + key] = val
        p["wt_" + key] = val.T
    for name in ("norm1_g", "sgu_norm_g", "q_lat_norm_g", "kv_lat_norm_g", "o_norm_g", "norm2_g"):
        p[name] = w[name][l][None, :]
    p["w_spatial"], p["b_spatial"] = w["w_spatial"][l], w["b_spatial"][l]
    p["wq"] = _pad_to(w["w_q_up"][l].astype(F32).reshape(B_Q_LORA, N_HEADS, B_QK), 2, HEAD_PAD).reshape(B_Q_LORA, -1)
    kv = w["w_kv_up"][l].astype(F32).reshape(B_KV_LORA, N_HEADS, B_NOPE + B_VDIM)
    p["wk"] = _pad_to(kv[:, :, :B_NOPE], 2, HEAD_PAD).reshape(B_KV_LORA, -1)
    p["wv"] = _pad_to(kv[:, :, B_NOPE:], 2, HEAD_PAD).reshape(B_KV_LORA, -1)
    p["qn_g"] = _pad_to(w["q_norm_g"][l][None, :], 1, HEAD_PAD)
    p["kn_g"] = _pad_to(w["k_norm_g"][l][None, :], 1, HEAD_PAD)
    p["conv_w"] = _pad_to(w["conv_w"][l], 0, 8)
    row = lambda v: jnp.pad(v[None, :], ((0, 0), (N_HEADS, HEAD_PAD - 2 * N_HEADS)))
    p["a_log"], p["dt_bias"] = row(w["a_log"][l]), row(w["dt_bias"][l])
    wb = w["w_branch"][l]
    wb1 = _pad_to(wb[1].reshape(N_HEADS, B_VDIM, D_MODEL), 1, HEAD_PAD).reshape(-1, D_MODEL)
    for key, val in (("wb0", wb[0]), ("wb1", wb1), ("wb2", wb[2]), ("w_out", w["w_out"][l]),
                     ("w_ff1", w["w_ff1"][l]), ("w_ff2", w["w_ff2"][l])):
        p[key] = val.astype(BF16)
        p[key + "_t"] = val.T.astype(BF16)
    return p


def _unprep_grads(g):
    out = {"w_in_" + key: g["w_" + key] for key in W_IN_GROUPS}
    for name in ("norm1_g", "sgu_norm_g", "q_lat_norm_g", "kv_lat_norm_g", "o_norm_g", "norm2_g"):
        out[name] = g[name][0]
    out["w_spatial"], out["b_spatial"] = g["w_spatial"], g["b_spatial"]
    out["w_q_up"] = g["wq"].reshape(B_Q_LORA, N_HEADS, HEAD_PAD)[:, :, :B_QK].reshape(B_Q_LORA, -1)
    gk = g["wk"].reshape(B_KV_LORA, N_HEADS, HEAD_PAD)[:, :, :B_NOPE]
    gv = g["wv"].reshape(B_KV_LORA, N_HEADS, HEAD_PAD)[:, :, :B_VDIM]
    out["w_kv_up"] = jnp.concatenate([gk, gv], axis=2).reshape(B_KV_LORA, -1)
    out["q_norm_g"], out["k_norm_g"] = g["qn_g"][0, :B_QK], g["kn_g"][0, :B_QK]
    out["conv_w"] = g["conv_w"][:4]
    out["a_log"], out["dt_bias"] = g["a_log"][0, N_HEADS:2 * N_HEADS], g["dt_bias"][0, N_HEADS:2 * N_HEADS]
    gb1 = g["wb1"].reshape(N_HEADS, HEAD_PAD, D_MODEL)[:, :B_VDIM].reshape(-1, D_MODEL)
    out["w_branch"] = jnp.stack([g["wb0"], gb1, g["wb2"]], axis=0)
    out["w_out"], out["w_ff1"], out["w_ff2"] = g["w_out"], g["w_ff1"], g["w_ff2"]
    return out


def _heads_first(a):
    s = a.shape[0]
    return a.reshape(s, N_HEADS, C_DK).transpose(1, 0, 2)


def _heads_last(a):
    return a.transpose(1, 0, 2).reshape(a.shape[1], N_HEADS * C_DK)


def _chunk_vec(a):
    return a.reshape(-1, C_CHUNK, N_HEADS).transpose(0, 2, 1)


def _unchunk_vec(a):
    return a.transpose(0, 2, 1).reshape(-1, N_HEADS)


def _layer_fwd(x, p, tabs, t, tb):
    sv = {"x": x}
    h1, = _local_fwd("norm1", _f_norm, [(x, D_MODEL, 0)], [p["norm1_g"]], [(D_MODEL, BF16)], t)
    sv["h1"] = h1
    p_a, p_b, p_c = (_matmul("proj_" + key, h1, p["w_" + key]) for key in "abc")
    p_g = _matmul("proj_g", h1, p["w_g"], out_dtype=BF16)
    sv.update(p_a=p_a, p_b=p_b, p_c=p_c, p_g=p_g)
    y_a, = _local_fwd("sgu", _f_sgu, [(p_a, 2 * A_WIDTH, 0)], [p["sgu_norm_g"], p["w_spatial"], p["b_spatial"]],
                      [(A_WIDTH, BF16)], SGU_CHUNK)
    q, k, v = _local_fwd("mla_prep", _f_mla_prep, [(p_b, W_B, 0)] + [(tb_, HEAD_PAD, 0) for tb_ in tabs],
                         [p["q_lat_norm_g"], p["kv_lat_norm_g"], p["wq"], p["wk"], p["wv"], p["qn_g"], p["kn_g"]],
                         [(N_HEADS * HEAD_PAD, BF16)] * 3, min(t, 256))
    o_b, lse = _flash_fwd(q, k, v, tb)
    sv.update(q=q, k=k, v=v, o_b=o_b, lse=lse)
    conv_pre = _conv_fwd(p_c, p["conv_w"], t)
    cq, ck, cv, gb = _local_fwd("gdn_pre", _f_gdn_pre, [(conv_pre, C_QKV, 0), (p_c, HEAD_PAD, (C_QKV + C_Z) // HEAD_PAD)],
                                [p["a_log"], p["dt_bias"]], [(A_WIDTH, F32)] * 3 + [(HEAD_PAD, F32)], t)
    cq, ck, cv = _heads_first(cq), _heads_first(ck), _heads_first(cv)
    beta, g = _chunk_vec(gb[:, :N_HEADS]), _chunk_vec(gb[:, N_HEADS:2 * N_HEADS])
    o_c, states = _delta_fwd(cq, ck, cv, g, beta)
    o_c = _heads_last(o_c)
    sv.update(conv_pre=conv_pre, cq=cq, ck=ck, cv=cv, beta=beta, g=g, states=states, o_c=o_c)
    y_c, = _local_fwd("gdn_post", _f_gdn_post, [(o_c, A_WIDTH, 0), (p_c, C_Z, C_QKV // C_Z)], [p["o_norm_g"]],
                      [(A_WIDTH, BF16)], t)
    y0 = _matmul("branch0", y_a, p["wb0"], out_dtype=BF16)
    y1 = _matmul("branch1", o_b, p["wb1"], out_dtype=BF16)
    y2 = _matmul("branch2", y_c, p["wb2"], out_dtype=BF16)
    merged, = _local_fwd("merge", _f_merge, [(p_g, 3 * D_MODEL, 0), (y0, D_MODEL, 0), (y1, D_MODEL, 0), (y2, D_MODEL, 0)],
                         [], [(D_MODEL, BF16)], t)
    x1 = _matmul("out_proj", merged, p["w_out"], add=x)
    sv.update(y_a=y_a, y_c=y_c, y0=y0, y1=y1, y2=y2, merged=merged, x1=x1)
    h2, = _local_fwd("norm2", _f_norm, [(x1, D_MODEL, 0)], [p["norm2_g"]], [(D_MODEL, BF16)], t)
    a, r = _matmul("ff1", h2, p["w_ff1"], epilogue=lambda acc: (acc, jnp.square(jnp.maximum(acc, 0.0))),
                   out_dtypes=(BF16, BF16))
    x2 = _matmul("ff2", r, p["w_ff2"], add=x1)
    sv.update(h2=h2, a=a, r=r)
    return x2, sv


def _layer_bwd(dx2, sv, p, tabs, t, tb):
    g = {}
    da = _matmul("d_ff2", dx2, p["w_ff2_t"], extras=(sv["a"],), out_dtypes=(BF16,),
                 epilogue=lambda dr, a: (dr * (2.0 * jnp.maximum(a.astype(F32), 0.0)),))
    g["w_ff2"] = _matmul_tn("dw_ff2", sv["r"], dx2)
    dh2 = _matmul("d_ff1", da, p["w_ff1_t"])
    g["w_ff1"] = _matmul_tn("dw_ff1", sv["h2"], da)
    (dx1,), (g["norm2_g"],) = _local_bwd("norm2_bwd", _f_norm_res, [(sv["x1"], D_MODEL, 0)], [p["norm2_g"]],
                                         [dh2, dx2], t, [True], [True])
    dmerged = _matmul("d_out_proj", dx1, p["w_out_t"])
    g["w_out"] = _matmul_tn("dw_out", sv["merged"], dx1)
    (dp_g, dy0, dy1, dy2), _ = _local_bwd(
        "merge_bwd", _f_merge, [(sv["p_g"], 3 * D_MODEL, 0), (sv["y0"], D_MODEL, 0), (sv["y1"], D_MODEL, 0),
                                (sv["y2"], D_MODEL, 0)], [], [dmerged], min(t, 256), [True] * 4, [], [BF16] * 4)
    dy_a = _matmul("d_branch0", dy0, p["wb0_t"])
    do_b = _matmul("d_branch1", dy1, p["wb1_t"])
    dy_c = _matmul("d_branch2", dy2, p["wb2_t"])
    g["wb0"] = _matmul_tn("dw_branch0", sv["y_a"], dy0)
    g["wb1"] = _matmul_tn("dw_branch1", sv["o_b"], dy1)
    g["wb2"] = _matmul_tn("dw_branch2", sv["y_c"], dy2)
    p_c = sv["p_c"]
    (do_c, dc_z), (g["o_norm_g"],) = _local_bwd(
        "gdn_post_bwd", _f_gdn_post, [(sv["o_c"], A_WIDTH, 0), (p_c, C_Z, C_QKV // C_Z)], [p["o_norm_g"]], [dy_c], t,
        [True, True], [True], [F32, BF16])
    dcq, dck, dcv, dg, dbeta = _delta_bwd(sv["cq"], sv["ck"], sv["cv"], sv["g"], sv["beta"], sv["states"],
                                          _heads_first(do_c))
    dgb = jnp.pad(jnp.concatenate([_unchunk_vec(dbeta), _unchunk_vec(dg)], axis=1),
                  ((0, 0), (0, HEAD_PAD - 2 * N_HEADS)))
    (dconv, dba), (g["a_log"], g["dt_bias"]) = _local_bwd(
        "gdn_pre_bwd", _f_gdn_pre, [(sv["conv_pre"], C_QKV, 0), (p_c, HEAD_PAD, (C_QKV + C_Z) // HEAD_PAD)],
        [p["a_log"], p["dt_bias"]], [_heads_last(dcq), _heads_last(dck), _heads_last(dcv), dgb], t,
        [True, True], [True, True], [F32, BF16])
    dc_qkv, g["conv_w"] = _conv_bwd(p_c, dconv, p["conv_w"], t)
    dp_c = jnp.concatenate([dc_qkv, dc_z, dba], axis=1)
    delta, = _local_fwd("attn_delta", _f_attn_delta, [(sv["o_b"], N_HEADS * HEAD_PAD, 0), (do_b, N_HEADS * HEAD_PAD, 0)], [],
                        [(N_HEADS * HEAD_PAD, F32)], t)
    dq, dk, dv = _flash_bwd(sv["q"], sv["k"], sv["v"], do_b, _row_stats(sv["lse"]), _row_stats(delta), tb)
    mla_pars = [p["q_lat_norm_g"], p["kv_lat_norm_g"], p["wq"], p["wk"], p["wv"], p["qn_g"], p["kn_g"]]
    (dp_b,), mla_g = _local_bwd(
        "mla_prep_bwd", _f_mla_prep, [(sv["p_b"], W_B, 0)] + [(tb_, HEAD_PAD, 0) for tb_ in tabs], mla_pars,
        [dq, dk, dv], min(t, 256), [True, False, False, False], [True] * 7, [BF16])
    for name, val in zip(("q_lat_norm_g", "kv_lat_norm_g", "wq", "wk", "wv", "qn_g", "kn_g"), mla_g):
        g[name] = val
    (dp_a,), (g["sgu_norm_g"], g["w_spatial"], g["b_spatial"]) = _local_bwd(
        "sgu_bwd", _f_sgu, [(sv["p_a"], 2 * A_WIDTH, 0)], [p["sgu_norm_g"], p["w_spatial"], p["b_spatial"]], [dy_a],
        SGU_CHUNK, [True], [True] * 3, [BF16])
    dh1 = None
    for key, dp in (("a", dp_a), ("b", dp_b), ("c", dp_c), ("g", dp_g)):
        dh1 = _matmul("d_proj_" + key, dp, p["wt_" + key], add=dh1)
        g["w_" + key] = _matmul_tn("dw_proj_" + key, sv["h1"], dp)
    (dx,), (g["norm1_g"],) = _local_bwd("norm1_bwd", _f_norm_res, [(sv["x"], D_MODEL, 0)], [p["norm1_g"]],
                                        [dh1, dx1], t, [True], [True])
    return dx, _unprep_grads(g)


def _local_step(x, positions, w, target, t=512, tb=1024):
    s = x.shape[0]
    t = min(t, s)
    w = dict(w)
    for key, val in zip(W_IN_GROUPS, _assemble_w_in(w["w_in"])):
        w["w_in_" + key] = val
    half = B_ROPE // 2
    inv_freq = 1.0 / (ROPE_BASE ** (jnp.arange(half, dtype=F32) / half))
    inv_row = jnp.concatenate([jnp.zeros((B_NOPE,), F32), inv_freq, inv_freq, jnp.zeros((HEAD_PAD - B_QK,), F32)])[None, :]
    tabs = _local_fwd("rope_tables", _f_rope_tables, [(positions, 1, 0)], [inv_row], [(HEAD_PAD, F32)] * 3, t)
    preps, saved = [], []
    for l in range(DEPTH):
        preps.append(_prep_layer(w, l))
        x, sv = _layer_fwd(x, preps[l], tabs, t, tb)
        saved.append(sv)
    sq, dx = _loss_head(x, target, t)
    grads = [None] * DEPTH
    for l in reversed(range(DEPTH)):
        dx, grads[l] = _layer_bwd(dx, saved[l], preps[l], tabs, t, tb)
    stacked = lambda name: jnp.stack([grads[l][name] for l in range(DEPTH)], axis=0)
    out = {name: stacked(name) for name in WEIGHTS if name != "w_in"}
    out["w_in"] = _split_dw_in([stacked("w_in_" + key) for key in W_IN_GROUPS])
    return sq, dx, out


def _exchange(name, arrays, gather):
    n = len(arrays)

    def body(*refs):
        send, recv = refs[:n], refs[n:2 * n]
        send_sems, recv_sems, local_sems = refs[2 * n:]
        x, y, c = lax.axis_index("x"), lax.axis_index("y"), lax.axis_index("c")
        me = 4 * x + 2 * y + c

        def src(a, idx):
            return send[a] if gather[a] else send[a].at[idx]

        local = [pltpu.make_async_copy(src(a, me), recv[a].at[me], local_sems.at[a]) for a in range(n)]
        for cp in local:
            cp.start()
        copies = []
        for d in range(1, N_DEV):
            px, py, pc = x ^ ((d >> 2) & 1), y ^ ((d >> 1) & 1), c ^ (d & 1)
            peer = 4 * px + 2 * py + pc
            for a in range(n):
                cp = pltpu.make_async_remote_copy(
                    src_ref=src(a, peer), dst_ref=recv[a].at[me], send_sem=send_sems.at[a, d],
                    recv_sem=recv_sems.at[a, d], device_id=(px, py, pc), device_id_type=pl.DeviceIdType.MESH)
                cp.start()
                copies.append((cp, a, peer, d))
        for cp, a, peer, d in copies:
            cp.wait_send()
            pltpu.make_async_remote_copy(
                src_ref=src(a, peer), dst_ref=recv[a].at[peer], send_sem=send_sems.at[a, d],
                recv_sem=recv_sems.at[a, d], device_id=(x, y, c), device_id_type=pl.DeviceIdType.MESH).wait_recv()
        for cp in local:
            cp.wait()

    any_spec = pl.BlockSpec(memory_space=pl.ANY)
    return pl.pallas_call(
        body, name=name, in_specs=[any_spec] * n, out_specs=[any_spec] * n,
        out_shape=[jax.ShapeDtypeStruct(((N_DEV,) + a.shape) if gather[i] else a.shape, a.dtype)
                   for i, a in enumerate(arrays)],
        scratch_shapes=[pltpu.SemaphoreType.DMA((n, N_DEV)), pltpu.SemaphoreType.DMA((n, N_DEV)),
                        pltpu.SemaphoreType.DMA((n,))],
    )(*arrays)


REDUCE_BLOCK_ELEMS = 64 * 1024


def _reduce_adamw(name, recv, w, m, v):
    rows, cols = w.shape
    tr = rows
    while tr % 16 == 0 and tr * (-(-cols // 128) * 128) > REDUCE_BLOCK_ELEMS:
        tr //= 2
    c1, c2 = 1.0 - ADAM_B1 ** ADAM_STEP, 1.0 - ADAM_B2 ** ADAM_STEP

    def body(r_ref, w_ref, m_ref, v_ref, g_ref, d_ref, nm_ref, nv_ref):
        g = r_ref[0].astype(F32)
        for j in range(1, N_DEV):
            g = g + r_ref[j].astype(F32)
        m_new = ADAM_B1 * m_ref[...] + (1.0 - ADAM_B1) * g
        v_new = ADAM_B2 * v_ref[...] + (1.0 - ADAM_B2) * jnp.square(g)
        d_ref[...] = -ADAM_LR * ((m_new / c1) / (jnp.sqrt(v_new / c2) + ADAM_EPS) + ADAM_WD * w_ref[...])
        g_ref[...], nm_ref[...], nv_ref[...] = g, m_new, v_new

    flat = pl.BlockSpec((tr, cols), lambda i: (i, 0))
    return pl.pallas_call(
        body, name=name, grid=(rows // tr,),
        in_specs=[pl.BlockSpec((N_DEV, tr, cols), lambda i: (0, i, 0)), flat, flat, flat], out_specs=[flat] * 4,
        out_shape=[jax.ShapeDtypeStruct((rows, cols), F32)] * 4, compiler_params=_params(("parallel",)),
    )(recv, w, m, v)


PACK_ROWS = 512


def _pack(cols):
    flat = jnp.concatenate(cols, axis=-1)
    tile = PACK_ROWS * 128
    flat = _pad_to(flat, flat.ndim - 1, -(-flat.shape[-1] // tile) * tile)
    return flat.reshape(flat.shape[:-1] + (-1, 128))


def _unpack(packed, shapes, lead=()):
    flat = packed.reshape(lead + (-1,))
    out, off = [], 0
    for shp in shapes:
        n = math.prod(shp)
        out.append(flat[..., off:off + n].reshape(lead + tuple(shp)))
        off += n
    return out


def _to_shards(name, full):
    ax = SHARD_AXIS[name]
    shp = full.shape
    return jnp.moveaxis(full.reshape(shp[:ax] + (N_DEV, shp[ax] // N_DEV) + shp[ax + 1:]), ax, 0)


def _from_shards(name, shards):
    ax = SHARD_AXIS[name]
    a = jnp.moveaxis(shards, 0, ax)
    return a.reshape(a.shape[:ax] + (a.shape[ax] * a.shape[ax + 1],) + a.shape[ax + 2:])


def kernel(x, positions, norm1_g, w_in, sgu_norm_g, w_spatial, b_spatial, q_lat_norm_g, w_q_up, kv_lat_norm_g, w_kv_up, q_norm_g, k_norm_g, conv_w, a_log, dt_bias, o_norm_g, w_branch, w_out, norm2_g, w_ff1, w_ff2, loss_target, m_norm1_g, m_w_in, m_sgu_norm_g, m_w_spatial, m_b_spatial, m_q_lat_norm_g, m_w_q_up, m_kv_lat_norm_g, m_w_kv_up, m_q_norm_g, m_k_norm_g, m_conv_w, m_a_log, m_dt_bias, m_o_norm_g, m_w_branch, m_w_out, m_norm2_g, m_w_ff1, m_w_ff2, v_norm1_g, v_w_in, v_sgu_norm_g, v_w_spatial, v_b_spatial, v_q_lat_norm_g, v_w_q_up, v_kv_lat_norm_g, v_w_kv_up, v_q_norm_g, v_k_norm_g, v_conv_w, v_a_log, v_dt_bias, v_o_norm_g, v_w_branch, v_w_out, v_norm2_g, v_w_ff1, v_w_ff2):
    args = locals()
    local_w = {n: args[n] for n in WEIGHTS}
    state = (local_w, {n: args["m_" + n] for n in WEIGHTS}, {n: args["v_" + n] for n in WEIGHTS})
    wire = [local_w[n] if n in EXACT_GATHER else local_w[n].astype(BF16) for n in SHARDED]
    full = dict(local_w)
    for n, part in zip(SHARDED, _exchange("gather_weights", wire, [True] * len(wire))):
        full[n] = part if n == "w_in" else _from_shards(n, part)
    sq, grad_x, grads = _local_step(x[0], positions.reshape(-1, 1), full, loss_target[0])
    loss = lax.psum(sq[0, 0] * (0.5 / D_MODEL), ("x", "y", "c"))
    send = [grads[n] if n == "w_in" else _to_shards(n, grads[n]).astype(GRAD_WIRE) for n in SHARDED]
    rep = _pack([grads[n].reshape(-1) for n in REPLICATED])
    recv = _exchange("exchange_grads", send + [rep], [False] * len(send) + [True])
    results = {}
    for n, r in zip(SHARDED, recv):
        shp = local_w[n].shape
        flat = (math.prod(shp[:-1]), shp[-1])
        outs = _reduce_adamw("adamw_" + n, r.reshape((N_DEV,) + flat), *[src[n].reshape(flat) for src in state])
        results[n] = [o.reshape(shp) for o in outs]
    outs = _reduce_adamw("adamw_replicated", recv[-1],
                         *[_pack([src[n].reshape(-1) for n in REPLICATED]) for src in state])
    rep_shapes = [local_w[n].shape for n in REPLICATED]
    for n, vals in zip(REPLICATED, zip(*[_unpack(o, rep_shapes) for o in outs])):
        results[n] = vals
    return (loss, grad_x[None], *[results[n][k] for k in range(4) for n in WEIGHTS])
```
